```python
import math
import jax, jax.numpy as jnp
from jax import lax
import numpy as np

D_MODEL = 1024
BATCH = 8
SEQ = 2048
DEPTH = 1

GLA_WIDTH = D_MODEL // 2
DIFF_WIDTH = D_MODEL - GLA_WIDTH
GLA_HEADS = 4
GLA_DV = GLA_WIDTH // GLA_HEADS
GLA_DK = GLA_DV // 2
GLA_QK = GLA_HEADS * GLA_DK
GLA_GATE_RANK = 16
GLA_GATE_NORM = 16.0
GLA_CHUNK = 64
DIFF_HEADS = 4
DIFF_DV = DIFF_WIDTH // DIFF_HEADS
DIFF_DQK = DIFF_DV // 2
DIFF_QK = DIFF_HEADS * 2 * DIFF_DQK
Q_BLOCK = 128
ROPE_THETA = 10000.0
D_FF = 4 * D_MODEL
EPS = 1e-6
N_ADA = 6
PROJ_SIZES = (GLA_QK, GLA_QK, GLA_WIDTH, GLA_GATE_RANK, GLA_WIDTH, DIFF_QK, DIFF_QK, DIFF_WIDTH)
PROJ_WIDTH = sum(PROJ_SIZES)

kernel_name = "hybrid_gla_diffattn_parallel_block"


def rms_norm(t, g):
    tf = t.astype(jnp.float32)
    y = tf * lax.rsqrt(jnp.mean(tf * tf, axis=-1, keepdims=True) + EPS)
    return (y * g.astype(jnp.float32)).astype(t.dtype)


def rope_tables(positions, dim):
    inv_freq = 1.0 / (ROPE_THETA ** (jnp.arange(0, dim, 2, dtype=jnp.float32) / dim))
    ang = positions.astype(jnp.float32)[..., None] * inv_freq
    return jnp.cos(ang), jnp.sin(ang)


def apply_rope(t, cos, sin):
    half = t.shape[-1] // 2
    t1 = t[..., :half].astype(jnp.float32)
    t2 = t[..., half:].astype(jnp.float32)
    return jnp.concatenate([t1 * cos - t2 * sin, t2 * cos + t1 * sin], axis=-1).astype(t.dtype)


def to_heads(t, n_heads):
    b, s, _ = t.shape
    return t.reshape(b, s, n_heads, -1).transpose(0, 2, 1, 3)


def from_heads(t):
    b, h, s, d = t.shape
    return t.transpose(0, 2, 1, 3).reshape(b, s, h * d)


def gla_chunked(q, k, v, g_log):
    bsz, nh, seq, dk = q.shape
    dv = v.shape[-1]
    n_chunks = seq // GLA_CHUNK

    def chunked(t):
        return t.astype(jnp.float32).reshape(bsz, nh, n_chunks, GLA_CHUNK, t.shape[-1]).transpose(2, 0, 1, 3, 4)

    causal = jnp.tril(jnp.ones((GLA_CHUNK, GLA_CHUNK), dtype=bool))[:, :, None]

    def step(state, inp):
        qc, kc, vc, gc = inp
        b = jnp.cumsum(gc, axis=-2)
        o_inter = jnp.einsum('bhcd,bhde->bhce', qc * jnp.exp(b), state)
        rel = b[:, :, :, None, :] - b[:, :, None, :, :]
        decay = jnp.exp(jnp.where(causal, rel, -jnp.inf))
        scores = jnp.einsum('bhid,bhjd,bhijd->bhij', qc, kc, decay)
        o_intra = jnp.einsum('bhij,bhje->bhie', scores, vc)
        b_last = b[:, :, -1:, :]
        new_state = jnp.exp(b_last[:, :, 0, :])[..., None] * state + \
            jnp.einsum('bhcd,bhce->bhde', kc * jnp.exp(b_last - b), vc)
        return new_state, o_inter + o_intra

    state0 = jnp.zeros((bsz, nh, dk, dv), jnp.float32)
    _, o = lax.scan(step, state0, (chunked(q), chunked(k), chunked(v), chunked(g_log)))
    return o.transpose(1, 2, 0, 3, 4).reshape(bsz, nh, seq, dv).astype(v.dtype)


def diff_attention(q, k, v, lam):
    seq = q.shape[3]
    scale = DIFF_DQK ** -0.5
    vf = v.astype(jnp.float32)
    outs = []
    for blk in range(seq // Q_BLOCK):
        q0, q1 = blk * Q_BLOCK, (blk + 1) * Q_BLOCK
        qb = q[:, :, :, q0:q1]
        kb = k[:, :, :, :q1]
        s = jnp.einsum('bhmqd,bhmkd->bhmqk', qb, kb).astype(jnp.float32) * scale
        mask = (q0 + jnp.arange(Q_BLOCK))[:, None] >= jnp.arange(q1)[None, :]
        p = jax.nn.softmax(jnp.where(mask, s, -jnp.inf), axis=-1)
        p_diff = p[:, :, 0] - lam * p[:, :, 1]
        outs.append(jnp.einsum('bhqk,bhkd->bhqd', p_diff, vf[:, :, :q1]))
    return jnp.concatenate(outs, axis=2).astype(v.dtype)


def hybrid_mixer(h, cos5, sin5, w_in, gate_w, gate_b, gla_norm, lq1, lk1, lq2, lk2,
                 diff_norm, w_out, lambda_init):
    bsz, seq, _ = h.shape
    proj = h @ w_in
    offsets = np.cumsum(PROJ_SIZES)[:-1].tolist()
    g_q, g_k, g_v, g_lr, g_og, d_q, d_k, d_v = jnp.split(proj, offsets, axis=-1)

    gq = to_heads(g_q, GLA_HEADS) * (GLA_DK ** -0.5)
    gk = to_heads(g_k, GLA_HEADS)
    gv = to_heads(g_v, GLA_HEADS)
    g_log = jax.nn.log_sigmoid((g_lr @ gate_w + gate_b).astype(jnp.float32)) / GLA_GATE_NORM
    g_log = to_heads(g_log, GLA_HEADS)
    go = gla_chunked(gq, gk, gv, g_log)
    go = from_heads(rms_norm(go, gla_norm)) * jax.nn.silu(g_og)

    def qk_heads(t):
        return t.reshape(bsz, seq, DIFF_HEADS, 2, DIFF_DQK).transpose(0, 2, 3, 1, 4)
    dq = apply_rope(qk_heads(d_q), cos5, sin5)
    dk = apply_rope(qk_heads(d_k), cos5, sin5)
    dv = to_heads(d_v, DIFF_HEADS)
    lam = jnp.exp(jnp.sum(lq1.astype(jnp.float32) * lk1.astype(jnp.float32))) \
        - jnp.exp(jnp.sum(lq2.astype(jnp.float32) * lk2.astype(jnp.float32))) + lambda_init
    do = diff_attention(dq, dk, dv, lam)
    do = from_heads(rms_norm(do, diff_norm) * (1.0 - lambda_init))

    return jnp.concatenate([go, do], axis=-1) @ w_out


def setup_inputs(seed: int = 0) -> dict:
    key = jax.random.key(seed)
    ks = jax.random.split(key, 24)
    f32 = jnp.float32

    def nrm(k, shape, scale):
        return jax.random.normal(k, shape, f32) * scale

    def gain(k, shape):
        return 1.0 + 0.02 * jax.random.normal(k, shape, f32)

    offsets = jax.random.randint(ks[2], (BATCH, 1), 0, 1024, dtype=jnp.int32)
    positions = offsets + jnp.arange(SEQ, dtype=jnp.int32)[None, :]
    return {
        "x": nrm(ks[0], (BATCH, SEQ, D_MODEL), 1.0),
        "c": nrm(ks[1], (BATCH, D_MODEL), 1.0),
        "positions": positions,
        "ada_w": nrm(ks[3], (DEPTH, D_MODEL, N_ADA * D_MODEL), D_MODEL ** -0.5),
        "ada_b": nrm(ks[4], (DEPTH, N_ADA * D_MODEL), 0.02),
        "pre_norm_mix": gain(ks[5], (DEPTH, D_MODEL)),
        "post_norm_mix": gain(ks[6], (DEPTH, D_MODEL)),
        "w_in": nrm(ks[7], (DEPTH, D_MODEL, PROJ_WIDTH), D_MODEL ** -0.5),
        "gla_gate_w": nrm(ks[8], (DEPTH, GLA_GATE_RANK, GLA_QK), GLA_GATE_RANK ** -0.5),
        "gla_gate_b": nrm(ks[9], (DEPTH, GLA_QK), 0.1),
        "gla_norm": gain(ks[10], (DEPTH, GLA_DV)),
        "lambda_q1": nrm(ks[11], (DEPTH, DIFF_DQK), 0.1),
        "lambda_k1": nrm(ks[12], (DEPTH, DIFF_DQK), 0.1),
        "lambda_q2": nrm(ks[13], (DEPTH, DIFF_DQK), 0.1),
        "lambda_k2": nrm(ks[14], (DEPTH, DIFF_DQK), 0.1),
        "diff_norm": gain(ks[15], (DEPTH, DIFF_DV)),
        "w_out": nrm(ks[16], (DEPTH, D_MODEL, D_MODEL), D_MODEL ** -0.5),
        "pre_norm_mlp": gain(ks[17], (DEPTH, D_MODEL)),
        "post_norm_mlp": gain(ks[18], (DEPTH, D_MODEL)),
        "w_up": nrm(ks[19], (DEPTH, D_MODEL, D_FF), D_MODEL ** -0.5),
        "w_down": nrm(ks[20], (DEPTH, D_FF, D_MODEL), D_FF ** -0.5),
    }


def reference(x, c, positions, ada_w, ada_b, pre_norm_mix, post_norm_mix, w_in, gla_gate_w,
              gla_gate_b, gla_norm, lambda_q1, lambda_k1, lambda_q2, lambda_k2, diff_norm,
              w_out, pre_norm_mlp, post_norm_mlp, w_up, w_down):
    cos, sin = rope_tables(positions, DIFF_DQK)
    cos5, sin5 = cos[:, None, None], sin[:, None, None]
    c_act = jax.nn.silu(c)
    for l in range(DEPTH):
        lambda_init = 0.8 - 0.6 * math.exp(-0.3 * l)
        ada = c_act @ ada_w[l] + ada_b[l]
        sh_a, sc_a, gt_a, sh_m, sc_m, gt_m = [a[:, None, :] for a in jnp.split(ada, N_ADA, axis=-1)]

        h = rms_norm(x, pre_norm_mix[l]) * (1.0 + sc_a) + sh_a
        y = hybrid_mixer(h, cos5, sin5, w_in[l], gla_gate_w[l], gla_gate_b[l], gla_norm[l],
                         lambda_q1[l], lambda_k1[l], lambda_q2[l], lambda_k2[l], diff_norm[l],
                         w_out[l], lambda_init)
        x = x + gt_a * rms_norm(y, post_norm_mix[l])

        h = rms_norm(x, pre_norm_mlp[l]) * (1.0 + sc_m) + sh_m
        y = jnp.square(jax.nn.relu(h @ w_up[l])) @ w_down[l]
        x = x + gt_m * rms_norm(y, post_norm_mlp[l])
    return x
```

```python
import functools
import math

import jax
import jax.numpy as jnp
from jax import lax
from jax.experimental import pallas as pl
from jax.experimental.pallas import tpu as pltpu

F32 = jnp.float32
BF16 = jnp.bfloat16

GLA_HEADS = 4
GLA_DK = 64
GLA_DV = 128
GLA_QK = GLA_HEADS * GLA_DK
GLA_WIDTH = GLA_HEADS * GLA_DV
GLA_GATE_RANK = 16
GLA_GATE_NORM = 16.0
GLA_CHUNK = 64
DIFF_HEADS = 4
DIFF_DQK = 64
DIFF_DV = 128
DIFF_QK = DIFF_HEADS * 2 * DIFF_DQK
DIFF_WIDTH = DIFF_HEADS * DIFF_DV
ROPE_THETA = 10000.0
EPS = 1e-6
N_ADA = 6

LANES = 128
GATE_PAD = LANES
VMEM_LIMIT = 56 * 1024 * 1024

_COL_GQ = 0
_COL_GK = _COL_GQ + GLA_QK
_COL_GV = _COL_GK + GLA_QK
_COL_GOG = _COL_GV + GLA_WIDTH
_COL_DQ = _COL_GOG + GLA_WIDTH
_COL_DK = _COL_DQ + DIFF_QK
_COL_DV = _COL_DK + DIFF_QK
_COL_LR = _COL_DV + DIFF_WIDTH
_COL_END = _COL_LR + GATE_PAD


def _dot(a, b):
    return jnp.dot(a, b, preferred_element_type=F32)


def _dot_nt(a, b):
    return lax.dot_general(a, b, (((1,), (1,)), ((), ())), preferred_element_type=F32)


def _dot_tn(a, b):
    return lax.dot_general(a, b, (((0,), (0,)), ((), ())), preferred_element_type=F32)


def _rms(t):
    return t * lax.rsqrt(jnp.mean(t * t, axis=-1, keepdims=True) + EPS)


def _silu(t):
    return t * (1.0 / (1.0 + jnp.exp(-t)))


def _ada_kernel(c_ref, w_ref, b_ref, o_ref):
    ca = _silu(c_ref[...]).astype(BF16)
    o_ref[...] = _dot(ca, w_ref[...].astype(BF16)) + b_ref[...]


def _ada(c, ada_w, ada_b):
    bsz, d = c.shape
    n = ada_w.shape[1]
    tn = d
    return pl.pallas_call(
        _ada_kernel,
        grid=(n // tn,),
        in_specs=[
            pl.BlockSpec((bsz, d), lambda j: (0, 0)),
            pl.BlockSpec((d, tn), lambda j: (0, j)),
            pl.BlockSpec((1, tn), lambda j: (0, j)),
        ],
        out_specs=pl.BlockSpec((bsz, tn), lambda j: (0, j)),
        out_shape=jax.ShapeDtypeStruct((bsz, n), F32),
        compiler_params=pltpu.CompilerParams(
            dimension_semantics=("parallel",), vmem_limit_bytes=VMEM_LIMIT),
        name="ada_ln",
    )(c, ada_w, ada_b.reshape(1, n))


def _inproj_kernel(x_ref, ada_ref, pn_ref, cos_ref, sin_ref, w_ref, gw_ref, gb_ref,
                   gq_ref, gk_ref, gv_ref, gog_ref, glog_ref, dq_ref, dk_ref, dv_ref):
    x = x_ref[0]
    ada = ada_ref[0]
    h = _rms(x) * pn_ref[...] * (1.0 + ada[1:2]) + ada[0:1]
    hb = h.astype(BF16)

    def proj(lo, hi):
        return _dot(hb, w_ref[:, lo:hi])

    gq_ref[0] = (proj(_COL_GQ, _COL_GK) * (GLA_DK ** -0.5)).astype(BF16)
    gk_ref[0] = proj(_COL_GK, _COL_GV).astype(BF16)
    gv_ref[0] = proj(_COL_GV, _COL_GOG).astype(BF16)
    gog_ref[0] = proj(_COL_GOG, _COL_DQ).astype(BF16)
    dv_ref[0] = proj(_COL_DV, _COL_LR).astype(BF16)

    lr = proj(_COL_LR, _COL_END).astype(BF16)
    z = _dot(lr, gw_ref[...]) + gb_ref[...]
    log_sig = jnp.minimum(z, 0.0) - jnp.log1p(jnp.exp(-jnp.abs(z)))
    glog_ref[0] = log_sig * (1.0 / GLA_GATE_NORM)

    cos = cos_ref[0]
    sin = sin_ref[0]
    lane = lax.broadcasted_iota(jnp.int32, cos.shape, 1)
    first_half = (lane % (2 * 32)) < 32

    def rope_store(out_ref, lo, scale):
        t = proj(lo, lo + DIFF_QK)
        for c in range(DIFF_QK // LANES):
            tc = t[:, c * LANES:(c + 1) * LANES]
            partner = jnp.where(first_half, pltpu.roll(tc, LANES - 32, 1), pltpu.roll(tc, 32, 1))
            out_ref[0, :, c * LANES:(c + 1) * LANES] = ((tc * cos + partner * sin) * scale).astype(BF16)

    rope_store(dq_ref, _COL_DQ, DIFF_DQK ** -0.5)
    rope_store(dk_ref, _COL_DK, 1.0)


def _inproj(x, ada3, pre_norm, cos_t, sin_t, w_pack, gate_w_pad, gate_b, tm):
    bsz, seq, d = x.shape
    grid = (bsz, seq // tm)
    row = lambda b, i: (b, i, 0)
    const2 = lambda b, i: (0, 0)

    def out(width, dtype):
        return (pl.BlockSpec((1, tm, width), row), jax.ShapeDtypeStruct((bsz, seq, width), dtype))

    outs = [out(GLA_QK, BF16), out(GLA_QK, BF16), out(GLA_WIDTH, BF16), out(GLA_WIDTH, BF16),
            out(GLA_QK, F32), out(DIFF_QK, BF16), out(DIFF_QK, BF16), out(DIFF_WIDTH, BF16)]
    return pl.pallas_call(
        _inproj_kernel,
        grid=grid,
        in_specs=[
            pl.BlockSpec((1, tm, d), row),
            pl.BlockSpec((1, N_ADA, d), lambda b, i: (b, 0, 0)),
            pl.BlockSpec((1, d), const2),
            pl.BlockSpec((1, tm, LANES), row),
            pl.BlockSpec((1, tm, LANES), row),
            pl.BlockSpec((d, _COL_END), const2),
            pl.BlockSpec((GATE_PAD, GLA_QK), const2),
            pl.BlockSpec((1, GLA_QK), const2),
        ],
        out_specs=[o[0] for o in outs],
        out_shape=[o[1] for o in outs],
        compiler_params=pltpu.CompilerParams(
            dimension_semantics=("parallel", "parallel"), vmem_limit_bytes=VMEM_LIMIT),
        name="in_proj",
    )(x, ada3, pre_norm, cos_t, sin_t, w_pack, gate_w_pad, gate_b)


def _head_stack(t, lane_head):
    return jnp.concatenate(
        [jnp.where(lane_head == h, t, jnp.zeros_like(t)) for h in range(GLA_HEADS)], axis=0)


def _gla_kernel(q_ref, k_ref, v_ref, g_ref, og_ref, gn_ref, o_ref, state_ref, *, chunks):
    C = GLA_CHUNK

    @pl.when(pl.program_id(1) == 0)
    def _():
        state_ref[...] = jnp.zeros_like(state_ref)

    row = lax.broadcasted_iota(jnp.int32, (C, C), 0)
    col = lax.broadcasted_iota(jnp.int32, (C, C), 1)
    cum_mat = (row >= col).astype(BF16)
    srow = lax.broadcasted_iota(jnp.int32, (GLA_HEADS * C, C), 0)
    scol = lax.broadcasted_iota(jnp.int32, (GLA_HEADS * C, C), 1)
    causal = (srow % C) >= scol
    lane_head = lax.broadcasted_iota(jnp.int32, (C, GLA_QK), 1) // GLA_DK
    ones_cv = jnp.ones((C, GLA_DV), BF16)
    gn = gn_ref[...]

    def chunk(c, carry):
        r0 = pl.multiple_of(c * C, C)
        rows = pl.ds(r0, C)
        g = g_ref[0, rows, :]
        g_hi = g.astype(BF16)
        g_lo = (g - g_hi.astype(F32)).astype(BF16)
        b = _dot(cum_mat, g_hi) + _dot(cum_mat, g_lo)
        b_tot_t = _dot_tn(g_hi, ones_cv) + _dot_tn(g_lo, ones_cv)
        b_last = b[C - 1:C, :]
        b_mid = b[C // 2 - 1:C // 2, :]

        q = q_ref[0, rows, :].astype(F32)
        k = k_ref[0, rows, :].astype(F32)
        v = v_ref[0, rows, :]
        q_in = (q * jnp.exp(b)).astype(BF16)
        q_mid = (q * jnp.exp(b - b_mid)).astype(BF16)
        k_mid = (k * jnp.exp(b_mid - b)).astype(BF16)
        k_out = (k * jnp.exp(b_last - b)).astype(BF16)

        state = state_ref[...]
        scores = _dot_nt(_head_stack(q_mid, lane_head), k_mid)
        scores = jnp.where(causal, scores, 0.0).astype(BF16)
        inter = _dot(_head_stack(q_in, lane_head), state.astype(BF16))
        upd = _dot_tn(k_out, v)
        decay = jnp.exp(b_tot_t)

        for h in range(GLA_HEADS):
            hr = slice(h * C, (h + 1) * C)
            hv = slice(h * GLA_DV, (h + 1) * GLA_DV)
            o = inter[hr, :] + _dot(scores[hr, :], v[:, hv])
            og = og_ref[0, rows, hv].astype(F32)
            o_ref[0, rows, hv] = (_rms(o) * gn * _silu(og)).astype(o_ref.dtype)
            sr = slice(h * GLA_DK, (h + 1) * GLA_DK)
            state_ref[sr, :] = state[sr, :] * decay[sr, :] + upd[sr, hv]
        return carry

    lax.fori_loop(0, chunks, chunk, 0)


def _gla(gq, gk, gv, glog, gog, gla_norm, ts):
    bsz, seq, _ = gq.shape
    row = lambda b, i: (b, i, 0)
    return pl.pallas_call(
        functools.partial(_gla_kernel, chunks=ts // GLA_CHUNK),
        grid=(bsz, seq // ts),
        in_specs=[
            pl.BlockSpec((1, ts, GLA_QK), row),
            pl.BlockSpec((1, ts, GLA_QK), row),
            pl.BlockSpec((1, ts, GLA_WIDTH), row),
            pl.BlockSpec((1, ts, GLA_QK), row),
            pl.BlockSpec((1, ts, GLA_WIDTH), row),
            pl.BlockSpec((1, GLA_DV), lambda b, i: (0, 0)),
        ],
        out_specs=pl.BlockSpec((1, ts, GLA_WIDTH), row),
        out_shape=jax.ShapeDtypeStruct((bsz, seq, GLA_WIDTH), BF16),
        scratch_shapes=[pltpu.VMEM((GLA_QK, GLA_DV), F32)],
        compiler_params=pltpu.CompilerParams(
            dimension_semantics=("parallel", "arbitrary"), vmem_limit_bytes=VMEM_LIMIT),
        name="gla",
    )(gq, gk, gv, glog, gog, gla_norm)


def _diff_kernel(q_ref, k_ref, v_ref, lq1_ref, lk1_ref, lq2_ref, lk2_ref, dn_ref, o_ref,
                 *, tq, lambda_init):
    qi = pl.program_id(2)
    q = q_ref[0]
    lane = lax.broadcasted_iota(jnp.int32, q.shape, 1)
    zero = jnp.zeros_like(q)
    qs = jnp.concatenate([jnp.where(lane < DIFF_DQK, q, zero),
                          jnp.where(lane >= DIFF_DQK, q, zero)], axis=0)

    def step(j, carry, masked):
        m, l, acc = carry
        kv_rows = pl.ds(pl.multiple_of(j * tq, tq), tq)
        kb = k_ref[0, kv_rows, :]
        vb = v_ref[0, kv_rows, :]
        s = _dot_nt(qs, kb)
        if masked:
            r = lax.broadcasted_iota(jnp.int32, s.shape, 0) % tq
            cidx = lax.broadcasted_iota(jnp.int32, s.shape, 1)
            s = jnp.where(r >= cidx, s, -jnp.inf)
        m_new = jnp.maximum(m, jnp.max(s, axis=-1, keepdims=True))
        p = jnp.exp(s - m_new)
        alpha = jnp.exp(m - m_new)
        l = alpha * l + jnp.sum(p, axis=-1, keepdims=True)
        acc = alpha * acc + _dot(p.astype(BF16), vb)
        return m_new, l, acc

    init = (jnp.full((2 * tq, 1), -jnp.inf, F32), jnp.zeros((2 * tq, 1), F32),
            jnp.zeros((2 * tq, DIFF_DV), F32))
    carry = lax.fori_loop(0, qi, functools.partial(step, masked=False), init)
    _, l, acc = step(qi, carry, True)

    lam = (jnp.exp(jnp.sum(lq1_ref[...] * lk1_ref[...], axis=-1, keepdims=True))
           - jnp.exp(jnp.sum(lq2_ref[...] * lk2_ref[...], axis=-1, keepdims=True))
           + lambda_init)
    o_all = acc / l
    o = o_all[:tq] - lam * o_all[tq:]
    o_ref[0] = (_rms(o) * dn_ref[...] * (1.0 - lambda_init)).astype(o_ref.dtype)


def _diff_attn(dq, dk, dv, lq1, lk1, lq2, lk2, diff_norm, lambda_init, tq):
    bsz, seq, _ = dq.shape
    vec = lambda n: pl.BlockSpec((1, n), lambda b, h, i: (0, 0))
    return pl.pallas_call(
        functools.partial(_diff_kernel, tq=tq, lambda_init=lambda_init),
        grid=(bsz, DIFF_HEADS, seq // tq),
        in_specs=[
            pl.BlockSpec((1, tq, 2 * DIFF_DQK), lambda b, h, i: (b, i, h)),
            pl.BlockSpec((1, seq, 2 * DIFF_DQK), lambda b, h, i: (b, 0, h)),
            pl.BlockSpec((1, seq, DIFF_DV), lambda b, h, i: (b, 0, h)),
            vec(DIFF_DQK), vec(DIFF_DQK), vec(DIFF_DQK), vec(DIFF_DQK), vec(DIFF_DV),
        ],
        out_specs=pl.BlockSpec((1, tq, DIFF_DV), lambda b, h, i: (b, i, h)),
        out_shape=jax.ShapeDtypeStruct((bsz, seq, DIFF_WIDTH), BF16),
        compiler_params=pltpu.CompilerParams(
            dimension_semantics=("parallel", "parallel", "parallel"),
            vmem_limit_bytes=VMEM_LIMIT),
        name="diff_attn",
    )(dq, dk, dv, lq1, lk1, lq2, lk2, diff_norm)


def _out_mlp_kernel(x_ref, go_ref, do_ref, ada_ref, pn_mix_ref, pre_mlp_ref, pn_mlp_ref,
                    wo_ref, wu_ref, wd_ref, o_ref, u_ref, *, ff_chunk):
    x = x_ref[0]
    ada = ada_ref[0]
    gt_a, sh_m, sc_m, gt_m = ada[2:3], ada[3:4], ada[4:5], ada[5:6]
    y = _dot(go_ref[0], wo_ref[:GLA_WIDTH, :]) + _dot(do_ref[0], wo_ref[GLA_WIDTH:, :])
    x1 = x + gt_a * (_rms(y) * pn_mix_ref[...])
    h = (_rms(x1) * pre_mlp_ref[...] * (1.0 + sc_m) + sh_m).astype(BF16)
    d_ff = wu_ref.shape[1]
    for f in range(d_ff // ff_chunk):
        cols = slice(f * ff_chunk, (f + 1) * ff_chunk)
        u = jnp.maximum(_dot(h, wu_ref[:, cols]), 0.0)
        u_ref[:, cols] = (u * u).astype(BF16)
    y2 = _dot(u_ref[...], wd_ref[...])
    o_ref[0] = x1 + gt_m * (_rms(y2) * pn_mlp_ref[...])


def _out_mlp(x, go, do, ada3, post_mix, pre_mlp, post_mlp, w_out, w_up, w_down, tm, ff_chunk):
    bsz, seq, d = x.shape
    d_ff = w_up.shape[1]
    row = lambda b, i: (b, i, 0)
    const2 = lambda b, i: (0, 0)
    resident = functools.partial(pl.BlockSpec, index_map=const2, pipeline_mode=pl.Buffered(1))
    return pl.pallas_call(
        functools.partial(_out_mlp_kernel, ff_chunk=ff_chunk),
        grid=(bsz, seq // tm),
        in_specs=[
            pl.BlockSpec((1, tm, d), row),
            pl.BlockSpec((1, tm, GLA_WIDTH), row),
            pl.BlockSpec((1, tm, DIFF_WIDTH), row),
            pl.BlockSpec((1, N_ADA, d), lambda b, i: (b, 0, 0)),
            pl.BlockSpec((1, d), const2),
            pl.BlockSpec((1, d), const2),
            pl.BlockSpec((1, d), const2),
            resident((d, d)),
            resident((d, d_ff)),
            resident((d_ff, d)),
        ],
        out_specs=pl.BlockSpec((1, tm, d), row),
        out_shape=jax.ShapeDtypeStruct((bsz, seq, d), F32),
        scratch_shapes=[pltpu.VMEM((tm, d_ff), BF16)],
        compiler_params=pltpu.CompilerParams(
            dimension_semantics=("parallel", "parallel"), vmem_limit_bytes=VMEM_LIMIT),
        name="out_mlp",
    )(x, go, do, ada3, post_mix, pre_mlp, post_mlp, w_out, w_up, w_down)


def _rope_tables(positions):
    half = DIFF_DQK // 2
    inv_freq = 1.0 / (ROPE_THETA ** (jnp.arange(0, DIFF_DQK, 2, dtype=F32) / DIFF_DQK))
    ang = positions.astype(F32)[..., None] * inv_freq
    cos, sin = jnp.cos(ang), jnp.sin(ang)
    reps = LANES // (2 * half)
    cos_t = jnp.tile(cos, (1, 1, 2 * reps))
    sin_t = jnp.tile(jnp.concatenate([-sin, sin], axis=-1), (1, 1, reps))
    return cos_t, sin_t


def _pack_w_in(w_in):
    gq, gk, gv, lr, og, dq, dk, dv = jnp.split(
        w_in, [256, 512, 1024, 1040, 1552, 2064, 2576], axis=-1)
    lr = jnp.pad(lr, ((0, 0), (0, GATE_PAD - GLA_GATE_RANK)))
    return jnp.concatenate([gq, gk, gv, og, dq, dk, dv, lr], axis=-1).astype(BF16)


def kernel(x, c, positions, ada_w, ada_b, pre_norm_mix, post_norm_mix, w_in, gla_gate_w, gla_gate_b, gla_norm, lambda_q1, lambda_k1, lambda_q2, lambda_k2, diff_norm, w_out, pre_norm_mlp, post_norm_mlp, w_up, w_down):
    depth = ada_w.shape[0]
    bsz, seq, d = x.shape
    cos_t, sin_t = _rope_tables(positions)
    vec = lambda t: t.reshape(1, -1)
    for l in range(depth):
        lambda_init = 0.8 - 0.6 * math.exp(-0.3 * l)
        ada3 = _ada(c, ada_w[l], ada_b[l]).reshape(bsz, N_ADA, d)
        gate_w_pad = jnp.pad(gla_gate_w[l], ((0, GATE_PAD - GLA_GATE_RANK), (0, 0))).astype(BF16)
        gq, gk, gv, gog, glog, dq, dk, dv = _inproj(
            x, ada3, vec(pre_norm_mix[l]), cos_t, sin_t, _pack_w_in(w_in[l]), gate_w_pad,
            vec(gla_gate_b[l]), tm=512)
        go = _gla(gq, gk, gv, glog, gog, vec(gla_norm[l]), ts=512)
        do = _diff_attn(dq, dk, dv, vec(lambda_q1[l]), vec(lambda_k1[l]), vec(lambda_q2[l]),
                        vec(lambda_k2[l]), vec(diff_norm[l]), lambda_init, tq=256)
        x = _out_mlp(x, go, do, ada3, vec(post_norm_mix[l]), vec(pre_norm_mlp[l]),
                     vec(post_norm_mlp[l]), w_out[l].astype(BF16), w_up[l].astype(BF16),
                     w_down[l].astype(BF16), tm=512, ff_chunk=1024)
    return x
```

```python
import functools
import math

import jax
import jax.numpy as jnp
from jax import lax
from jax.experimental import pallas as pl
from jax.experimental.pallas import tpu as pltpu

F32 = jnp.float32
BF16 = jnp.bfloat16

GLA_HEADS = 4
GLA_DK = 64
GLA_DV = 128
GLA_QK = GLA_HEADS * GLA_DK
GLA_WIDTH = GLA_HEADS * GLA_DV
GLA_GATE_RANK = 16
GLA_GATE_NORM = 16.0
GLA_CHUNK = 64
DIFF_HEADS = 4
DIFF_DQK = 64
DIFF_DV = 128
DIFF_QK = DIFF_HEADS * 2 * DIFF_DQK
DIFF_WIDTH = DIFF_HEADS * DIFF_DV
ROPE_THETA = 10000.0
EPS = 1e-6
N_ADA = 6

LANES = 128
GATE_PAD = LANES
ROPE_HALF = DIFF_DQK // 2
_SUM_ROWS = 16
LOG2E = math.log2(math.e)
VMEM_LIMIT = 56 * 1024 * 1024

_COL_GQ = 0
_COL_GK = _COL_GQ + GLA_QK
_COL_GV = _COL_GK + GLA_QK
_COL_GOG = _COL_GV + GLA_WIDTH
_COL_DQ = _COL_GOG + GLA_WIDTH
_COL_DK = _COL_DQ + DIFF_QK
_COL_DV = _COL_DK + DIFF_QK
_COL_LR = _COL_DV + DIFF_WIDTH
_COL_END = _COL_LR + GATE_PAD


def _dot(a, b):
    return jnp.dot(a, b, preferred_element_type=F32)


def _dot_nt(a, b):
    return lax.dot_general(a, b, (((1,), (1,)), ((), ())), preferred_element_type=F32)


def _dot_tn(a, b):
    return lax.dot_general(a, b, (((0,), (0,)), ((), ())), preferred_element_type=F32)


def _rms(t):
    return t * lax.rsqrt(jnp.mean(t * t, axis=-1, keepdims=True) + EPS)


def _silu(t):
    return t * (1.0 / (1.0 + jnp.exp(-t)))


def _ada_kernel(c_ref, w_ref, b_ref, o_ref):
    ca = _silu(c_ref[...]).astype(BF16)
    o_ref[...] = _dot(ca, w_ref[...].astype(BF16)) + b_ref[...]


def _ada(c, ada_w, ada_b):
    bsz, d = c.shape
    n = ada_w.shape[1]
    tn = d
    return pl.pallas_call(
        _ada_kernel,
        grid=(n // tn,),
        in_specs=[
            pl.BlockSpec((bsz, d), lambda j: (0, 0)),
            pl.BlockSpec((d, tn), lambda j: (0, j)),
            pl.BlockSpec((1, tn), lambda j: (0, j)),
        ],
        out_specs=pl.BlockSpec((bsz, tn), lambda j: (0, j)),
        out_shape=jax.ShapeDtypeStruct((bsz, n), F32),
        compiler_params=pltpu.CompilerParams(
            dimension_semantics=("parallel",), vmem_limit_bytes=VMEM_LIMIT),
        name="ada_ln",
    )(c, ada_w, ada_b.reshape(1, n))


def _inproj_kernel(x_ref, ada_ref, pn_ref, cos_ref, sin_ref, w_ref, gw_ref, gb_ref,
                   gq_ref, gk_ref, gv_ref, gog_ref, glog_ref, dq_ref, dk_ref, dvt_ref):
    x = x_ref[0]
    ada = ada_ref[0]
    h = _rms(x) * pn_ref[...] * (1.0 + ada[1:2]) + ada[0:1]
    hb = h.astype(BF16)

    def proj(lo, hi):
        return _dot(hb, w_ref[:, lo:hi])

    gq_ref[0] = (proj(_COL_GQ, _COL_GK) * (GLA_DK ** -0.5)).astype(BF16)
    gk_ref[0] = proj(_COL_GK, _COL_GV).astype(BF16)
    gv_ref[0] = proj(_COL_GV, _COL_GOG).astype(BF16)
    gog_ref[0] = proj(_COL_GOG, _COL_DQ).astype(BF16)
    dvt_ref[0] = proj(_COL_DV, _COL_LR).T.astype(BF16)

    lr = proj(_COL_LR, _COL_END).astype(BF16)
    z = _dot(lr, gw_ref[...]) + gb_ref[...]
    log_sig = jnp.minimum(z, 0.0) - jnp.log1p(jnp.exp(-jnp.abs(z)))
    glog_ref[0] = log_sig * (1.0 / GLA_GATE_NORM)

    cos = cos_ref[0]
    sin = sin_ref[0]
    lane = lax.broadcasted_iota(jnp.int32, cos.shape, 1)
    first_half = (lane % DIFF_DQK) < ROPE_HALF

    def rope_store(out_ref, lo, scale):
        t = proj(lo, lo + DIFF_QK)
        for c in range(DIFF_QK // LANES):
            tc = t[:, c * LANES:(c + 1) * LANES]
            partner = jnp.where(first_half, pltpu.roll(tc, LANES - ROPE_HALF, 1),
                                pltpu.roll(tc, ROPE_HALF, 1))
            out_ref[0, :, c * LANES:(c + 1) * LANES] = ((tc * cos + partner * sin) * scale).astype(BF16)

    rope_store(dq_ref, _COL_DQ, DIFF_DQK ** -0.5 * LOG2E)
    rope_store(dk_ref, _COL_DK, 1.0)


def _inproj(x, ada3, pre_norm, cos_t, sin_t, w_pack, gate_w_pad, gate_b, tm):
    bsz, seq, d = x.shape
    grid = (bsz, seq // tm)
    row = lambda b, i: (b, i, 0)
    const2 = lambda b, i: (0, 0)

    def out(width, dtype):
        return (pl.BlockSpec((1, tm, width), row), jax.ShapeDtypeStruct((bsz, seq, width), dtype))

    dvt = (pl.BlockSpec((1, DIFF_WIDTH, tm), lambda b, i: (b, 0, i)),
           jax.ShapeDtypeStruct((bsz, DIFF_WIDTH, seq), BF16))
    outs = [out(GLA_QK, BF16), out(GLA_QK, BF16), out(GLA_WIDTH, BF16), out(GLA_WIDTH, BF16),
            out(GLA_QK, F32), out(DIFF_QK, BF16), out(DIFF_QK, BF16), dvt]
    return pl.pallas_call(
        _inproj_kernel,
        grid=grid,
        in_specs=[
            pl.BlockSpec((1, tm, d), row),
            pl.BlockSpec((1, N_ADA, d), lambda b, i: (b, 0, 0)),
            pl.BlockSpec((1, d), const2),
            pl.BlockSpec((1, tm, LANES), row),
            pl.BlockSpec((1, tm, LANES), row),
            pl.BlockSpec((d, _COL_END), const2),
            pl.BlockSpec((GATE_PAD, GLA_QK), const2),
            pl.BlockSpec((1, GLA_QK), const2),
        ],
        out_specs=[o[0] for o in outs],
        out_shape=[o[1] for o in outs],
        compiler_params=pltpu.CompilerParams(
            dimension_semantics=("parallel", "parallel"), vmem_limit_bytes=VMEM_LIMIT),
        name="in_proj",
    )(x, ada3, pre_norm, cos_t, sin_t, w_pack, gate_w_pad, gate_b)


def _head_stack(t, lane_head):
    return jnp.concatenate(
        [jnp.where(lane_head == h, t, jnp.zeros_like(t)) for h in range(GLA_HEADS)], axis=0)


def _gla_kernel(q_ref, k_ref, v_ref, g_ref, og_ref, gn_ref, o_ref, state_ref, *, chunks):
    C = GLA_CHUNK

    @pl.when(pl.program_id(1) == 0)
    def _():
        state_ref[...] = jnp.zeros_like(state_ref)

    row = lax.broadcasted_iota(jnp.int32, (C, C), 0)
    col = lax.broadcasted_iota(jnp.int32, (C, C), 1)
    cum_mat = (row >= col).astype(BF16)
    srow = lax.broadcasted_iota(jnp.int32, (GLA_HEADS * C, C), 0)
    scol = lax.broadcasted_iota(jnp.int32, (GLA_HEADS * C, C), 1)
    causal = (srow % C) >= scol
    lane_head = lax.broadcasted_iota(jnp.int32, (C, GLA_QK), 1) // GLA_DK
    ones_cv = jnp.ones((C, GLA_DV), BF16)
    gn = gn_ref[...]

    def chunk(c, carry):
        r0 = pl.multiple_of(c * C, C)
        rows = pl.ds(r0, C)
        g = g_ref[0, rows, :]
        g_hi = g.astype(BF16)
        g_lo = (g - g_hi.astype(F32)).astype(BF16)
        b = _dot(cum_mat, g_hi) + _dot(cum_mat, g_lo)
        b_tot_t = _dot_tn(g_hi, ones_cv) + _dot_tn(g_lo, ones_cv)
        b_last = b[C - 1:C, :]
        b_mid = b[C // 2 - 1:C // 2, :]

        q = q_ref[0, rows, :].astype(F32)
        k = k_ref[0, rows, :].astype(F32)
        v = v_ref[0, rows, :]
        q_in = (q * jnp.exp(b)).astype(BF16)
        q_mid = (q * jnp.exp(b - b_mid)).astype(BF16)
        k_mid = (k * jnp.exp(b_mid - b)).astype(BF16)
        k_out = (k * jnp.exp(b_last - b)).astype(BF16)

        state = state_ref[...]
        scores = _dot_nt(_head_stack(q_mid, lane_head), k_mid)
        scores = jnp.where(causal, scores, 0.0).astype(BF16)
        inter = _dot(_head_stack(q_in, lane_head), state.astype(BF16))
        upd = _dot_tn(k_out, v)
        decay = jnp.exp(b_tot_t)

        for h in range(GLA_HEADS):
            hr = slice(h * C, (h + 1) * C)
            hv = slice(h * GLA_DV, (h + 1) * GLA_DV)
            o = inter[hr, :] + _dot(scores[hr, :], v[:, hv])
            og = og_ref[0, rows, hv].astype(F32)
            o_ref[0, rows, hv] = (_rms(o) * gn * _silu(og)).astype(o_ref.dtype)
            sr = slice(h * GLA_DK, (h + 1) * GLA_DK)
            state_ref[sr, :] = state[sr, :] * decay[sr, :] + upd[sr, hv]
        return carry

    lax.fori_loop(0, chunks, chunk, 0)


def _gla(gq, gk, gv, glog, gog, gla_norm, ts):
    bsz, seq, _ = gq.shape
    row = lambda b, i: (b, i, 0)
    return pl.pallas_call(
        functools.partial(_gla_kernel, chunks=ts // GLA_CHUNK),
        grid=(bsz, seq // ts),
        in_specs=[
            pl.BlockSpec((1, ts, GLA_QK), row),
            pl.BlockSpec((1, ts, GLA_QK), row),
            pl.BlockSpec((1, ts, GLA_WIDTH), row),
            pl.BlockSpec((1, ts, GLA_QK), row),
            pl.BlockSpec((1, ts, GLA_WIDTH), row),
            pl.BlockSpec((1, GLA_DV), lambda b, i: (0, 0)),
        ],
        out_specs=pl.BlockSpec((1, ts, GLA_WIDTH), row),
        out_shape=jax.ShapeDtypeStruct((bsz, seq, GLA_WIDTH), BF16),
        scratch_shapes=[pltpu.VMEM((GLA_QK, GLA_DV), F32)],
        compiler_params=pltpu.CompilerParams(
            dimension_semantics=("parallel", "arbitrary"), vmem_limit_bytes=VMEM_LIMIT),
        name="gla",
    )(gq, gk, gv, glog, gog, gla_norm)


def _diff_kernel(q_ref, k_ref, vt_ref, lq1_ref, lk1_ref, lq2_ref, lk2_ref, dn_ref, o_ref,
                 acc_ref, *, tq, heads, lambda_init):
    qi = pl.program_id(2)
    lane = lax.broadcasted_iota(jnp.int32, (tq, 2 * DIFF_DQK), 1)
    ones_rows = jnp.ones((_SUM_ROWS, tq), BF16)
    acc_ref[...] = jnp.zeros_like(acc_ref)

    def head_cols(h, width):
        return slice(h * width, (h + 1) * width)

    qs = []
    for h in range(heads):
        q = q_ref[0, :, head_cols(h, 2 * DIFF_DQK)]
        zero = jnp.zeros_like(q)
        qs.append(jnp.concatenate([jnp.where(lane < DIFF_DQK, q, zero),
                                   jnp.where(lane >= DIFF_DQK, q, zero)], axis=0))

    def scores(j, h):
        kb = k_ref[0, pl.ds(pl.multiple_of(j * tq, tq), tq), head_cols(h, 2 * DIFF_DQK)]
        return _dot_nt(kb, qs[h])

    def consume(j, h, st, m, masked):
        vtb = vt_ref[0, head_cols(h, DIFF_DV), pl.ds(pl.multiple_of(j * tq, tq), tq)]
        if masked:
            key = lax.broadcasted_iota(jnp.int32, st.shape, 0)
            qry = lax.broadcasted_iota(jnp.int32, st.shape, 1) % tq
            st = jnp.where(key <= qry, st, -jnp.inf)
        m_new = jnp.maximum(m, jnp.max(st, axis=0, keepdims=True))
        p = jnp.exp2(st - m_new).astype(BF16)
        alpha = jnp.exp2(m - m_new)
        v_aug = jnp.concatenate([vtb, ones_rows], axis=0)
        acc_ref[h] = alpha * acc_ref[h] + _dot(v_aug, p)
        return m_new

    def body(j, carry):
        sts, ms = carry
        nxt = tuple(scores(j + 1, h) for h in range(heads))
        return nxt, tuple(consume(j, h, sts[h], ms[h], False) for h in range(heads))

    m0 = jnp.full((1, 2 * tq), -jnp.inf, F32)
    init = (tuple(scores(0, h) for h in range(heads)), (m0,) * heads)
    sts, ms = lax.fori_loop(0, qi, body, init)

    lam = (jnp.exp(jnp.sum(lq1_ref[...] * lk1_ref[...], axis=-1, keepdims=True))
           - jnp.exp(jnp.sum(lq2_ref[...] * lk2_ref[...], axis=-1, keepdims=True))
           + lambda_init)
    for h in range(heads):
        consume(qi, h, sts[h], ms[h], True)
        acc = acc_ref[h]
        o_all = acc[:DIFF_DV] * (1.0 / acc[DIFF_DV:DIFF_DV + 1])
        ot = o_all[:, :tq] - lam * o_all[:, tq:]
        ot = ot * lax.rsqrt(jnp.mean(ot * ot, axis=0, keepdims=True) + EPS)
        o_ref[0, :, head_cols(h, DIFF_DV)] = (
            ot.T * dn_ref[...] * (1.0 - lambda_init)).astype(o_ref.dtype)


def _diff_attn(dq, dk, dvt, lq1, lk1, lq2, lk2, diff_norm, lambda_init, tq, heads):
    bsz, seq, _ = dq.shape
    vec = lambda n: pl.BlockSpec((1, n), lambda b, g, i: (0, 0))
    return pl.pallas_call(
        functools.partial(_diff_kernel, tq=tq, heads=heads, lambda_init=lambda_init),
        grid=(bsz, DIFF_HEADS // heads, seq // tq),
        in_specs=[
            pl.BlockSpec((1, tq, heads * 2 * DIFF_DQK), lambda b, g, i: (b, i, g)),
            pl.BlockSpec((1, seq, heads * 2 * DIFF_DQK), lambda b, g, i: (b, 0, g)),
            pl.BlockSpec((1, heads * DIFF_DV, seq), lambda b, g, i: (b, g, 0)),
            vec(DIFF_DQK), vec(DIFF_DQK), vec(DIFF_DQK), vec(DIFF_DQK), vec(DIFF_DV),
        ],
        out_specs=pl.BlockSpec((1, tq, heads * DIFF_DV), lambda b, g, i: (b, i, g)),
        out_shape=jax.ShapeDtypeStruct((bsz, seq, DIFF_WIDTH), BF16),
        scratch_shapes=[pltpu.VMEM((heads, DIFF_DV + _SUM_ROWS, 2 * tq), F32)],
        compiler_params=pltpu.CompilerParams(
            dimension_semantics=("parallel", "parallel", "parallel"),
            vmem_limit_bytes=VMEM_LIMIT),
        name="diff_attn",
    )(dq, dk, dvt, lq1, lk1, lq2, lk2, diff_norm)


def _out_mlp_kernel(x_ref, go_ref, do_ref, ada_ref, pn_mix_ref, pre_mlp_ref, pn_mlp_ref,
                    wo_ref, wu_ref, wd_ref, o_ref, u_ref, *, ff_chunk):
    x = x_ref[0]
    ada = ada_ref[0]
    gt_a, sh_m, sc_m, gt_m = ada[2:3], ada[3:4], ada[4:5], ada[5:6]
    y = _dot(go_ref[0], wo_ref[:GLA_WIDTH, :]) + _dot(do_ref[0], wo_ref[GLA_WIDTH:, :])
    x1 = x + gt_a * (_rms(y) * pn_mix_ref[...])
    h = (_rms(x1) * pre_mlp_ref[...] * (1.0 + sc_m) + sh_m).astype(BF16)
    d_ff = wu_ref.shape[1]
    for f in range(d_ff // ff_chunk):
        cols = slice(f * ff_chunk, (f + 1) * ff_chunk)
        u = jnp.maximum(_dot(h, wu_ref[:, cols]), 0.0)
        u_ref[:, cols] = (u * u).astype(BF16)
    y2 = _dot(u_ref[...], wd_ref[...])
    o_ref[0] = x1 + gt_m * (_rms(y2) * pn_mlp_ref[...])


def _out_mlp(x, go, do, ada3, post_mix, pre_mlp, post_mlp, w_out, w_up, w_down, tm, ff_chunk):
    bsz, seq, d = x.shape
    d_ff = w_up.shape[1]
    row = lambda b, i: (b, i, 0)
    const2 = lambda b, i: (0, 0)
    resident = functools.partial(pl.BlockSpec, index_map=const2, pipeline_mode=pl.Buffered(1))
    return pl.pallas_call(
        functools.partial(_out_mlp_kernel, ff_chunk=ff_chunk),
        grid=(bsz, seq // tm),
        in_specs=[
            pl.BlockSpec((1, tm, d), row),
            pl.BlockSpec((1, tm, GLA_WIDTH), row),
            pl.BlockSpec((1, tm, DIFF_WIDTH), row),
            pl.BlockSpec((1, N_ADA, d), lambda b, i: (b, 0, 0)),
            pl.BlockSpec((1, d), const2),
            pl.BlockSpec((1, d), const2),
            pl.BlockSpec((1, d), const2),
            resident((d, d)),
            resident((d, d_ff)),
            resident((d_ff, d)),
        ],
        out_specs=pl.BlockSpec((1, tm, d), row),
        out_shape=jax.ShapeDtypeStruct((bsz, seq, d), F32),
        scratch_shapes=[pltpu.VMEM((tm, d_ff), BF16)],
        compiler_params=pltpu.CompilerParams(
            dimension_semantics=("parallel", "parallel"), vmem_limit_bytes=VMEM_LIMIT),
        name="out_mlp",
    )(x, go, do, ada3, post_mix, pre_mlp, post_mlp, w_out, w_up, w_down)


def _rope_tables(positions):
    half = DIFF_DQK // 2
    inv_freq = 1.0 / (ROPE_THETA ** (jnp.arange(0, DIFF_DQK, 2, dtype=F32) / DIFF_DQK))
    ang = positions.astype(F32)[..., None] * inv_freq
    cos, sin = jnp.cos(ang), jnp.sin(ang)
    reps = LANES // (2 * half)
    cos_t = jnp.tile(cos, (1, 1, 2 * reps))
    sin_t = jnp.tile(jnp.concatenate([-sin, sin], axis=-1), (1, 1, reps))
    return cos_t, sin_t


def _pack_w_in(w_in):
    sizes = (GLA_QK, GLA_QK, GLA_WIDTH, GLA_GATE_RANK, GLA_WIDTH, DIFF_QK, DIFF_QK, DIFF_WIDTH)
    offsets = [sum(sizes[:n]) for n in range(1, len(sizes))]
    gq, gk, gv, lr, og, dq, dk, dv = jnp.split(w_in, offsets, axis=-1)
    lr = jnp.pad(lr, ((0, 0), (0, GATE_PAD - GLA_GATE_RANK)))
    return jnp.concatenate([gq, gk, gv, og, dq, dk, dv, lr], axis=-1).astype(BF16)


def kernel(x, c, positions, ada_w, ada_b, pre_norm_mix, post_norm_mix, w_in, gla_gate_w, gla_gate_b, gla_norm, lambda_q1, lambda_k1, lambda_q2, lambda_k2, diff_norm, w_out, pre_norm_mlp, post_norm_mlp, w_up, w_down):
    depth = ada_w.shape[0]
    bsz, seq, d = x.shape
    cos_t, sin_t = _rope_tables(positions)
    vec = lambda t: t.reshape(1, -1)
    for l in range(depth):
        lambda_init = 0.8 - 0.6 * math.exp(-0.3 * l)
        ada3 = _ada(c, ada_w[l], ada_b[l]).reshape(bsz, N_ADA, d)
        gate_w_pad = jnp.pad(gla_gate_w[l], ((0, GATE_PAD - GLA_GATE_RANK), (0, 0))).astype(BF16)
        gq, gk, gv, gog, glog, dq, dk, dvt = _inproj(
            x, ada3, vec(pre_norm_mix[l]), cos_t, sin_t, _pack_w_in(w_in[l]), gate_w_pad,
            vec(gla_gate_b[l]), tm=512)
        go = _gla(gq, gk, gv, glog, gog, vec(gla_norm[l]), ts=512)
        do = _diff_attn(dq, dk, dvt, vec(lambda_q1[l]), vec(lambda_k1[l]), vec(lambda_q2[l]),
                        vec(lambda_k2[l]), vec(diff_norm[l]), lambda_init, tq=256, heads=4)
        x = _out_mlp(x, go, do, ada3, vec(post_norm_mix[l]), vec(pre_norm_mlp[l]),
                     vec(post_norm_mlp[l]), w_out[l].astype(BF16), w_up[l].astype(BF16),
                     w_down[l].astype(BF16), tm=512, ff_chunk=1024)
    return x
```

```python
import functools
import math

import jax
import jax.numpy as jnp
from jax import lax
from jax.experimental import pallas as pl
from jax.experimental.pallas import tpu as pltpu

F32 = jnp.float32
BF16 = jnp.bfloat16

GLA_HEADS = 4
GLA_DK = 64
GLA_DV = 128
GLA_QK = GLA_HEADS * GLA_DK
GLA_WIDTH = GLA_HEADS * GLA_DV
GLA_GATE_RANK = 16
GLA_GATE_NORM = 16.0
GLA_CHUNK = 64
DIFF_HEADS = 4
DIFF_DQK = 64
DIFF_DV = 128
DIFF_QK = DIFF_HEADS * 2 * DIFF_DQK
DIFF_WIDTH = DIFF_HEADS * DIFF_DV
ROPE_THETA = 10000.0
EPS = 1e-6
N_ADA = 6

LANES = 128
GATE_PAD = LANES
ROPE_HALF = DIFF_DQK // 2
_SUM_ROWS = 16
LOG2E = math.log2(math.e)
VMEM_LIMIT = 56 * 1024 * 1024

_COL_GQ = 0
_COL_GK = _COL_GQ + GLA_QK
_COL_GV = _COL_GK + GLA_QK
_COL_GOG = _COL_GV + GLA_WIDTH
_COL_DQ = _COL_GOG + GLA_WIDTH
_COL_DK = _COL_DQ + DIFF_QK
_COL_DV = _COL_DK + DIFF_QK
_COL_LR = _COL_DV + DIFF_WIDTH
_COL_END = _COL_LR + GATE_PAD


def _dot(a, b):
    return jnp.dot(a, b, preferred_element_type=F32)


def _dot_nt(a, b):
    return lax.dot_general(a, b, (((1,), (1,)), ((), ())), preferred_element_type=F32)


def _dot_tn(a, b):
    return lax.dot_general(a, b, (((0,), (0,)), ((), ())), preferred_element_type=F32)


def _rms(t):
    return t * lax.rsqrt(jnp.mean(t * t, axis=-1, keepdims=True) + EPS)


def _silu(t):
    return t * (1.0 / (1.0 + jnp.exp(-t)))


def _ada_kernel(c_ref, w_ref, b_ref, o_ref):
    ca = _silu(c_ref[...]).astype(BF16)
    o_ref[...] = _dot(ca, w_ref[...].astype(BF16)) + b_ref[...]


def _ada(c, ada_w, ada_b):
    bsz, d = c.shape
    n = ada_w.shape[1]
    tn = d
    return pl.pallas_call(
        _ada_kernel,
        grid=(n // tn,),
        in_specs=[
            pl.BlockSpec((bsz, d), lambda j: (0, 0)),
            pl.BlockSpec((d, tn), lambda j: (0, j)),
            pl.BlockSpec((1, tn), lambda j: (0, j)),
        ],
        out_specs=pl.BlockSpec((bsz, tn), lambda j: (0, j)),
        out_shape=jax.ShapeDtypeStruct((bsz, n), F32),
        compiler_params=pltpu.CompilerParams(
            dimension_semantics=("parallel",), vmem_limit_bytes=VMEM_LIMIT),
        name="ada_ln",
    )(c, ada_w, ada_b.reshape(1, n))


def _inproj_kernel(x_ref, ada_ref, pn_ref, cos_ref, sin_ref, w_ref, gw_ref, gb_ref,
                   gq_ref, gk_ref, gv_ref, gog_ref, glog_ref, dq_ref, dk_ref, dvt_ref):
    x = x_ref[0]
    ada = ada_ref[0]
    h = _rms(x) * pn_ref[...] * (1.0 + ada[1:2]) + ada[0:1]
    hb = h.astype(BF16)

    def proj(lo, hi):
        return _dot(hb, w_ref[:, lo:hi])

    gq_ref[0] = (proj(_COL_GQ, _COL_GK) * (GLA_DK ** -0.5)).astype(BF16)
    gk_ref[0] = proj(_COL_GK, _COL_GV).astype(BF16)
    gv_ref[0] = proj(_COL_GV, _COL_GOG).astype(BF16)
    gog_ref[0] = proj(_COL_GOG, _COL_DQ).astype(BF16)
    dvt_ref[0] = proj(_COL_DV, _COL_LR).T.astype(BF16)

    lr = proj(_COL_LR, _COL_END).astype(BF16)
    z = _dot(lr, gw_ref[...]) + gb_ref[...]
    log_sig = jnp.minimum(z, 0.0) - jnp.log1p(jnp.exp(-jnp.abs(z)))
    glog_ref[0] = log_sig * (1.0 / GLA_GATE_NORM)

    cos = cos_ref[0]
    sin = sin_ref[0]
    lane = lax.broadcasted_iota(jnp.int32, cos.shape, 1)
    first_half = (lane % DIFF_DQK) < ROPE_HALF

    def rope_store(out_ref, lo, scale):
        t = proj(lo, lo + DIFF_QK)
        for c in range(DIFF_QK // LANES):
            tc = t[:, c * LANES:(c + 1) * LANES]
            partner = jnp.where(first_half, pltpu.roll(tc, LANES - ROPE_HALF, 1),
                                pltpu.roll(tc, ROPE_HALF, 1))
            out_ref[0, :, c * LANES:(c + 1) * LANES] = ((tc * cos + partner * sin) * scale).astype(BF16)

    rope_store(dq_ref, _COL_DQ, DIFF_DQK ** -0.5 * LOG2E)
    rope_store(dk_ref, _COL_DK, 1.0)


def _inproj(x, ada3, pre_norm, cos_t, sin_t, w_pack, gate_w_pad, gate_b, tm):
    bsz, seq, d = x.shape
    grid = (bsz, seq // tm)
    row = lambda b, i: (b, i, 0)
    const2 = lambda b, i: (0, 0)

    def out(width, dtype):
        return (pl.BlockSpec((1, tm, width), row), jax.ShapeDtypeStruct((bsz, seq, width), dtype))

    dvt = (pl.BlockSpec((1, DIFF_WIDTH, tm), lambda b, i: (b, 0, i)),
           jax.ShapeDtypeStruct((bsz, DIFF_WIDTH, seq), BF16))
    outs = [out(GLA_QK, BF16), out(GLA_QK, BF16), out(GLA_WIDTH, BF16), out(GLA_WIDTH, BF16),
            out(GLA_QK, F32), out(DIFF_QK, BF16), out(DIFF_QK, BF16), dvt]
    return pl.pallas_call(
        _inproj_kernel,
        grid=grid,
        in_specs=[
            pl.BlockSpec((1, tm, d), row),
            pl.BlockSpec((1, N_ADA, d), lambda b, i: (b, 0, 0)),
            pl.BlockSpec((1, d), const2),
            pl.BlockSpec((1, tm, LANES), row),
            pl.BlockSpec((1, tm, LANES), row),
            pl.BlockSpec((d, _COL_END), const2),
            pl.BlockSpec((GATE_PAD, GLA_QK), const2),
            pl.BlockSpec((1, GLA_QK), const2),
        ],
        out_specs=[o[0] for o in outs],
        out_shape=[o[1] for o in outs],
        compiler_params=pltpu.CompilerParams(
            dimension_semantics=("parallel", "parallel"), vmem_limit_bytes=VMEM_LIMIT),
        name="in_proj",
    )(x, ada3, pre_norm, cos_t, sin_t, w_pack, gate_w_pad, gate_b)


def _head_stack(t, lane_head):
    return jnp.concatenate(
        [jnp.where(lane_head == h, t, jnp.zeros_like(t)) for h in range(GLA_HEADS)], axis=0)


def _gla_kernel(q_ref, k_ref, v_ref, g_ref, og_ref, gn_ref, o_ref, state_ref, *, chunks):
    C = GLA_CHUNK

    @pl.when(pl.program_id(1) == 0)
    def _():
        state_ref[...] = jnp.zeros_like(state_ref)

    row = lax.broadcasted_iota(jnp.int32, (C, C), 0)
    col = lax.broadcasted_iota(jnp.int32, (C, C), 1)
    cum_mat = (row >= col).astype(BF16)
    srow = lax.broadcasted_iota(jnp.int32, (GLA_HEADS * C, C), 0)
    scol = lax.broadcasted_iota(jnp.int32, (GLA_HEADS * C, C), 1)
    causal = (srow % C) >= scol
    lane_head = lax.broadcasted_iota(jnp.int32, (C, GLA_QK), 1) // GLA_DK
    ones_cv = jnp.ones((C, GLA_DV), BF16)
    gn = gn_ref[...]

    def chunk(c, carry):
        r0 = pl.multiple_of(c * C, C)
        rows = pl.ds(r0, C)
        g = g_ref[0, rows, :]
        g_hi = g.astype(BF16)
        g_lo = (g - g_hi.astype(F32)).astype(BF16)
        b = _dot(cum_mat, g_hi) + _dot(cum_mat, g_lo)
        b_tot_t = _dot_tn(g_hi, ones_cv) + _dot_tn(g_lo, ones_cv)
        b_last = b[C - 1:C, :]
        b_mid = b[C // 2 - 1:C // 2, :]

        q = q_ref[0, rows, :].astype(F32)
        k = k_ref[0, rows, :].astype(F32)
        v = v_ref[0, rows, :]
        q_in = (q * jnp.exp(b)).astype(BF16)
        q_mid = (q * jnp.exp(b - b_mid)).astype(BF16)
        k_mid = (k * jnp.exp(b_mid - b)).astype(BF16)
        k_out = (k * jnp.exp(b_last - b)).astype(BF16)

        state = state_ref[...]
        scores = _dot_nt(_head_stack(q_mid, lane_head), k_mid)
        scores = jnp.where(causal, scores, 0.0).astype(BF16)
        inter = _dot(_head_stack(q_in, lane_head), state.astype(BF16))
        upd = _dot_tn(k_out, v)
        decay = jnp.exp(b_tot_t)

        for h in range(GLA_HEADS):
            hr = slice(h * C, (h + 1) * C)
            hv = slice(h * GLA_DV, (h + 1) * GLA_DV)
            o = inter[hr, :] + _dot(scores[hr, :], v[:, hv])
            og = og_ref[0, rows, hv].astype(F32)
            o_ref[0, rows, hv] = (_rms(o) * gn * _silu(og)).astype(o_ref.dtype)
            sr = slice(h * GLA_DK, (h + 1) * GLA_DK)
            state_ref[sr, :] = state[sr, :] * decay[sr, :] + upd[sr, hv]
        return carry

    lax.fori_loop(0, chunks, chunk, 0)


def _gla(gq, gk, gv, glog, gog, gla_norm, ts):
    bsz, seq, _ = gq.shape
    row = lambda b, i: (b, i, 0)
    return pl.pallas_call(
        functools.partial(_gla_kernel, chunks=ts // GLA_CHUNK),
        grid=(bsz, seq // ts),
        in_specs=[
            pl.BlockSpec((1, ts, GLA_QK), row),
            pl.BlockSpec((1, ts, GLA_QK), row),
            pl.BlockSpec((1, ts, GLA_WIDTH), row),
            pl.BlockSpec((1, ts, GLA_QK), row),
            pl.BlockSpec((1, ts, GLA_WIDTH), row),
            pl.BlockSpec((1, GLA_DV), lambda b, i: (0, 0)),
        ],
        out_specs=pl.BlockSpec((1, ts, GLA_WIDTH), row),
        out_shape=jax.ShapeDtypeStruct((bsz, seq, GLA_WIDTH), BF16),
        scratch_shapes=[pltpu.VMEM((GLA_QK, GLA_DV), F32)],
        compiler_params=pltpu.CompilerParams(
            dimension_semantics=("parallel", "arbitrary"), vmem_limit_bytes=VMEM_LIMIT),
        name="gla",
    )(gq, gk, gv, glog, gog, gla_norm)


def _diff_kernel(q_ref, k_ref, vt_ref, lq1_ref, lk1_ref, lq2_ref, lk2_ref, dn_ref, o_ref,
                 acc_ref, s_ref, *, tq, heads, lambda_init):
    qi = pl.program_id(2)
    lane = lax.broadcasted_iota(jnp.int32, (tq, 2 * DIFF_DQK), 1)
    ones_rows = jnp.ones((_SUM_ROWS, tq), BF16)
    acc_ref[...] = jnp.zeros_like(acc_ref)

    def head_cols(h, width):
        return slice(h * width, (h + 1) * width)

    qs = []
    for h in range(heads):
        q = q_ref[0, :, head_cols(h, 2 * DIFF_DQK)]
        zero = jnp.zeros_like(q)
        qs.append(jnp.concatenate([jnp.where(lane < DIFF_DQK, q, zero),
                                   jnp.where(lane >= DIFF_DQK, q, zero)], axis=0))

    def score(slot, j):
        for h in range(heads):
            kb = k_ref[0, pl.ds(pl.multiple_of(j * tq, tq), tq), head_cols(h, 2 * DIFF_DQK)]
            s_ref[slot, h] = _dot_nt(kb, qs[h])

    def consume(slot, j, ms, masked):
        out = []
        for h in range(heads):
            vtb = vt_ref[0, head_cols(h, DIFF_DV), pl.ds(pl.multiple_of(j * tq, tq), tq)]
            m_new, p = [], []
            for c in range(2 * tq // LANES):
                cols = slice(c * LANES, (c + 1) * LANES)
                st = s_ref[slot, h, :, cols]
                if masked:
                    key = lax.broadcasted_iota(jnp.int32, st.shape, 0)
                    qry = lax.broadcasted_iota(jnp.int32, st.shape, 1) + (c * LANES) % tq
                    st = jnp.where(key <= qry, st, -jnp.inf)
                mc = jnp.maximum(ms[h][:, cols], jnp.max(st, axis=0, keepdims=True))
                p.append(jnp.exp2(st - mc).astype(BF16))
                m_new.append(mc)
            m_new = jnp.concatenate(m_new, axis=1)
            alpha = jnp.exp2(ms[h] - m_new)
            v_aug = jnp.concatenate([vtb, ones_rows], axis=0)
            acc_ref[h] = alpha * acc_ref[h] + _dot(v_aug, jnp.concatenate(p, axis=1))
            out.append(m_new)
        return tuple(out)

    def block_pair(i, ms):
        score(1, 2 * i + 1)
        ms = consume(0, 2 * i, ms, False)
        score(0, 2 * i + 2)
        return consume(1, 2 * i + 1, ms, False)

    score(0, 0)
    ms = lax.fori_loop(0, qi // 2, block_pair,
                       (jnp.full((1, 2 * tq), -jnp.inf, F32),) * heads)

    @pl.when(qi % 2 == 0)
    def _():
        consume(0, qi, ms, True)

    @pl.when(qi % 2 == 1)
    def _():
        score(1, qi)
        consume(1, qi, consume(0, qi - 1, ms, False), True)

    lam = (jnp.exp(jnp.sum(lq1_ref[...] * lk1_ref[...], axis=-1, keepdims=True))
           - jnp.exp(jnp.sum(lq2_ref[...] * lk2_ref[...], axis=-1, keepdims=True))
           + lambda_init)
    for h in range(heads):
        acc = acc_ref[h]
        o_all = acc[:DIFF_DV] * (1.0 / acc[DIFF_DV:DIFF_DV + 1])
        ot = o_all[:, :tq] - lam * o_all[:, tq:]
        ot = ot * lax.rsqrt(jnp.mean(ot * ot, axis=0, keepdims=True) + EPS)
        o_ref[0, :, head_cols(h, DIFF_DV)] = (
            ot.T * dn_ref[...] * (1.0 - lambda_init)).astype(o_ref.dtype)


def _diff_attn(dq, dk, dvt, lq1, lk1, lq2, lk2, diff_norm, lambda_init, tq, heads):
    bsz, seq, _ = dq.shape
    vec = lambda n: pl.BlockSpec((1, n), lambda b, g, i: (0, 0))
    return pl.pallas_call(
        functools.partial(_diff_kernel, tq=tq, heads=heads, lambda_init=lambda_init),
        grid=(bsz, DIFF_HEADS // heads, seq // tq),
        in_specs=[
            pl.BlockSpec((1, tq, heads * 2 * DIFF_DQK), lambda b, g, i: (b, i, g)),
            pl.BlockSpec((1, seq, heads * 2 * DIFF_DQK), lambda b, g, i: (b, 0, g)),
            pl.BlockSpec((1, heads * DIFF_DV, seq), lambda b, g, i: (b, g, 0)),
            vec(DIFF_DQK), vec(DIFF_DQK), vec(DIFF_DQK), vec(DIFF_DQK), vec(DIFF_DV),
        ],
        out_specs=pl.BlockSpec((1, tq, heads * DIFF_DV), lambda b, g, i: (b, i, g)),
        out_shape=jax.ShapeDtypeStruct((bsz, seq, DIFF_WIDTH), BF16),
        scratch_shapes=[pltpu.VMEM((heads, DIFF_DV + _SUM_ROWS, 2 * tq), F32),
                        pltpu.VMEM((2, heads, tq, 2 * tq), F32)],
        compiler_params=pltpu.CompilerParams(
            dimension_semantics=("parallel", "parallel", "parallel"),
            vmem_limit_bytes=VMEM_LIMIT),
        name="diff_attn",
    )(dq, dk, dvt, lq1, lk1, lq2, lk2, diff_norm)


def _out_mlp_kernel(x_ref, go_ref, do_ref, ada_ref, pn_mix_ref, pre_mlp_ref, pn_mlp_ref,
                    wo_ref, wu_ref, wd_ref, o_ref, u_ref, *, ff_chunk):
    x = x_ref[0]
    ada = ada_ref[0]
    gt_a, sh_m, sc_m, gt_m = ada[2:3], ada[3:4], ada[4:5], ada[5:6]
    y = _dot(go_ref[0], wo_ref[:GLA_WIDTH, :]) + _dot(do_ref[0], wo_ref[GLA_WIDTH:, :])
    x1 = x + gt_a * (_rms(y) * pn_mix_ref[...])
    h = (_rms(x1) * pre_mlp_ref[...] * (1.0 + sc_m) + sh_m).astype(BF16)
    d_ff = wu_ref.shape[1]
    for f in range(d_ff // ff_chunk):
        cols = slice(f * ff_chunk, (f + 1) * ff_chunk)
        u = jnp.maximum(_dot(h, wu_ref[:, cols]), 0.0)
        u_ref[:, cols] = (u * u).astype(BF16)
    y2 = _dot(u_ref[...], wd_ref[...])
    o_ref[0] = x1 + gt_m * (_rms(y2) * pn_mlp_ref[...])


def _out_mlp(x, go, do, ada3, post_mix, pre_mlp, post_mlp, w_out, w_up, w_down, tm, ff_chunk):
    bsz, seq, d = x.shape
    d_ff = w_up.shape[1]
    row = lambda b, i: (b, i, 0)
    const2 = lambda b, i: (0, 0)
    resident = functools.partial(pl.BlockSpec, index_map=const2, pipeline_mode=pl.Buffered(1))
    return pl.pallas_call(
        functools.partial(_out_mlp_kernel, ff_chunk=ff_chunk),
        grid=(bsz, seq // tm),
        in_specs=[
            pl.BlockSpec((1, tm, d), row),
            pl.BlockSpec((1, tm, GLA_WIDTH), row),
            pl.BlockSpec((1, tm, DIFF_WIDTH), row),
            pl.BlockSpec((1, N_ADA, d), lambda b, i: (b, 0, 0)),
            pl.BlockSpec((1, d), const2),
            pl.BlockSpec((1, d), const2),
            pl.BlockSpec((1, d), const2),
            resident((d, d)),
            resident((d, d_ff)),
            resident((d_ff, d)),
        ],
        out_specs=pl.BlockSpec((1, tm, d), row),
        out_shape=jax.ShapeDtypeStruct((bsz, seq, d), F32),
        scratch_shapes=[pltpu.VMEM((tm, d_ff), BF16)],
        compiler_params=pltpu.CompilerParams(
            dimension_semantics=("parallel", "parallel"), vmem_limit_bytes=VMEM_LIMIT),
        name="out_mlp",
    )(x, go, do, ada3, post_mix, pre_mlp, post_mlp, w_out, w_up, w_down)


def _rope_tables(positions):
    half = DIFF_DQK // 2
    inv_freq = 1.0 / (ROPE_THETA ** (jnp.arange(0, DIFF_DQK, 2, dtype=F32) / DIFF_DQK))
    ang = positions.astype(F32)[..., None] * inv_freq
    cos, sin = jnp.cos(ang), jnp.sin(ang)
    reps = LANES // (2 * half)
    cos_t = jnp.tile(cos, (1, 1, 2 * reps))
    sin_t = jnp.tile(jnp.concatenate([-sin, sin], axis=-1), (1, 1, reps))
    return cos_t, sin_t


def _pack_w_in(w_in):
    sizes = (GLA_QK, GLA_QK, GLA_WIDTH, GLA_GATE_RANK, GLA_WIDTH, DIFF_QK, DIFF_QK, DIFF_WIDTH)
    offsets = [sum(sizes[:n]) for n in range(1, len(sizes))]
    gq, gk, gv, lr, og, dq, dk, dv = jnp.split(w_in, offsets, axis=-1)
    lr = jnp.pad(lr, ((0, 0), (0, GATE_PAD - GLA_GATE_RANK)))
    return jnp.concatenate([gq, gk, gv, og, dq, dk, dv, lr], axis=-1).astype(BF16)


def kernel(x, c, positions, ada_w, ada_b, pre_norm_mix, post_norm_mix, w_in, gla_gate_w, gla_gate_b, gla_norm, lambda_q1, lambda_k1, lambda_q2, lambda_k2, diff_norm, w_out, pre_norm_mlp, post_norm_mlp, w_up, w_down):
    depth = ada_w.shape[0]
    bsz, seq, d = x.shape
    cos_t, sin_t = _rope_tables(positions)
    vec = lambda t: t.reshape(1, -1)
    for l in range(depth):
        lambda_init = 0.8 - 0.6 * math.exp(-0.3 * l)
        ada3 = _ada(c, ada_w[l], ada_b[l]).reshape(bsz, N_ADA, d)
        gate_w_pad = jnp.pad(gla_gate_w[l], ((0, GATE_PAD - GLA_GATE_RANK), (0, 0))).astype(BF16)
        gq, gk, gv, gog, glog, dq, dk, dvt = _inproj(
            x, ada3, vec(pre_norm_mix[l]), cos_t, sin_t, _pack_w_in(w_in[l]), gate_w_pad,
            vec(gla_gate_b[l]), tm=512)
        go = _gla(gq, gk, gv, glog, gog, vec(gla_norm[l]), ts=512)
        do = _diff_attn(dq, dk, dvt, vec(lambda_q1[l]), vec(lambda_k1[l]), vec(lambda_q2[l]),
                        vec(lambda_k2[l]), vec(diff_norm[l]), lambda_init, tq=256, heads=4)
        x = _out_mlp(x, go, do, ada3, vec(post_norm_mix[l]), vec(pre_norm_mlp[l]),
                     vec(post_norm_mlp[l]), w_out[l].astype(BF16), w_up[l].astype(BF16),
                     w_down[l].astype(BF16), tm=512, ff_chunk=1024)
    return x
```

```python
import functools
import math

import jax
import jax.numpy as jnp
from jax import lax
from jax.experimental import pallas as pl
from jax.experimental.pallas import tpu as pltpu

F32 = jnp.float32
BF16 = jnp.bfloat16

GLA_HEADS = 4
GLA_DK = 64
GLA_DV = 128
GLA_QK = GLA_HEADS * GLA_DK
GLA_WIDTH = GLA_HEADS * GLA_DV
GLA_GATE_RANK = 16
GLA_GATE_NORM = 16.0
GLA_CHUNK = 64
DIFF_HEADS = 4
DIFF_DQK = 64
DIFF_DV = 128
DIFF_QK = DIFF_HEADS * 2 * DIFF_DQK
DIFF_WIDTH = DIFF_HEADS * DIFF_DV
ROPE_THETA = 10000.0
EPS = 1e-6
N_ADA = 6

LANES = 128
GATE_PAD = LANES
ROPE_HALF = DIFF_DQK // 2
_SUM_ROWS = 16
LOG2E = math.log2(math.e)
VMEM_LIMIT = 56 * 1024 * 1024

_COL_GQ = 0
_COL_GK = _COL_GQ + GLA_QK
_COL_GV = _COL_GK + GLA_QK
_COL_GOG = _COL_GV + GLA_WIDTH
_COL_DQ = _COL_GOG + GLA_WIDTH
_COL_DK = _COL_DQ + DIFF_QK
_COL_DV = _COL_DK + DIFF_QK
_COL_LR = _COL_DV + DIFF_WIDTH
_COL_END = _COL_LR + GATE_PAD


def _dot(a, b):
    return jnp.dot(a, b, preferred_element_type=F32)


def _dot_nt(a, b):
    return lax.dot_general(a, b, (((1,), (1,)), ((), ())), preferred_element_type=F32)


def _dot_tn(a, b):
    return lax.dot_general(a, b, (((0,), (0,)), ((), ())), preferred_element_type=F32)


def _rms(t):
    return t * lax.rsqrt(jnp.mean(t * t, axis=-1, keepdims=True) + EPS)


def _silu(t):
    return t * (1.0 / (1.0 + jnp.exp(-t)))


def _ada_kernel(c_ref, w_ref, b_ref, o_ref):
    ca = _silu(c_ref[...]).astype(BF16)
    o_ref[...] = _dot(ca, w_ref[...].astype(BF16)) + b_ref[...]


def _ada(c, ada_w, ada_b):
    bsz, d = c.shape
    n = ada_w.shape[1]
    tn = d
    return pl.pallas_call(
        _ada_kernel,
        grid=(n // tn,),
        in_specs=[
            pl.BlockSpec((bsz, d), lambda j: (0, 0)),
            pl.BlockSpec((d, tn), lambda j: (0, j)),
            pl.BlockSpec((1, tn), lambda j: (0, j)),
        ],
        out_specs=pl.BlockSpec((bsz, tn), lambda j: (0, j)),
        out_shape=jax.ShapeDtypeStruct((bsz, n), F32),
        compiler_params=pltpu.CompilerParams(
            dimension_semantics=("parallel",), vmem_limit_bytes=VMEM_LIMIT),
        name="ada_ln",
    )(c, ada_w, ada_b.reshape(1, n))


def _inproj_kernel(x_ref, ada_ref, pn_ref, cos_ref, sin_ref, w_ref, gw_ref, gb_ref,
                   gq_ref, gk_ref, gv_ref, gog_ref, glog_ref, dq_ref, dk_ref, dvt_ref):
    x = x_ref[0]
    ada = ada_ref[0]
    h = _rms(x) * pn_ref[...] * (1.0 + ada[1:2]) + ada[0:1]
    hb = h.astype(BF16)

    def proj(lo, hi):
        return _dot(hb, w_ref[:, lo:hi])

    gq_ref[0] = (proj(_COL_GQ, _COL_GK) * (GLA_DK ** -0.5)).astype(BF16)
    gk_ref[0] = proj(_COL_GK, _COL_GV).astype(BF16)
    gv_ref[0] = proj(_COL_GV, _COL_GOG).astype(BF16)
    gog_ref[0] = proj(_COL_GOG, _COL_DQ).astype(BF16)
    dvt_ref[0] = proj(_COL_DV, _COL_LR).T.astype(BF16)

    lr = proj(_COL_LR, _COL_END).astype(BF16)
    z = _dot(lr, gw_ref[...]) + gb_ref[...]
    log_sig = jnp.minimum(z, 0.0) - jnp.log1p(jnp.exp(-jnp.abs(z)))
    glog_ref[0] = log_sig * (1.0 / GLA_GATE_NORM)

    cos = cos_ref[0]
    sin = sin_ref[0]
    lane = lax.broadcasted_iota(jnp.int32, cos.shape, 1)
    first_half = (lane % DIFF_DQK) < ROPE_HALF

    def rope_store(out_ref, lo, scale):
        t = proj(lo, lo + DIFF_QK)
        for c in range(DIFF_QK // LANES):
            tc = t[:, c * LANES:(c + 1) * LANES]
            partner = jnp.where(first_half, pltpu.roll(tc, LANES - ROPE_HALF, 1),
                                pltpu.roll(tc, ROPE_HALF, 1))
            out_ref[0, :, c * LANES:(c + 1) * LANES] = ((tc * cos + partner * sin) * scale).astype(BF16)

    rope_store(dq_ref, _COL_DQ, DIFF_DQK ** -0.5 * LOG2E)
    rope_store(dk_ref, _COL_DK, 1.0)


def _inproj(x, ada3, pre_norm, cos_t, sin_t, w_pack, gate_w_pad, gate_b, tm):
    bsz, seq, d = x.shape
    grid = (bsz, seq // tm)
    row = lambda b, i: (b, i, 0)
    const2 = lambda b, i: (0, 0)

    def out(width, dtype):
        return (pl.BlockSpec((1, tm, width), row), jax.ShapeDtypeStruct((bsz, seq, width), dtype))

    dvt = (pl.BlockSpec((1, DIFF_WIDTH, tm), lambda b, i: (b, 0, i)),
           jax.ShapeDtypeStruct((bsz, DIFF_WIDTH, seq), BF16))
    outs = [out(GLA_QK, BF16), out(GLA_QK, BF16), out(GLA_WIDTH, BF16), out(GLA_WIDTH, BF16),
            out(GLA_QK, F32), out(DIFF_QK, BF16), out(DIFF_QK, BF16), dvt]
    return pl.pallas_call(
        _inproj_kernel,
        grid=grid,
        in_specs=[
            pl.BlockSpec((1, tm, d), row),
            pl.BlockSpec((1, N_ADA, d), lambda b, i: (b, 0, 0)),
            pl.BlockSpec((1, d), const2),
            pl.BlockSpec((1, tm, LANES), row),
            pl.BlockSpec((1, tm, LANES), row),
            pl.BlockSpec((d, _COL_END), const2),
            pl.BlockSpec((GATE_PAD, GLA_QK), const2),
            pl.BlockSpec((1, GLA_QK), const2),
        ],
        out_specs=[o[0] for o in outs],
        out_shape=[o[1] for o in outs],
        compiler_params=pltpu.CompilerParams(
            dimension_semantics=("parallel", "parallel"), vmem_limit_bytes=VMEM_LIMIT),
        name="in_proj",
    )(x, ada3, pre_norm, cos_t, sin_t, w_pack, gate_w_pad, gate_b)


def _head_stack(t, lane_head):
    return jnp.concatenate(
        [jnp.where(lane_head == h, t, jnp.zeros_like(t)) for h in range(GLA_HEADS)], axis=0)


def _gla_kernel(q_ref, k_ref, v_ref, g_ref, og_ref, gn_ref, o_ref, state_ref, *, chunks):
    C = GLA_CHUNK

    @pl.when(pl.program_id(1) == 0)
    def _():
        state_ref[...] = jnp.zeros_like(state_ref)

    row = lax.broadcasted_iota(jnp.int32, (C, C), 0)
    col = lax.broadcasted_iota(jnp.int32, (C, C), 1)
    cum_mat = (row >= col).astype(BF16)
    srow = lax.broadcasted_iota(jnp.int32, (GLA_HEADS * C, C), 0)
    scol = lax.broadcasted_iota(jnp.int32, (GLA_HEADS * C, C), 1)
    causal = (srow % C) >= scol
    lane_head = lax.broadcasted_iota(jnp.int32, (C, GLA_QK), 1) // GLA_DK
    state_head = lax.broadcasted_iota(jnp.int32, (GLA_DV, GLA_QK), 1) // GLA_DK
    gn = gn_ref[...]

    chunk_rows = [slice(c * C, (c + 1) * C) for c in range(chunks)]

    cum = []
    for rows in chunk_rows:
        g = g_ref[0, rows, :]
        g_hi = g.astype(BF16)
        g_lo = (g - g_hi.astype(F32)).astype(BF16)
        cum.append(_dot(cum_mat, g_hi) + _dot(cum_mat, g_lo))

    q_ins, scores, upds, decays = [], [], [], []
    for rows, b in zip(chunk_rows, cum):
        b_last = b[C - 1:C, :]
        b_mid = b[C // 2 - 1:C // 2, :]
        q = q_ref[0, rows, :].astype(F32)
        k = k_ref[0, rows, :].astype(F32)
        q_ins.append((q * jnp.exp(b)).astype(BF16))
        q_mid = (q * jnp.exp(b - b_mid)).astype(BF16)
        k_mid = (k * jnp.exp(b_mid - b)).astype(BF16)
        k_out = (k * jnp.exp(b_last - b)).astype(BF16)
        s = _dot_nt(_head_stack(q_mid, lane_head), k_mid)
        scores.append(jnp.where(causal, s, 0.0).astype(BF16))
        upd = _dot_tn(v_ref[0, rows, :], k_out)
        own = upd[:GLA_DV, :]
        for h in range(1, GLA_HEADS):
            own = jnp.where(state_head == h, upd[h * GLA_DV:(h + 1) * GLA_DV, :], own)
        upds.append(own)
        decays.append(jnp.exp(b_last))

    state = state_ref[...]
    states = []
    for upd, decay in zip(upds, decays):
        states.append(state.astype(BF16))
        state = state * decay + upd
    state_ref[...] = state

    for rows, q_in, s, st in zip(chunk_rows, q_ins, scores, states):
        inter = _dot_nt(_head_stack(q_in, lane_head), st)
        for h in range(GLA_HEADS):
            hr = slice(h * C, (h + 1) * C)
            hv = slice(h * GLA_DV, (h + 1) * GLA_DV)
            o = inter[hr, :] + _dot(s[hr, :], v_ref[0, rows, hv])
            og = og_ref[0, rows, hv].astype(F32)
            o_ref[0, rows, hv] = (_rms(o) * gn * _silu(og)).astype(o_ref.dtype)


def _gla(gq, gk, gv, glog, gog, gla_norm, ts):
    bsz, seq, _ = gq.shape
    row = lambda b, i: (b, i, 0)
    return pl.pallas_call(
        functools.partial(_gla_kernel, chunks=ts // GLA_CHUNK),
        grid=(bsz, seq // ts),
        in_specs=[
            pl.BlockSpec((1, ts, GLA_QK), row),
            pl.BlockSpec((1, ts, GLA_QK), row),
            pl.BlockSpec((1, ts, GLA_WIDTH), row),
            pl.BlockSpec((1, ts, GLA_QK), row),
            pl.BlockSpec((1, ts, GLA_WIDTH), row),
            pl.BlockSpec((1, GLA_DV), lambda b, i: (0, 0)),
        ],
        out_specs=pl.BlockSpec((1, ts, GLA_WIDTH), row),
        out_shape=jax.ShapeDtypeStruct((bsz, seq, GLA_WIDTH), BF16),
        scratch_shapes=[pltpu.VMEM((GLA_DV, GLA_QK), F32)],
        compiler_params=pltpu.CompilerParams(
            dimension_semantics=("parallel", "arbitrary"), vmem_limit_bytes=VMEM_LIMIT),
        name="gla",
    )(gq, gk, gv, glog, gog, gla_norm)


def _diff_kernel(q_ref, k_ref, vt_ref, lq1_ref, lk1_ref, lq2_ref, lk2_ref, dn_ref, o_ref,
                 acc_ref, s_ref, *, tq, heads, lambda_init):
    qi = pl.program_id(2)
    lane = lax.broadcasted_iota(jnp.int32, (tq, 2 * DIFF_DQK), 1)
    ones_rows = jnp.ones((_SUM_ROWS, tq), BF16)
    acc_ref[...] = jnp.zeros_like(acc_ref)

    def head_cols(h, width):
        return slice(h * width, (h + 1) * width)

    qs = []
    for h in range(heads):
        q = q_ref[0, :, head_cols(h, 2 * DIFF_DQK)]
        zero = jnp.zeros_like(q)
        qs.append(jnp.concatenate([jnp.where(lane < DIFF_DQK, q, zero),
                                   jnp.where(lane >= DIFF_DQK, q, zero)], axis=0))

    def score(slot, j):
        for h in range(heads):
            kb = k_ref[0, pl.ds(pl.multiple_of(j * tq, tq), tq), head_cols(h, 2 * DIFF_DQK)]
            s_ref[slot, h] = _dot_nt(kb, qs[h])

    def consume(slot, j, ms, masked):
        out = []
        for h in range(heads):
            vtb = vt_ref[0, head_cols(h, DIFF_DV), pl.ds(pl.multiple_of(j * tq, tq), tq)]
            m_new, p = [], []
            for c in range(2 * tq // LANES):
                cols = slice(c * LANES, (c + 1) * LANES)
                st = s_ref[slot, h, :, cols]
                if masked:
                    key = lax.broadcasted_iota(jnp.int32, st.shape, 0)
                    qry = lax.broadcasted_iota(jnp.int32, st.shape, 1) + (c * LANES) % tq
                    st = jnp.where(key <= qry, st, -jnp.inf)
                mc = jnp.maximum(ms[h][:, cols], jnp.max(st, axis=0, keepdims=True))
                p.append(jnp.exp2(st - mc).astype(BF16))
                m_new.append(mc)
            m_new = jnp.concatenate(m_new, axis=1)
            alpha = jnp.exp2(ms[h] - m_new)
            v_aug = jnp.concatenate([vtb, ones_rows], axis=0)
            acc_ref[h] = alpha * acc_ref[h] + _dot(v_aug, jnp.concatenate(p, axis=1))
            out.append(m_new)
        return tuple(out)

    def block_pair(i, ms):
        score(1, 2 * i + 1)
        ms = consume(0, 2 * i, ms, False)
        score(0, 2 * i + 2)
        return consume(1, 2 * i + 1, ms, False)

    score(0, 0)
    ms = lax.fori_loop(0, qi // 2, block_pair,
                       (jnp.full((1, 2 * tq), -jnp.inf, F32),) * heads)

    @pl.when(qi % 2 == 0)
    def _():
        consume(0, qi, ms, True)

    @pl.when(qi % 2 == 1)
    def _():
        score(1, qi)
        consume(1, qi, consume(0, qi - 1, ms, False), True)

    lam = (jnp.exp(jnp.sum(lq1_ref[...] * lk1_ref[...], axis=-1, keepdims=True))
           - jnp.exp(jnp.sum(lq2_ref[...] * lk2_ref[...], axis=-1, keepdims=True))
           + lambda_init)
    for h in range(heads):
        acc = acc_ref[h]
        o_all = acc[:DIFF_DV] * (1.0 / acc[DIFF_DV:DIFF_DV + 1])
        ot = o_all[:, :tq] - lam * o_all[:, tq:]
        ot = ot * lax.rsqrt(jnp.mean(ot * ot, axis=0, keepdims=True) + EPS)
        o_ref[0, :, head_cols(h, DIFF_DV)] = (
            ot.T * dn_ref[...] * (1.0 - lambda_init)).astype(o_ref.dtype)


def _diff_attn(dq, dk, dvt, lq1, lk1, lq2, lk2, diff_norm, lambda_init, tq, heads):
    bsz, seq, _ = dq.shape
    vec = lambda n: pl.BlockSpec((1, n), lambda b, g, i: (0, 0))
    return pl.pallas_call(
        functools.partial(_diff_kernel, tq=tq, heads=heads, lambda_init=lambda_init),
        grid=(bsz, DIFF_HEADS // heads, seq // tq),
        in_specs=[
            pl.BlockSpec((1, tq, heads * 2 * DIFF_DQK), lambda b, g, i: (b, i, g)),
            pl.BlockSpec((1, seq, heads * 2 * DIFF_DQK), lambda b, g, i: (b, 0, g)),
            pl.BlockSpec((1, heads * DIFF_DV, seq), lambda b, g, i: (b, g, 0)),
            vec(DIFF_DQK), vec(DIFF_DQK), vec(DIFF_DQK), vec(DIFF_DQK), vec(DIFF_DV),
        ],
        out_specs=pl.BlockSpec((1, tq, heads * DIFF_DV), lambda b, g, i: (b, i, g)),
        out_shape=jax.ShapeDtypeStruct((bsz, seq, DIFF_WIDTH), BF16),
        scratch_shapes=[pltpu.VMEM((heads, DIFF_DV + _SUM_ROWS, 2 * tq), F32),
                        pltpu.VMEM((2, heads, tq, 2 * tq), F32)],
        compiler_params=pltpu.CompilerParams(
            dimension_semantics=("parallel", "parallel", "parallel"),
            vmem_limit_bytes=VMEM_LIMIT),
        name="diff_attn",
    )(dq, dk, dvt, lq1, lk1, lq2, lk2, diff_norm)


def _out_mlp_kernel(x_ref, go_ref, do_ref, ada_ref, pn_mix_ref, pre_mlp_ref, pn_mlp_ref,
                    wo_ref, wu_ref, wd_ref, o_ref, u_ref, *, ff_chunk):
    x = x_ref[0]
    ada = ada_ref[0]
    gt_a, sh_m, sc_m, gt_m = ada[2:3], ada[3:4], ada[4:5], ada[5:6]
    y = _dot(go_ref[0], wo_ref[:GLA_WIDTH, :]) + _dot(do_ref[0], wo_ref[GLA_WIDTH:, :])
    x1 = x + gt_a * (_rms(y) * pn_mix_ref[...])
    h = (_rms(x1) * pre_mlp_ref[...] * (1.0 + sc_m) + sh_m).astype(BF16)
    d_ff = wu_ref.shape[1]
    for f in range(d_ff // ff_chunk):
        cols = slice(f * ff_chunk, (f + 1) * ff_chunk)
        u = jnp.maximum(_dot(h, wu_ref[:, cols]), 0.0)
        u_ref[:, cols] = (u * u).astype(BF16)
    y2 = _dot(u_ref[...], wd_ref[...])
    o_ref[0] = x1 + gt_m * (_rms(y2) * pn_mlp_ref[...])


def _out_mlp(x, go, do, ada3, post_mix, pre_mlp, post_mlp, w_out, w_up, w_down, tm, ff_chunk):
    bsz, seq, d = x.shape
    d_ff = w_up.shape[1]
    row = lambda b, i: (b, i, 0)
    const2 = lambda b, i: (0, 0)
    resident = functools.partial(pl.BlockSpec, index_map=const2, pipeline_mode=pl.Buffered(1))
    return pl.pallas_call(
        functools.partial(_out_mlp_kernel, ff_chunk=ff_chunk),
        grid=(bsz, seq // tm),
        in_specs=[
            pl.BlockSpec((1, tm, d), row),
            pl.BlockSpec((1, tm, GLA_WIDTH), row),
            pl.BlockSpec((1, tm, DIFF_WIDTH), row),
            pl.BlockSpec((1, N_ADA, d), lambda b, i: (b, 0, 0)),
            pl.BlockSpec((1, d), const2),
            pl.BlockSpec((1, d), const2),
            pl.BlockSpec((1, d), const2),
            resident((d, d)),
            resident((d, d_ff)),
            resident((d_ff, d)),
        ],
        out_specs=pl.BlockSpec((1, tm, d), row),
        out_shape=jax.ShapeDtypeStruct((bsz, seq, d), F32),
        scratch_shapes=[pltpu.VMEM((tm, d_ff), BF16)],
        compiler_params=pltpu.CompilerParams(
            dimension_semantics=("parallel", "parallel"), vmem_limit_bytes=VMEM_LIMIT),
        name="out_mlp",
    )(x, go, do, ada3, post_mix, pre_mlp, post_mlp, w_out, w_up, w_down)


def _rope_tables(positions):
    half = DIFF_DQK // 2
    inv_freq = 1.0 / (ROPE_THETA ** (jnp.arange(0, DIFF_DQK, 2, dtype=F32) / DIFF_DQK))
    ang = positions.astype(F32)[..., None] * inv_freq
    cos, sin = jnp.cos(ang), jnp.sin(ang)
    reps = LANES // (2 * half)
    cos_t = jnp.tile(cos, (1, 1, 2 * reps))
    sin_t = jnp.tile(jnp.concatenate([-sin, sin], axis=-1), (1, 1, reps))
    return cos_t, sin_t


def _pack_w_in(w_in):
    sizes = (GLA_QK, GLA_QK, GLA_WIDTH, GLA_GATE_RANK, GLA_WIDTH, DIFF_QK, DIFF_QK, DIFF_WIDTH)
    offsets = [sum(sizes[:n]) for n in range(1, len(sizes))]
    gq, gk, gv, lr, og, dq, dk, dv = jnp.split(w_in, offsets, axis=-1)
    lr = jnp.pad(lr, ((0, 0), (0, GATE_PAD - GLA_GATE_RANK)))
    return jnp.concatenate([gq, gk, gv, og, dq, dk, dv, lr], axis=-1).astype(BF16)


def kernel(x, c, positions, ada_w, ada_b, pre_norm_mix, post_norm_mix, w_in, gla_gate_w, gla_gate_b, gla_norm, lambda_q1, lambda_k1, lambda_q2, lambda_k2, diff_norm, w_out, pre_norm_mlp, post_norm_mlp, w_up, w_down):
    depth = ada_w.shape[0]
    bsz, seq, d = x.shape
    cos_t, sin_t = _rope_tables(positions)
    vec = lambda t: t.reshape(1, -1)
    for l in range(depth):
        lambda_init = 0.8 - 0.6 * math.exp(-0.3 * l)
        ada3 = _ada(c, ada_w[l], ada_b[l]).reshape(bsz, N_ADA, d)
        gate_w_pad = jnp.pad(gla_gate_w[l], ((0, GATE_PAD - GLA_GATE_RANK), (0, 0))).astype(BF16)
        gq, gk, gv, gog, glog, dq, dk, dvt = _inproj(
            x, ada3, vec(pre_norm_mix[l]), cos_t, sin_t, _pack_w_in(w_in[l]), gate_w_pad,
            vec(gla_gate_b[l]), tm=512)
        go = _gla(gq, gk, gv, glog, gog, vec(gla_norm[l]), ts=512)
        do = _diff_attn(dq, dk, dvt, vec(lambda_q1[l]), vec(lambda_k1[l]), vec(lambda_q2[l]),
                        vec(lambda_k2[l]), vec(diff_norm[l]), lambda_init, tq=256, heads=4)
        x = _out_mlp(x, go, do, ada3, vec(post_norm_mix[l]), vec(pre_norm_mlp[l]),
                     vec(post_norm_mlp[l]), w_out[l].astype(BF16), w_up[l].astype(BF16),
                     w_down[l].astype(BF16), tm=512, ff_chunk=1024)
    return x
```

```python
import functools
import math

import jax
import jax.numpy as jnp
from jax import lax
from jax.experimental import pallas as pl
from jax.experimental.pallas import tpu as pltpu

F32 = jnp.float32
BF16 = jnp.bfloat16

GLA_HEADS = 4
GLA_DK = 64
GLA_DV = 128
GLA_QK = GLA_HEADS * GLA_DK
GLA_WIDTH = GLA_HEADS * GLA_DV
GLA_GATE_RANK = 16
GLA_GATE_NORM = 16.0
GLA_CHUNK = 64
DIFF_HEADS = 4
DIFF_DQK = 64
DIFF_DV = 128
DIFF_QK = DIFF_HEADS * 2 * DIFF_DQK
DIFF_WIDTH = DIFF_HEADS * DIFF_DV
ROPE_THETA = 10000.0
EPS = 1e-6
N_ADA = 6

LANES = 128
GATE_PAD = LANES
ROPE_HALF = DIFF_DQK // 2
_SUM_ROWS = 16
LOG2E = math.log2(math.e)
VMEM_LIMIT = 56 * 1024 * 1024

_COL_GQ = 0
_COL_GK = _COL_GQ + GLA_QK
_COL_GV = _COL_GK + GLA_QK
_COL_GOG = _COL_GV + GLA_WIDTH
_COL_DQ = _COL_GOG + GLA_WIDTH
_COL_DK = _COL_DQ + DIFF_QK
_COL_DV = _COL_DK + DIFF_QK
_COL_LR = _COL_DV + DIFF_WIDTH
_COL_END = _COL_LR + GATE_PAD


def _dot(a, b):
    return jnp.dot(a, b, preferred_element_type=F32)


def _dot_nt(a, b):
    return lax.dot_general(a, b, (((1,), (1,)), ((), ())), preferred_element_type=F32)


def _dot_tn(a, b):
    return lax.dot_general(a, b, (((0,), (0,)), ((), ())), preferred_element_type=F32)


def _rms(t):
    return t * lax.rsqrt(jnp.mean(t * t, axis=-1, keepdims=True) + EPS)


def _silu(t):
    return t * (1.0 / (1.0 + jnp.exp(-t)))


def _ada_kernel(c_ref, w_ref, b_ref, o_ref):
    ca = _silu(c_ref[...]).astype(BF16)
    o_ref[0] = _dot(ca, w_ref[...].astype(BF16)) + b_ref[...]


def _ada(c, ada_w, ada_b):
    bsz, d = c.shape
    n = ada_w.shape[1]
    tn = d
    return pl.pallas_call(
        _ada_kernel,
        grid=(n // tn,),
        in_specs=[
            pl.BlockSpec((bsz, d), lambda j: (0, 0)),
            pl.BlockSpec((d, tn), lambda j: (0, j)),
            pl.BlockSpec((1, tn), lambda j: (0, j)),
        ],
        out_specs=pl.BlockSpec((1, bsz, tn), lambda j: (j, 0, 0)),
        out_shape=jax.ShapeDtypeStruct((n // tn, bsz, tn), F32),
        compiler_params=pltpu.CompilerParams(
            dimension_semantics=("parallel",), vmem_limit_bytes=VMEM_LIMIT),
        name="ada_ln",
    )(c, ada_w, ada_b.reshape(1, n))


def _ada_rows(ada_ref, first, count):
    b = pl.program_id(0)
    return [ada_ref[n, pl.ds(b, 1), :] for n in range(first, first + count)]


def _inproj_kernel(x_ref, ada_ref, pn_ref, cs_ref, w_ref, gw_ref, gb_ref,
                   gq_ref, gk_ref, gv_ref, gog_ref, glog_ref, dq_ref, dk_ref, dvt_ref):
    x = x_ref[0]
    shift, scale = _ada_rows(ada_ref, 0, 2)
    h = _rms(x) * pn_ref[...] * (1.0 + scale) + shift
    hb = h.astype(BF16)

    def proj(lo, hi):
        return _dot(hb, w_ref[:, lo:hi])

    dvt_ref[0] = proj(_COL_DV, _COL_LR).T.astype(BF16)

    cs = cs_ref[0]
    lane = lax.broadcasted_iota(jnp.int32, cs.shape, 1)
    group = lane // ROPE_HALF
    r1, r2, r3 = (pltpu.roll(cs, n * ROPE_HALF, 1) for n in (1, 2, 3))
    cos = jnp.where(group == 0, cs, jnp.where(group == 1, r1, jnp.where(group == 2, r2, r3)))
    sin = jnp.where(group == 0, -r3, jnp.where(group == 1, cs, jnp.where(group == 2, -r1, r2)))
    first_half = (lane % DIFF_DQK) < ROPE_HALF

    def rope_store(out_ref, lo, scale):
        t = proj(lo, lo + DIFF_QK)
        for c in range(DIFF_QK // LANES):
            tc = t[:, c * LANES:(c + 1) * LANES]
            partner = jnp.where(first_half, pltpu.roll(tc, LANES - ROPE_HALF, 1),
                                pltpu.roll(tc, ROPE_HALF, 1))
            out_ref[0, :, c * LANES:(c + 1) * LANES] = ((tc * cos + partner * sin) * scale).astype(BF16)

    rope_store(dq_ref, _COL_DQ, DIFF_DQK ** -0.5 * LOG2E)
    rope_store(dk_ref, _COL_DK, 1.0)

    lr = proj(_COL_LR, _COL_END).astype(BF16)
    z = _dot(lr, gw_ref[...]) + gb_ref[...]
    log_sig = jnp.minimum(z, 0.0) - jnp.log1p(jnp.exp(-jnp.abs(z)))
    glog_ref[0] = log_sig * (1.0 / GLA_GATE_NORM)

    gq_ref[0] = (proj(_COL_GQ, _COL_GK) * (GLA_DK ** -0.5)).astype(BF16)
    gk_ref[0] = proj(_COL_GK, _COL_GV).astype(BF16)
    gv_ref[0] = proj(_COL_GV, _COL_GOG).astype(BF16)
    gog_ref[0] = proj(_COL_GOG, _COL_DQ).astype(BF16)


def _inproj(x, ada3, pre_norm, cos_sin, w_pack, gate_w_pad, gate_b, tm):
    bsz, seq, d = x.shape
    grid = (bsz, seq // tm)
    row = lambda b, i: (b, i, 0)
    const2 = lambda b, i: (0, 0)

    def out(width, dtype):
        return (pl.BlockSpec((1, tm, width), row), jax.ShapeDtypeStruct((bsz, seq, width), dtype))

    dvt = (pl.BlockSpec((1, DIFF_WIDTH, tm), lambda b, i: (b, 0, i)),
           jax.ShapeDtypeStruct((bsz, DIFF_WIDTH, seq), BF16))
    outs = [out(GLA_QK, BF16), out(GLA_QK, BF16), out(GLA_WIDTH, BF16), out(GLA_WIDTH, BF16),
            out(GLA_QK, F32), out(DIFF_QK, BF16), out(DIFF_QK, BF16), dvt]
    return pl.pallas_call(
        _inproj_kernel,
        grid=grid,
        in_specs=[
            pl.BlockSpec((1, tm, d), row),
            pl.BlockSpec((N_ADA, bsz, d), lambda b, i: (0, 0, 0)),
            pl.BlockSpec((1, d), const2),
            pl.BlockSpec((1, tm, LANES), row),
            pl.BlockSpec((d, _COL_END), const2),
            pl.BlockSpec((GATE_PAD, GLA_QK), const2),
            pl.BlockSpec((1, GLA_QK), const2),
        ],
        out_specs=[o[0] for o in outs],
        out_shape=[o[1] for o in outs],
        compiler_params=pltpu.CompilerParams(
            dimension_semantics=("parallel", "parallel"), vmem_limit_bytes=VMEM_LIMIT),
        name="in_proj",
    )(x, ada3, pre_norm, cos_sin, w_pack, gate_w_pad, gate_b)


def _head_stack(t, lane_head):
    return jnp.concatenate(
        [jnp.where(lane_head == h, t, jnp.zeros_like(t)) for h in range(GLA_HEADS)], axis=0)


def _gla_kernel(q_ref, k_ref, v_ref, g_ref, og_ref, gn_ref, o_ref, state_ref, *, chunks):
    C = GLA_CHUNK

    @pl.when(pl.program_id(1) == 0)
    def _():
        state_ref[...] = jnp.zeros_like(state_ref)

    row = lax.broadcasted_iota(jnp.int32, (C, C), 0)
    col = lax.broadcasted_iota(jnp.int32, (C, C), 1)
    cum_mat = (row >= col).astype(BF16)
    srow = lax.broadcasted_iota(jnp.int32, (GLA_HEADS * C, C), 0)
    scol = lax.broadcasted_iota(jnp.int32, (GLA_HEADS * C, C), 1)
    causal = (srow % C) >= scol
    lane_head = lax.broadcasted_iota(jnp.int32, (C, GLA_QK), 1) // GLA_DK
    state_head = lax.broadcasted_iota(jnp.int32, (GLA_DV, GLA_QK), 1) // GLA_DK
    gn = gn_ref[...]

    chunk_rows = [slice(c * C, (c + 1) * C) for c in range(chunks)]

    cum = []
    for rows in chunk_rows:
        g = g_ref[0, rows, :]
        g_hi = g.astype(BF16)
        g_lo = (g - g_hi.astype(F32)).astype(BF16)
        cum.append(_dot(cum_mat, g_hi) + _dot(cum_mat, g_lo))

    q_ins, scores, upds, decays = [], [], [], []
    for rows, b in zip(chunk_rows, cum):
        b_last = b[C - 1:C, :]
        b_mid = b[C // 2 - 1:C // 2, :]
        q = q_ref[0, rows, :].astype(F32)
        k = k_ref[0, rows, :].astype(F32)
        q_ins.append((q * jnp.exp(b)).astype(BF16))
        q_mid = (q * jnp.exp(b - b_mid)).astype(BF16)
        k_mid = (k * jnp.exp(b_mid - b)).astype(BF16)
        k_out = (k * jnp.exp(b_last - b)).astype(BF16)
        s = _dot_nt(_head_stack(q_mid, lane_head), k_mid)
        scores.append(jnp.where(causal, s, 0.0).astype(BF16))
        upd = _dot_tn(v_ref[0, rows, :], k_out)
        own = upd[:GLA_DV, :]
        for h in range(1, GLA_HEADS):
            own = jnp.where(state_head == h, upd[h * GLA_DV:(h + 1) * GLA_DV, :], own)
        upds.append(own)
        decays.append(jnp.exp(b_last))

    state = state_ref[...]
    states = []
    for upd, decay in zip(upds, decays):
        states.append(state.astype(BF16))
        state = state * decay + upd
    state_ref[...] = state

    for rows, q_in, s, st in zip(chunk_rows, q_ins, scores, states):
        inter = _dot_nt(_head_stack(q_in, lane_head), st)
        for h in range(GLA_HEADS):
            hr = slice(h * C, (h + 1) * C)
            hv = slice(h * GLA_DV, (h + 1) * GLA_DV)
            o = inter[hr, :] + _dot(s[hr, :], v_ref[0, rows, hv])
            og = og_ref[0, rows, hv].astype(F32)
            o_ref[0, rows, hv] = (_rms(o) * gn * _silu(og)).astype(o_ref.dtype)


def _gla(gq, gk, gv, glog, gog, gla_norm, ts):
    bsz, seq, _ = gq.shape
    row = lambda b, i: (b, i, 0)
    return pl.pallas_call(
        functools.partial(_gla_kernel, chunks=ts // GLA_CHUNK),
        grid=(bsz, seq // ts),
        in_specs=[
            pl.BlockSpec((1, ts, GLA_QK), row),
            pl.BlockSpec((1, ts, GLA_QK), row),
            pl.BlockSpec((1, ts, GLA_WIDTH), row),
            pl.BlockSpec((1, ts, GLA_QK), row),
            pl.BlockSpec((1, ts, GLA_WIDTH), row),
            pl.BlockSpec((1, GLA_DV), lambda b, i: (0, 0)),
        ],
        out_specs=pl.BlockSpec((1, ts, GLA_WIDTH), row),
        out_shape=jax.ShapeDtypeStruct((bsz, seq, GLA_WIDTH), BF16),
        scratch_shapes=[pltpu.VMEM((GLA_DV, GLA_QK), F32)],
        compiler_params=pltpu.CompilerParams(
            dimension_semantics=("parallel", "arbitrary"), vmem_limit_bytes=VMEM_LIMIT),
        name="gla",
    )(gq, gk, gv, glog, gog, gla_norm)


def _diff_kernel(q_ref, k_ref, vt_ref, lq1_ref, lk1_ref, lq2_ref, lk2_ref, dn_ref, o_ref,
                 acc_ref, s_ref, *, tq, heads, lambda_init):
    qi = pl.program_id(2)
    lane = lax.broadcasted_iota(jnp.int32, (tq, 2 * DIFF_DQK), 1)
    ones_rows = jnp.ones((_SUM_ROWS, tq), BF16)
    acc_ref[...] = jnp.zeros_like(acc_ref)

    def head_cols(h, width):
        return slice(h * width, (h + 1) * width)

    qs = []
    for h in range(heads):
        q = q_ref[0, :, head_cols(h, 2 * DIFF_DQK)]
        zero = jnp.zeros_like(q)
        qs.append(jnp.concatenate([jnp.where(lane < DIFF_DQK, q, zero),
                                   jnp.where(lane >= DIFF_DQK, q, zero)], axis=0))

    def score(slot, j):
        for h in range(heads):
            kb = k_ref[0, pl.ds(pl.multiple_of(j * tq, tq), tq), head_cols(h, 2 * DIFF_DQK)]
            s_ref[slot, h] = _dot_nt(kb, qs[h])

    def consume(slot, j, ms, masked):
        out = []
        for h in range(heads):
            vtb = vt_ref[0, head_cols(h, DIFF_DV), pl.ds(pl.multiple_of(j * tq, tq), tq)]
            m_new, p = [], []
            for c in range(2 * tq // LANES):
                cols = slice(c * LANES, (c + 1) * LANES)
                st = s_ref[slot, h, :, cols]
                if masked:
                    key = lax.broadcasted_iota(jnp.int32, st.shape, 0)
                    qry = lax.broadcasted_iota(jnp.int32, st.shape, 1) + (c * LANES) % tq
                    st = jnp.where(key <= qry, st, -jnp.inf)
                mc = jnp.maximum(ms[h][:, cols], jnp.max(st, axis=0, keepdims=True))
                p.append(jnp.exp2(st - mc).astype(BF16))
                m_new.append(mc)
            m_new = jnp.concatenate(m_new, axis=1)
            alpha = jnp.exp2(ms[h] - m_new)
            v_aug = jnp.concatenate([vtb, ones_rows], axis=0)
            acc_ref[h] = alpha * acc_ref[h] + _dot(v_aug, jnp.concatenate(p, axis=1))
            out.append(m_new)
        return tuple(out)

    def block_pair(i, ms):
        score(1, 2 * i + 1)
        ms = consume(0, 2 * i, ms, False)
        score(0, 2 * i + 2)
        return consume(1, 2 * i + 1, ms, False)

    score(0, 0)
    ms = lax.fori_loop(0, qi // 2, block_pair,
                       (jnp.full((1, 2 * tq), -jnp.inf, F32),) * heads)

    @pl.when(qi % 2 == 0)
    def _():
        consume(0, qi, ms, True)

    @pl.when(qi % 2 == 1)
    def _():
        score(1, qi)
        consume(1, qi, consume(0, qi - 1, ms, False), True)

    lam = (jnp.exp(jnp.sum(lq1_ref[...] * lk1_ref[...], axis=-1, keepdims=True))
           - jnp.exp(jnp.sum(lq2_ref[...] * lk2_ref[...], axis=-1, keepdims=True))
           + lambda_init)
    for h in range(heads):
        acc = acc_ref[h]
        o_all = acc[:DIFF_DV] * (1.0 / acc[DIFF_DV:DIFF_DV + 1])
        ot = o_all[:, :tq] - lam * o_all[:, tq:]
        ot = ot * lax.rsqrt(jnp.mean(ot * ot, axis=0, keepdims=True) + EPS)
        o_ref[0, :, head_cols(h, DIFF_DV)] = (
            ot.T * dn_ref[...] * (1.0 - lambda_init)).astype(o_ref.dtype)


def _diff_attn(dq, dk, dvt, lq1, lk1, lq2, lk2, diff_norm, lambda_init, tq, heads):
    bsz, seq, _ = dq.shape
    vec = lambda n: pl.BlockSpec((1, n), lambda b, g, i: (0, 0))
    return pl.pallas_call(
        functools.partial(_diff_kernel, tq=tq, heads=heads, lambda_init=lambda_init),
        grid=(bsz, DIFF_HEADS // heads, seq // tq),
        in_specs=[
            pl.BlockSpec((1, tq, heads * 2 * DIFF_DQK), lambda b, g, i: (b, i, g)),
            pl.BlockSpec((1, seq, heads * 2 * DIFF_DQK), lambda b, g, i: (b, 0, g)),
            pl.BlockSpec((1, heads * DIFF_DV, seq), lambda b, g, i: (b, g, 0)),
            vec(DIFF_DQK), vec(DIFF_DQK), vec(DIFF_DQK), vec(DIFF_DQK), vec(DIFF_DV),
        ],
        out_specs=pl.BlockSpec((1, tq, heads * DIFF_DV), lambda b, g, i: (b, i, g)),
        out_shape=jax.ShapeDtypeStruct((bsz, seq, DIFF_WIDTH), BF16),
        scratch_shapes=[pltpu.VMEM((heads, DIFF_DV + _SUM_ROWS, 2 * tq), F32),
                        pltpu.VMEM((2, heads, tq, 2 * tq), F32)],
        compiler_params=pltpu.CompilerParams(
            dimension_semantics=("parallel", "parallel", "parallel"),
            vmem_limit_bytes=VMEM_LIMIT),
        name="diff_attn",
    )(dq, dk, dvt, lq1, lk1, lq2, lk2, diff_norm)


def _out_mlp_kernel(x_ref, go_ref, do_ref, ada_ref, pn_mix_ref, pre_mlp_ref, pn_mlp_ref,
                    wo_ref, wu_ref, wd_ref, o_ref, u_ref, *, ff_chunk):
    x = x_ref[0]
    gt_a, sh_m, sc_m, gt_m = _ada_rows(ada_ref, 2, 4)
    y = _dot(go_ref[0], wo_ref[:GLA_WIDTH, :]) + _dot(do_ref[0], wo_ref[GLA_WIDTH:, :])
    x1 = x + gt_a * (_rms(y) * pn_mix_ref[...])
    h = (_rms(x1) * pre_mlp_ref[...] * (1.0 + sc_m) + sh_m).astype(BF16)
    d_ff = wu_ref.shape[1]
    for f in range(d_ff // ff_chunk):
        cols = slice(f * ff_chunk, (f + 1) * ff_chunk)
        u = jnp.maximum(_dot(h, wu_ref[:, cols]), 0.0)
        u_ref[:, cols] = (u * u).astype(BF16)
    y2 = _dot(u_ref[...], wd_ref[...])
    o_ref[0] = x1 + gt_m * (_rms(y2) * pn_mlp_ref[...])


def _out_mlp(x, go, do, ada3, post_mix, pre_mlp, post_mlp, w_out, w_up, w_down, tm, ff_chunk):
    bsz, seq, d = x.shape
    d_ff = w_up.shape[1]
    row = lambda b, i: (b, i, 0)
    const2 = lambda b, i: (0, 0)
    resident = functools.partial(pl.BlockSpec, index_map=const2, pipeline_mode=pl.Buffered(1))
    return pl.pallas_call(
        functools.partial(_out_mlp_kernel, ff_chunk=ff_chunk),
        grid=(bsz, seq // tm),
        in_specs=[
            pl.BlockSpec((1, tm, d), row),
            pl.BlockSpec((1, tm, GLA_WIDTH), row),
            pl.BlockSpec((1, tm, DIFF_WIDTH), row),
            pl.BlockSpec((N_ADA, bsz, d), lambda b, i: (0, 0, 0)),
            pl.BlockSpec((1, d), const2),
            pl.BlockSpec((1, d), const2),
            pl.BlockSpec((1, d), const2),
            resident((d, d)),
            resident((d, d_ff)),
            resident((d_ff, d)),
        ],
        out_specs=pl.BlockSpec((1, tm, d), row),
        out_shape=jax.ShapeDtypeStruct((bsz, seq, d), F32),
        scratch_shapes=[pltpu.VMEM((tm, d_ff), BF16)],
        compiler_params=pltpu.CompilerParams(
            dimension_semantics=("parallel", "parallel"), vmem_limit_bytes=VMEM_LIMIT),
        name="out_mlp",
    )(x, go, do, ada3, post_mix, pre_mlp, post_mlp, w_out, w_up, w_down)


def _rope_table(positions):
    inv_freq = 1.0 / (ROPE_THETA ** (jnp.arange(0, DIFF_DQK, 2, dtype=F32) / DIFF_DQK))
    ang = positions.astype(F32)[..., None] * inv_freq
    pad = jnp.zeros(ang.shape[:-1] + (LANES - 2 * ROPE_HALF,), F32)
    return jnp.concatenate([jnp.cos(ang), jnp.sin(ang), pad], axis=-1)


def _pack_w_in(w_in):
    sizes = (GLA_QK, GLA_QK, GLA_WIDTH, GLA_GATE_RANK, GLA_WIDTH, DIFF_QK, DIFF_QK, DIFF_WIDTH)
    offsets = [sum(sizes[:n]) for n in range(1, len(sizes))]
    gq, gk, gv, lr, og, dq, dk, dv = jnp.split(w_in, offsets, axis=-1)
    lr = jnp.pad(lr, ((0, 0), (0, GATE_PAD - GLA_GATE_RANK)))
    return jnp.concatenate([gq, gk, gv, og, dq, dk, dv, lr], axis=-1).astype(BF16)


def kernel(x, c, positions, ada_w, ada_b, pre_norm_mix, post_norm_mix, w_in, gla_gate_w, gla_gate_b, gla_norm, lambda_q1, lambda_k1, lambda_q2, lambda_k2, diff_norm, w_out, pre_norm_mlp, post_norm_mlp, w_up, w_down):
    depth = ada_w.shape[0]
    bsz, seq, d = x.shape
    cos_sin = _rope_table(positions)
    vec = lambda t: t.reshape(1, -1)
    for l in range(depth):
        lambda_init = 0.8 - 0.6 * math.exp(-0.3 * l)
        ada3 = _ada(c, ada_w[l], ada_b[l])
        gate_w_pad = jnp.pad(gla_gate_w[l], ((0, GATE_PAD - GLA_GATE_RANK), (0, 0))).astype(BF16)
        gq, gk, gv, gog, glog, dq, dk, dvt = _inproj(
            x, ada3, vec(pre_norm_mix[l]), cos_sin, _pack_w_in(w_in[l]), gate_w_pad,
            vec(gla_gate_b[l]), tm=512)
        go = _gla(gq, gk, gv, glog, gog, vec(gla_norm[l]), ts=512)
        do = _diff_attn(dq, dk, dvt, vec(lambda_q1[l]), vec(lambda_k1[l]), vec(lambda_q2[l]),
                        vec(lambda_k2[l]), vec(diff_norm[l]), lambda_init, tq=256, heads=4)
        x = _out_mlp(x, go, do, ada3, vec(post_norm_mix[l]), vec(pre_norm_mlp[l]),
                     vec(post_norm_mlp[l]), w_out[l].astype(BF16), w_up[l].astype(BF16),
                     w_down[l].astype(BF16), tm=512, ff_chunk=1024)
    return x
```

```python
import functools
import math

import jax
import jax.numpy as jnp
from jax import lax
from jax.experimental import pallas as pl
from jax.experimental.pallas import tpu as pltpu

F32 = jnp.float32
BF16 = jnp.bfloat16

GLA_HEADS = 4
GLA_DK = 64
GLA_DV = 128
GLA_QK = GLA_HEADS * GLA_DK
GLA_WIDTH = GLA_HEADS * GLA_DV
GLA_GATE_RANK = 16
GLA_GATE_NORM = 16.0
GLA_CHUNK = 64
DIFF_HEADS = 4
DIFF_DQK = 64
DIFF_DV = 128
DIFF_QK = DIFF_HEADS * 2 * DIFF_DQK
DIFF_WIDTH = DIFF_HEADS * DIFF_DV
ROPE_THETA = 10000.0
EPS = 1e-6
N_ADA = 6

LANES = 128
GATE_PAD = LANES
ROPE_HALF = DIFF_DQK // 2
_SUM_ROWS = 16
LOG2E = math.log2(math.e)
VMEM_LIMIT = 56 * 1024 * 1024

_COL_GQ = 0
_COL_GK = _COL_GQ + GLA_QK
_COL_GV = _COL_GK + GLA_QK
_COL_GOG = _COL_GV + GLA_WIDTH
_COL_DQ = _COL_GOG + GLA_WIDTH
_COL_DK = _COL_DQ + DIFF_QK
_COL_DV = _COL_DK + DIFF_QK
_COL_LR = _COL_DV + DIFF_WIDTH
_COL_END = _COL_LR + GATE_PAD


def _dot(a, b):
    return jnp.dot(a, b, preferred_element_type=F32)


def _dot_nt(a, b):
    return lax.dot_general(a, b, (((1,), (1,)), ((), ())), preferred_element_type=F32)


def _dot_tn(a, b):
    return lax.dot_general(a, b, (((0,), (0,)), ((), ())), preferred_element_type=F32)


def _rms(t):
    return t * lax.rsqrt(jnp.mean(t * t, axis=-1, keepdims=True) + EPS)


def _silu(t):
    return t * (1.0 / (1.0 + jnp.exp(-t)))


def _ada_kernel(c_ref, w_ref, b_ref, o_ref):
    ca = _silu(c_ref[...]).astype(BF16)
    o_ref[0] = _dot(ca, w_ref[...].astype(BF16)) + b_ref[...]


def _ada(c, ada_w, ada_b):
    bsz, d = c.shape
    n = ada_w.shape[1]
    tn = d
    return pl.pallas_call(
        _ada_kernel,
        grid=(n // tn,),
        in_specs=[
            pl.BlockSpec((bsz, d), lambda j: (0, 0)),
            pl.BlockSpec((d, tn), lambda j: (0, j)),
            pl.BlockSpec((1, tn), lambda j: (0, j)),
        ],
        out_specs=pl.BlockSpec((1, bsz, tn), lambda j: (j, 0, 0)),
        out_shape=jax.ShapeDtypeStruct((n // tn, bsz, tn), F32),
        compiler_params=pltpu.CompilerParams(
            dimension_semantics=("parallel",), vmem_limit_bytes=VMEM_LIMIT),
        name="ada_ln",
    )(c, ada_w, ada_b.reshape(1, n))


def _ada_rows(ada_ref, first, count):
    b = pl.program_id(0)
    return [ada_ref[n, pl.ds(b, 1), :] for n in range(first, first + count)]


def _inproj_kernel(x_ref, ada_ref, pn_ref, pos_ref, freq_ref, w_ref, gw_ref, gb_ref,
                   gq_ref, gk_ref, gv_ref, gog_ref, glog_ref, dq_ref, dk_ref, dvt_ref):
    x = x_ref[0]
    shift, scale = _ada_rows(ada_ref, 0, 2)
    h = _rms(x) * pn_ref[...] * (1.0 + scale) + shift
    hb = h.astype(BF16)

    def proj(lo, hi):
        return _dot(hb, w_ref[:, lo:hi])

    dvt_ref[0] = proj(_COL_DV, _COL_LR).T.astype(BF16)

    ang = pos_ref[0] * freq_ref[...]
    lane = lax.broadcasted_iota(jnp.int32, ang.shape, 1)
    first_half = (lane % DIFF_DQK) < ROPE_HALF
    cos = jnp.cos(ang)
    sin = jnp.where(first_half, -jnp.sin(ang), jnp.sin(ang))

    def rope_store(out_ref, lo, scale):
        t = proj(lo, lo + DIFF_QK)
        for c in range(DIFF_QK // LANES):
            tc = t[:, c * LANES:(c + 1) * LANES]
            partner = jnp.where(first_half, pltpu.roll(tc, LANES - ROPE_HALF, 1),
                                pltpu.roll(tc, ROPE_HALF, 1))
            out_ref[0, :, c * LANES:(c + 1) * LANES] = ((tc * cos + partner * sin) * scale).astype(BF16)

    rope_store(dq_ref, _COL_DQ, DIFF_DQK ** -0.5 * LOG2E)
    rope_store(dk_ref, _COL_DK, 1.0)

    lr = proj(_COL_LR, _COL_END).astype(BF16)
    z = _dot(lr, gw_ref[...]) + gb_ref[...]
    log_sig = jnp.minimum(z, 0.0) - jnp.log1p(jnp.exp(-jnp.abs(z)))
    glog_ref[0] = log_sig * (1.0 / GLA_GATE_NORM)

    gq_ref[0] = (proj(_COL_GQ, _COL_GK) * (GLA_DK ** -0.5)).astype(BF16)
    gk_ref[0] = proj(_COL_GK, _COL_GV).astype(BF16)
    gv_ref[0] = proj(_COL_GV, _COL_GOG).astype(BF16)
    gog_ref[0] = proj(_COL_GOG, _COL_DQ).astype(BF16)


def _inproj(x, ada3, pre_norm, pos, freq, w_pack, gate_w_pad, gate_b, tm):
    bsz, seq, d = x.shape
    grid = (bsz, seq // tm)
    row = lambda b, i: (b, i, 0)
    const2 = lambda b, i: (0, 0)

    def out(width, dtype):
        return (pl.BlockSpec((1, tm, width), row), jax.ShapeDtypeStruct((bsz, seq, width), dtype))

    dvt = (pl.BlockSpec((1, DIFF_WIDTH, tm), lambda b, i: (b, 0, i)),
           jax.ShapeDtypeStruct((bsz, DIFF_WIDTH, seq), BF16))
    outs = [out(GLA_QK, BF16), out(GLA_QK, BF16), out(GLA_WIDTH, BF16), out(GLA_WIDTH, BF16),
            out(GLA_QK, F32), out(DIFF_QK, BF16), out(DIFF_QK, BF16), dvt]
    return pl.pallas_call(
        _inproj_kernel,
        grid=grid,
        in_specs=[
            pl.BlockSpec((1, tm, d), row),
            pl.BlockSpec((N_ADA, bsz, d), lambda b, i: (0, 0, 0)),
            pl.BlockSpec((1, d), const2),
            pl.BlockSpec((1, tm, 1), row),
            pl.BlockSpec((1, LANES), const2),
            pl.BlockSpec((d, _COL_END), const2),
            pl.BlockSpec((GATE_PAD, GLA_QK), const2),
            pl.BlockSpec((1, GLA_QK), const2),
        ],
        out_specs=[o[0] for o in outs],
        out_shape=[o[1] for o in outs],
        compiler_params=pltpu.CompilerParams(
            dimension_semantics=("parallel", "parallel"), vmem_limit_bytes=VMEM_LIMIT),
        name="in_proj",
    )(x, ada3, pre_norm, pos, freq, w_pack, gate_w_pad, gate_b)


def _head_stack(t, lane_head):
    return jnp.concatenate(
        [jnp.where(lane_head == h, t, jnp.zeros_like(t)) for h in range(GLA_HEADS)], axis=0)


def _gla_kernel(q_ref, k_ref, v_ref, g_ref, og_ref, gn_ref, o_ref, state_ref, *, chunks):
    C = GLA_CHUNK

    @pl.when(pl.program_id(1) == 0)
    def _():
        state_ref[...] = jnp.zeros_like(state_ref)

    row = lax.broadcasted_iota(jnp.int32, (C, C), 0)
    col = lax.broadcasted_iota(jnp.int32, (C, C), 1)
    cum_mat = (row >= col).astype(BF16)
    srow = lax.broadcasted_iota(jnp.int32, (GLA_HEADS * C, C), 0)
    scol = lax.broadcasted_iota(jnp.int32, (GLA_HEADS * C, C), 1)
    causal = (srow % C) >= scol
    lane_head = lax.broadcasted_iota(jnp.int32, (C, GLA_QK), 1) // GLA_DK
    state_head = lax.broadcasted_iota(jnp.int32, (GLA_DV, GLA_QK), 1) // GLA_DK
    gn = gn_ref[...]

    chunk_rows = [slice(c * C, (c + 1) * C) for c in range(chunks)]

    cum = []
    for rows in chunk_rows:
        g = g_ref[0, rows, :]
        g_hi = g.astype(BF16)
        g_lo = (g - g_hi.astype(F32)).astype(BF16)
        cum.append(_dot(cum_mat, g_hi) + _dot(cum_mat, g_lo))

    q_ins, scores, upds, decays = [], [], [], []
    for rows, b in zip(chunk_rows, cum):
        b_last = b[C - 1:C, :]
        b_mid = b[C // 2 - 1:C // 2, :]
        q = q_ref[0, rows, :].astype(F32)
        k = k_ref[0, rows, :].astype(F32)
        q_ins.append((q * jnp.exp(b)).astype(BF16))
        q_mid = (q * jnp.exp(b - b_mid)).astype(BF16)
        k_mid = (k * jnp.exp(b_mid - b)).astype(BF16)
        k_out = (k * jnp.exp(b_last - b)).astype(BF16)
        s = _dot_nt(_head_stack(q_mid, lane_head), k_mid)
        scores.append(jnp.where(causal, s, 0.0).astype(BF16))
        upd = _dot_tn(v_ref[0, rows, :], k_out)
        own = upd[:GLA_DV, :]
        for h in range(1, GLA_HEADS):
            own = jnp.where(state_head == h, upd[h * GLA_DV:(h + 1) * GLA_DV, :], own)
        upds.append(own)
        decays.append(jnp.exp(b_last))

    state = state_ref[...]
    states = []
    for upd, decay in zip(upds, decays):
        states.append(state.astype(BF16))
        state = state * decay + upd
    state_ref[...] = state

    for rows, q_in, s, st in zip(chunk_rows, q_ins, scores, states):
        inter = _dot_nt(_head_stack(q_in, lane_head), st)
        for h in range(GLA_HEADS):
            hr = slice(h * C, (h + 1) * C)
            hv = slice(h * GLA_DV, (h + 1) * GLA_DV)
            o = inter[hr, :] + _dot(s[hr, :], v_ref[0, rows, hv])
            og = og_ref[0, rows, hv].astype(F32)
            o_ref[0, rows, hv] = (_rms(o) * gn * _silu(og)).astype(o_ref.dtype)


def _gla(gq, gk, gv, glog, gog, gla_norm, ts):
    bsz, seq, _ = gq.shape
    row = lambda b, i: (b, i, 0)
    return pl.pallas_call(
        functools.partial(_gla_kernel, chunks=ts // GLA_CHUNK),
        grid=(bsz, seq // ts),
        in_specs=[
            pl.BlockSpec((1, ts, GLA_QK), row),
            pl.BlockSpec((1, ts, GLA_QK), row),
            pl.BlockSpec((1, ts, GLA_WIDTH), row),
            pl.BlockSpec((1, ts, GLA_QK), row),
            pl.BlockSpec((1, ts, GLA_WIDTH), row),
            pl.BlockSpec((1, GLA_DV), lambda b, i: (0, 0)),
        ],
        out_specs=pl.BlockSpec((1, ts, GLA_WIDTH), row),
        out_shape=jax.ShapeDtypeStruct((bsz, seq, GLA_WIDTH), BF16),
        scratch_shapes=[pltpu.VMEM((GLA_DV, GLA_QK), F32)],
        compiler_params=pltpu.CompilerParams(
            dimension_semantics=("parallel", "arbitrary"), vmem_limit_bytes=VMEM_LIMIT),
        name="gla",
    )(gq, gk, gv, glog, gog, gla_norm)


def _diff_kernel(q_ref, k_ref, vt_ref, lq1_ref, lk1_ref, lq2_ref, lk2_ref, dn_ref, o_ref,
                 acc_ref, s_ref, *, tq, heads, lambda_init):
    qi = pl.program_id(2)
    lane = lax.broadcasted_iota(jnp.int32, (tq, 2 * DIFF_DQK), 1)
    ones_rows = jnp.ones((_SUM_ROWS, tq), BF16)
    acc_ref[...] = jnp.zeros_like(acc_ref)

    def head_cols(h, width):
        return slice(h * width, (h + 1) * width)

    qs = []
    for h in range(heads):
        q = q_ref[0, :, head_cols(h, 2 * DIFF_DQK)]
        zero = jnp.zeros_like(q)
        qs.append(jnp.concatenate([jnp.where(lane < DIFF_DQK, q, zero),
                                   jnp.where(lane >= DIFF_DQK, q, zero)], axis=0))

    def score(slot, j):
        for h in range(heads):
            kb = k_ref[0, pl.ds(pl.multiple_of(j * tq, tq), tq), head_cols(h, 2 * DIFF_DQK)]
            s_ref[slot, h] = _dot_nt(kb, qs[h])

    def consume(slot, j, ms, masked):
        out = []
        for h in range(heads):
            vtb = vt_ref[0, head_cols(h, DIFF_DV), pl.ds(pl.multiple_of(j * tq, tq), tq)]
            m_new, p = [], []
            for c in range(2 * tq // LANES):
                cols = slice(c * LANES, (c + 1) * LANES)
                st = s_ref[slot, h, :, cols]
                if masked:
                    key = lax.broadcasted_iota(jnp.int32, st.shape, 0)
                    qry = lax.broadcasted_iota(jnp.int32, st.shape, 1) + (c * LANES) % tq
                    st = jnp.where(key <= qry, st, -jnp.inf)
                mc = jnp.maximum(ms[h][:, cols], jnp.max(st, axis=0, keepdims=True))
                p.append(jnp.exp2(st - mc).astype(BF16))
                m_new.append(mc)
            m_new = jnp.concatenate(m_new, axis=1)
            alpha = jnp.exp2(ms[h] - m_new)
            v_aug = jnp.concatenate([vtb, ones_rows], axis=0)
            acc_ref[h] = alpha * acc_ref[h] + _dot(v_aug, jnp.concatenate(p, axis=1))
            out.append(m_new)
        return tuple(out)

    def block_pair(i, ms):
        score(1, 2 * i + 1)
        ms = consume(0, 2 * i, ms, False)
        score(0, 2 * i + 2)
        return consume(1, 2 * i + 1, ms, False)

    score(0, 0)
    ms = lax.fori_loop(0, qi // 2, block_pair,
                       (jnp.full((1, 2 * tq), -jnp.inf, F32),) * heads)

    @pl.when(qi % 2 == 0)
    def _():
        consume(0, qi, ms, True)

    @pl.when(qi % 2 == 1)
    def _():
        score(1, qi)
        consume(1, qi, consume(0, qi - 1, ms, False), True)

    lam = (jnp.exp(jnp.sum(lq1_ref[...] * lk1_ref[...], axis=-1, keepdims=True))
           - jnp.exp(jnp.sum(lq2_ref[...] * lk2_ref[...], axis=-1, keepdims=True))
           + lambda_init)
    for h in range(heads):
        acc = acc_ref[h]
        o_all = acc[:DIFF_DV] * (1.0 / acc[DIFF_DV:DIFF_DV + 1])
        ot = o_all[:, :tq] - lam * o_all[:, tq:]
        ot = ot * lax.rsqrt(jnp.mean(ot * ot, axis=0, keepdims=True) + EPS)
        o_ref[0, :, head_cols(h, DIFF_DV)] = (
            ot.T * dn_ref[...] * (1.0 - lambda_init)).astype(o_ref.dtype)


def _diff_attn(dq, dk, dvt, lq1, lk1, lq2, lk2, diff_norm, lambda_init, tq, heads):
    bsz, seq, _ = dq.shape
    vec = lambda n: pl.BlockSpec((1, n), lambda b, g, i: (0, 0))
    return pl.pallas_call(
        functools.partial(_diff_kernel, tq=tq, heads=heads, lambda_init=lambda_init),
        grid=(bsz, DIFF_HEADS // heads, seq // tq),
        in_specs=[
            pl.BlockSpec((1, tq, heads * 2 * DIFF_DQK), lambda b, g, i: (b, i, g)),
            pl.BlockSpec((1, seq, heads * 2 * DIFF_DQK), lambda b, g, i: (b, 0, g)),
            pl.BlockSpec((1, heads * DIFF_DV, seq), lambda b, g, i: (b, g, 0)),
            vec(DIFF_DQK), vec(DIFF_DQK), vec(DIFF_DQK), vec(DIFF_DQK), vec(DIFF_DV),
        ],
        out_specs=pl.BlockSpec((1, tq, heads * DIFF_DV), lambda b, g, i: (b, i, g)),
        out_shape=jax.ShapeDtypeStruct((bsz, seq, DIFF_WIDTH), BF16),
        scratch_shapes=[pltpu.VMEM((heads, DIFF_DV + _SUM_ROWS, 2 * tq), F32),
                        pltpu.VMEM((2, heads, tq, 2 * tq), F32)],
        compiler_params=pltpu.CompilerParams(
            dimension_semantics=("parallel", "parallel", "parallel"),
            vmem_limit_bytes=VMEM_LIMIT),
        name="diff_attn",
    )(dq, dk, dvt, lq1, lk1, lq2, lk2, diff_norm)


def _out_mlp_kernel(x_ref, go_ref, do_ref, ada_ref, pn_mix_ref, pre_mlp_ref, pn_mlp_ref,
                    wo_ref, wu_ref, wd_ref, o_ref, u_ref, *, ff_chunk):
    x = x_ref[0]
    gt_a, sh_m, sc_m, gt_m = _ada_rows(ada_ref, 2, 4)
    y = _dot(go_ref[0], wo_ref[:GLA_WIDTH, :]) + _dot(do_ref[0], wo_ref[GLA_WIDTH:, :])
    x1 = x + gt_a * (_rms(y) * pn_mix_ref[...])
    h = (_rms(x1) * pre_mlp_ref[...] * (1.0 + sc_m) + sh_m).astype(BF16)
    d_ff = wu_ref.shape[1]
    for f in range(d_ff // ff_chunk):
        cols = slice(f * ff_chunk, (f + 1) * ff_chunk)
        u = jnp.maximum(_dot(h, wu_ref[:, cols]), 0.0)
        u_ref[:, cols] = (u * u).astype(BF16)
    y2 = _dot(u_ref[...], wd_ref[...])
    o_ref[0] = x1 + gt_m * (_rms(y2) * pn_mlp_ref[...])


def _out_mlp(x, go, do, ada3, post_mix, pre_mlp, post_mlp, w_out, w_up, w_down, tm, ff_chunk):
    bsz, seq, d = x.shape
    d_ff = w_up.shape[1]
    row = lambda b, i: (b, i, 0)
    const2 = lambda b, i: (0, 0)
    resident = functools.partial(pl.BlockSpec, index_map=const2, pipeline_mode=pl.Buffered(1))
    return pl.pallas_call(
        functools.partial(_out_mlp_kernel, ff_chunk=ff_chunk),
        grid=(bsz, seq // tm),
        in_specs=[
            pl.BlockSpec((1, tm, d), row),
            pl.BlockSpec((1, tm, GLA_WIDTH), row),
            pl.BlockSpec((1, tm, DIFF_WIDTH), row),
            pl.BlockSpec((N_ADA, bsz, d), lambda b, i: (0, 0, 0)),
            pl.BlockSpec((1, d), const2),
            pl.BlockSpec((1, d), const2),
            pl.BlockSpec((1, d), const2),
            resident((d, d)),
            resident((d, d_ff)),
            resident((d_ff, d)),
        ],
        out_specs=pl.BlockSpec((1, tm, d), row),
        out_shape=jax.ShapeDtypeStruct((bsz, seq, d), F32),
        scratch_shapes=[pltpu.VMEM((tm, d_ff), BF16)],
        compiler_params=pltpu.CompilerParams(
            dimension_semantics=("parallel", "parallel"), vmem_limit_bytes=VMEM_LIMIT),
        name="out_mlp",
    )(x, go, do, ada3, post_mix, pre_mlp, post_mlp, w_out, w_up, w_down)


def _rope_freq():
    inv_freq = 1.0 / (ROPE_THETA ** (jnp.arange(0, DIFF_DQK, 2, dtype=F32) / DIFF_DQK))
    return jnp.tile(inv_freq, LANES // ROPE_HALF).reshape(1, LANES)


def _pack_w_in(w_in):
    sizes = (GLA_QK, GLA_QK, GLA_WIDTH, GLA_GATE_RANK, GLA_WIDTH, DIFF_QK, DIFF_QK, DIFF_WIDTH)
    offsets = [sum(sizes[:n]) for n in range(1, len(sizes))]
    gq, gk, gv, lr, og, dq, dk, dv = jnp.split(w_in, offsets, axis=-1)
    lr = jnp.pad(lr, ((0, 0), (0, GATE_PAD - GLA_GATE_RANK)))
    return jnp.concatenate([gq, gk, gv, og, dq, dk, dv, lr], axis=-1).astype(BF16)


def kernel(x, c, positions, ada_w, ada_b, pre_norm_mix, post_norm_mix, w_in, gla_gate_w, gla_gate_b, gla_norm, lambda_q1, lambda_k1, lambda_q2, lambda_k2, diff_norm, w_out, pre_norm_mlp, post_norm_mlp, w_up, w_down):
    depth = ada_w.shape[0]
    bsz, seq, d = x.shape
    pos = positions.astype(F32)[..., None]
    freq = _rope_freq()
    vec = lambda t: t.reshape(1, -1)
    for l in range(depth):
        lambda_init = 0.8 - 0.6 * math.exp(-0.3 * l)
        ada3 = _ada(c, ada_w[l], ada_b[l])
        gate_w_pad = jnp.pad(gla_gate_w[l], ((0, GATE_PAD - GLA_GATE_RANK), (0, 0))).astype(BF16)
        gq, gk, gv, gog, glog, dq, dk, dvt = _inproj(
            x, ada3, vec(pre_norm_mix[l]), pos, freq, _pack_w_in(w_in[l]), gate_w_pad,
            vec(gla_gate_b[l]), tm=512)
        go = _gla(gq, gk, gv, glog, gog, vec(gla_norm[l]), ts=512)
        do = _diff_attn(dq, dk, dvt, vec(lambda_q1[l]), vec(lambda_k1[l]), vec(lambda_q2[l]),
                        vec(lambda_k2[l]), vec(diff_norm[l]), lambda_init, tq=256, heads=4)
        x = _out_mlp(x, go, do, ada3, vec(post_norm_mix[l]), vec(pre_norm_mlp[l]),
                     vec(post_norm_mlp[l]), w_out[l].astype(BF16), w_up[l].astype(BF16),
                     w_down[l].astype(BF16), tm=512, ff_chunk=1024)
    return x
```

```python
import functools
import math

import jax
import jax.numpy as jnp
from jax import lax
from jax.experimental import pallas as pl
from jax.experimental.pallas import tpu as pltpu

F32 = jnp.float32
BF16 = jnp.bfloat16

GLA_HEADS = 4
GLA_DK = 64
GLA_DV = 128
GLA_QK = GLA_HEADS * GLA_DK
GLA_WIDTH = GLA_HEADS * GLA_DV
GLA_GATE_RANK = 16
GLA_GATE_NORM = 16.0
GLA_CHUNK = 64
DIFF_HEADS = 4
DIFF_DQK = 64
DIFF_DV = 128
DIFF_QK = DIFF_HEADS * 2 * DIFF_DQK
DIFF_WIDTH = DIFF_HEADS * DIFF_DV
ROPE_THETA = 10000.0
EPS = 1e-6
N_ADA = 6

LANES = 128
GATE_PAD = LANES
ROPE_HALF = DIFF_DQK // 2
_SUM_ROWS = 16
LOG2E = math.log2(math.e)
VMEM_LIMIT = 56 * 1024 * 1024

_COL_GQ = 0
_COL_GK = _COL_GQ + GLA_QK
_COL_GV = _COL_GK + GLA_QK
_COL_GOG = _COL_GV + GLA_WIDTH
_COL_DQ = _COL_GOG + GLA_WIDTH
_COL_DK = _COL_DQ + DIFF_QK
_COL_DV = _COL_DK + DIFF_QK
_COL_LR = _COL_DV + DIFF_WIDTH
_COL_END = _COL_LR + GATE_PAD


def _dot(a, b):
    return jnp.dot(a, b, preferred_element_type=F32)


def _dot_nt(a, b):
    return lax.dot_general(a, b, (((1,), (1,)), ((), ())), preferred_element_type=F32)


def _dot_tn(a, b):
    return lax.dot_general(a, b, (((0,), (0,)), ((), ())), preferred_element_type=F32)


def _rms(t):
    return t * lax.rsqrt(jnp.mean(t * t, axis=-1, keepdims=True) + EPS)


def _silu(t):
    return t * (1.0 / (1.0 + jnp.exp(-t)))


def _ada_kernel(c_ref, w_ref, b_ref, pos_ref, freq_ref, o_ref, cos_ref, sin_ref):
    ca = _silu(c_ref[...]).astype(BF16)
    o_ref[...] = _dot(ca, w_ref[...].astype(BF16)) + b_ref[...]
    ang = pos_ref[...] * freq_ref[...]
    cos_ref[...] = jnp.cos(ang)
    sin_ref[...] = jnp.sin(ang)


def _ada_rope(c, ada_w, ada_b, positions, steps=8):
    bsz, d = c.shape
    n = ada_w.shape[1]
    tn = n // steps
    per_row = LANES // ROPE_HALF
    inv_freq = 1.0 / (ROPE_THETA ** (jnp.arange(0, DIFF_DQK, 2, dtype=F32) / DIFF_DQK))
    freq = jnp.tile(inv_freq, per_row).reshape(1, LANES)
    pos = jnp.repeat(positions.astype(F32).reshape(-1, per_row), ROPE_HALF, axis=1)
    rows = pos.shape[0]
    tr = rows // steps
    table = (pl.BlockSpec((tr, LANES), lambda j: (j, 0)), jax.ShapeDtypeStruct((rows, LANES), F32))
    ada, cos, sin = pl.pallas_call(
        _ada_kernel,
        grid=(steps,),
        in_specs=[
            pl.BlockSpec((bsz, d), lambda j: (0, 0)),
            pl.BlockSpec((d, tn), lambda j: (0, j)),
            pl.BlockSpec((1, tn), lambda j: (0, j)),
            table[0],
            pl.BlockSpec((1, LANES), lambda j: (0, 0)),
        ],
        out_specs=[pl.BlockSpec((bsz, tn), lambda j: (0, j)), table[0], table[0]],
        out_shape=[jax.ShapeDtypeStruct((bsz, n), F32), table[1], table[1]],
        compiler_params=pltpu.CompilerParams(
            dimension_semantics=("parallel",), vmem_limit_bytes=VMEM_LIMIT),
        name="ada_ln",
    )(c, ada_w, ada_b.reshape(1, n), pos, freq)
    shape = positions.shape + (ROPE_HALF,)
    return ada, cos.reshape(shape), sin.reshape(shape)


def _ada_rows(ada_ref, first, count):
    b = pl.program_id(0)
    d = ada_ref.shape[1] // N_ADA
    return [ada_ref[pl.ds(b, 1), n * d:(n + 1) * d] for n in range(first, first + count)]


def _inproj_kernel(x_ref, ada_ref, pn_ref, cos_ref, sin_ref, w_ref, gw_ref, gb_ref,
                   gq_ref, gk_ref, gv_ref, gog_ref, glog_ref, dq_ref, dk_ref, dvt_ref):
    x = x_ref[0]
    shift, scale = _ada_rows(ada_ref, 0, 2)
    h = _rms(x) * pn_ref[...] * (1.0 + scale) + shift
    hb = h.astype(BF16)

    def proj(lo, hi):
        return _dot(hb, w_ref[:, lo:hi])

    dvt_ref[0] = proj(_COL_DV, _COL_LR).T.astype(BF16)

    c32, s32 = cos_ref[0], sin_ref[0]
    cs = jnp.concatenate([c32, s32, c32, s32], axis=1)
    lane = lax.broadcasted_iota(jnp.int32, cs.shape, 1)
    first_half = (lane % DIFF_DQK) < ROPE_HALF
    cos = jnp.where(first_half, cs, pltpu.roll(cs, ROPE_HALF, 1))
    sin = jnp.where(first_half, -pltpu.roll(cs, LANES - ROPE_HALF, 1), cs)

    def rope_store(out_ref, lo, scale):
        t = proj(lo, lo + DIFF_QK)
        for c in range(DIFF_QK // LANES):
            tc = t[:, c * LANES:(c + 1) * LANES]
            partner = jnp.where(first_half, pltpu.roll(tc, LANES - ROPE_HALF, 1),
                                pltpu.roll(tc, ROPE_HALF, 1))
            out_ref[0, :, c * LANES:(c + 1) * LANES] = ((tc * cos + partner * sin) * scale).astype(BF16)

    rope_store(dq_ref, _COL_DQ, DIFF_DQK ** -0.5 * LOG2E)
    rope_store(dk_ref, _COL_DK, 1.0)

    lr = proj(_COL_LR, _COL_END).astype(BF16)
    z = _dot(lr, gw_ref[...]) + gb_ref[...]
    log_sig = jnp.minimum(z, 0.0) - jnp.log1p(jnp.exp(-jnp.abs(z)))
    glog_ref[0] = log_sig * (1.0 / GLA_GATE_NORM)

    gq_ref[0] = (proj(_COL_GQ, _COL_GK) * (GLA_DK ** -0.5)).astype(BF16)
    gk_ref[0] = proj(_COL_GK, _COL_GV).astype(BF16)
    gv_ref[0] = proj(_COL_GV, _COL_GOG).astype(BF16)
    gog_ref[0] = proj(_COL_GOG, _COL_DQ).astype(BF16)


def _inproj(x, ada, pre_norm, cos, sin, w_pack, gate_w_pad, gate_b, tm):
    bsz, seq, d = x.shape
    grid = (bsz, seq // tm)
    row = lambda b, i: (b, i, 0)
    const2 = lambda b, i: (0, 0)

    def out(width, dtype):
        return (pl.BlockSpec((1, tm, width), row), jax.ShapeDtypeStruct((bsz, seq, width), dtype))

    dvt = (pl.BlockSpec((1, DIFF_WIDTH, tm), lambda b, i: (b, 0, i)),
           jax.ShapeDtypeStruct((bsz, DIFF_WIDTH, seq), BF16))
    outs = [out(GLA_QK, BF16), out(GLA_QK, BF16), out(GLA_WIDTH, BF16), out(GLA_WIDTH, BF16),
            out(GLA_QK, F32), out(DIFF_QK, BF16), out(DIFF_QK, BF16), dvt]
    return pl.pallas_call(
        _inproj_kernel,
        grid=grid,
        in_specs=[
            pl.BlockSpec((1, tm, d), row),
            pl.BlockSpec((bsz, N_ADA * d), const2),
            pl.BlockSpec((1, d), const2),
            pl.BlockSpec((1, tm, ROPE_HALF), row),
            pl.BlockSpec((1, tm, ROPE_HALF), row),
            pl.BlockSpec((d, _COL_END), const2),
            pl.BlockSpec((GATE_PAD, GLA_QK), const2),
            pl.BlockSpec((1, GLA_QK), const2),
        ],
        out_specs=[o[0] for o in outs],
        out_shape=[o[1] for o in outs],
        compiler_params=pltpu.CompilerParams(
            dimension_semantics=("parallel", "parallel"), vmem_limit_bytes=VMEM_LIMIT),
        name="in_proj",
    )(x, ada, pre_norm, cos, sin, w_pack, gate_w_pad, gate_b)


def _head_stack(t, lane_head):
    return jnp.concatenate(
        [jnp.where(lane_head == h, t, jnp.zeros_like(t)) for h in range(GLA_HEADS)], axis=0)


def _gla_kernel(q_ref, k_ref, v_ref, g_ref, og_ref, gn_ref, o_ref, state_ref, *, chunks):
    C = GLA_CHUNK

    @pl.when(pl.program_id(1) == 0)
    def _():
        state_ref[...] = jnp.zeros_like(state_ref)

    row = lax.broadcasted_iota(jnp.int32, (C, C), 0)
    col = lax.broadcasted_iota(jnp.int32, (C, C), 1)
    cum_mat = (row >= col).astype(BF16)
    srow = lax.broadcasted_iota(jnp.int32, (GLA_HEADS * C, C), 0)
    scol = lax.broadcasted_iota(jnp.int32, (GLA_HEADS * C, C), 1)
    causal = (srow % C) >= scol
    lane_head = lax.broadcasted_iota(jnp.int32, (C, GLA_QK), 1) // GLA_DK
    state_head = lax.broadcasted_iota(jnp.int32, (GLA_DV, GLA_QK), 1) // GLA_DK
    gn = gn_ref[...]

    chunk_rows = [slice(c * C, (c + 1) * C) for c in range(chunks)]

    cum = []
    for rows in chunk_rows:
        g = g_ref[0, rows, :]
        g_hi = g.astype(BF16)
        g_lo = (g - g_hi.astype(F32)).astype(BF16)
        cum.append(_dot(cum_mat, g_hi) + _dot(cum_mat, g_lo))

    q_ins, scores, upds, decays = [], [], [], []
    for rows, b in zip(chunk_rows, cum):
        b_last = b[C - 1:C, :]
        b_mid = b[C // 2 - 1:C // 2, :]
        q = q_ref[0, rows, :].astype(F32)
        k = k_ref[0, rows, :].astype(F32)
        q_ins.append((q * jnp.exp(b)).astype(BF16))
        q_mid = (q * jnp.exp(b - b_mid)).astype(BF16)
        k_mid = (k * jnp.exp(b_mid - b)).astype(BF16)
        k_out = (k * jnp.exp(b_last - b)).astype(BF16)
        s = _dot_nt(_head_stack(q_mid, lane_head), k_mid)
        scores.append(jnp.where(causal, s, 0.0).astype(BF16))
        upd = _dot_tn(v_ref[0, rows, :], k_out)
        own = upd[:GLA_DV, :]
        for h in range(1, GLA_HEADS):
            own = jnp.where(state_head == h, upd[h * GLA_DV:(h + 1) * GLA_DV, :], own)
        upds.append(own)
        decays.append(jnp.exp(b_last))

    state = state_ref[...]
    states = []
    for upd, decay in zip(upds, decays):
        states.append(state.astype(BF16))
        state = state * decay + upd
    state_ref[...] = state

    for rows, q_in, s, st in zip(chunk_rows, q_ins, scores, states):
        inter = _dot_nt(_head_stack(q_in, lane_head), st)
        for h in range(GLA_HEADS):
            hr = slice(h * C, (h + 1) * C)
            hv = slice(h * GLA_DV, (h + 1) * GLA_DV)
            o = inter[hr, :] + _dot(s[hr, :], v_ref[0, rows, hv])
            og = og_ref[0, rows, hv].astype(F32)
            o_ref[0, rows, hv] = (_rms(o) * gn * _silu(og)).astype(o_ref.dtype)


def _gla(gq, gk, gv, glog, gog, gla_norm, ts):
    bsz, seq, _ = gq.shape
    row = lambda b, i: (b, i, 0)
    return pl.pallas_call(
        functools.partial(_gla_kernel, chunks=ts // GLA_CHUNK),
        grid=(bsz, seq // ts),
        in_specs=[
            pl.BlockSpec((1, ts, GLA_QK), row),
            pl.BlockSpec((1, ts, GLA_QK), row),
            pl.BlockSpec((1, ts, GLA_WIDTH), row),
            pl.BlockSpec((1, ts, GLA_QK), row),
            pl.BlockSpec((1, ts, GLA_WIDTH), row),
            pl.BlockSpec((1, GLA_DV), lambda b, i: (0, 0)),
        ],
        out_specs=pl.BlockSpec((1, ts, GLA_WIDTH), row),
        out_shape=jax.ShapeDtypeStruct((bsz, seq, GLA_WIDTH), BF16),
        scratch_shapes=[pltpu.VMEM((GLA_DV, GLA_QK), F32)],
        compiler_params=pltpu.CompilerParams(
            dimension_semantics=("parallel", "arbitrary"), vmem_limit_bytes=VMEM_LIMIT),
        name="gla",
    )(gq, gk, gv, glog, gog, gla_norm)


def _diff_kernel(q_ref, k_ref, vt_ref, lq1_ref, lk1_ref, lq2_ref, lk2_ref, dn_ref, o_ref,
                 acc_ref, s_ref, *, tq, heads, lambda_init):
    qi = pl.program_id(2)
    lane = lax.broadcasted_iota(jnp.int32, (tq, 2 * DIFF_DQK), 1)
    ones_rows = jnp.ones((_SUM_ROWS, tq), BF16)
    acc_ref[...] = jnp.zeros_like(acc_ref)

    def head_cols(h, width):
        return slice(h * width, (h + 1) * width)

    qs = []
    for h in range(heads):
        q = q_ref[0, :, head_cols(h, 2 * DIFF_DQK)]
        zero = jnp.zeros_like(q)
        qs.append(jnp.concatenate([jnp.where(lane < DIFF_DQK, q, zero),
                                   jnp.where(lane >= DIFF_DQK, q, zero)], axis=0))

    def score(slot, j):
        for h in range(heads):
            kb = k_ref[0, pl.ds(pl.multiple_of(j * tq, tq), tq), head_cols(h, 2 * DIFF_DQK)]
            s_ref[slot, h] = _dot_nt(kb, qs[h])

    def consume(slot, j, ms, masked):
        out = []
        for h in range(heads):
            vtb = vt_ref[0, head_cols(h, DIFF_DV), pl.ds(pl.multiple_of(j * tq, tq), tq)]
            m_new, p = [], []
            for c in range(2 * tq // LANES):
                cols = slice(c * LANES, (c + 1) * LANES)
                st = s_ref[slot, h, :, cols]
                if masked:
                    key = lax.broadcasted_iota(jnp.int32, st.shape, 0)
                    qry = lax.broadcasted_iota(jnp.int32, st.shape, 1) + (c * LANES) % tq
                    st = jnp.where(key <= qry, st, -jnp.inf)
                mc = jnp.maximum(ms[h][:, cols], jnp.max(st, axis=0, keepdims=True))
                p.append(jnp.exp2(st - mc).astype(BF16))
                m_new.append(mc)
            m_new = jnp.concatenate(m_new, axis=1)
            alpha = jnp.exp2(ms[h] - m_new)
            v_aug = jnp.concatenate([vtb, ones_rows], axis=0)
            acc_ref[h] = alpha * acc_ref[h] + _dot(v_aug, jnp.concatenate(p, axis=1))
            out.append(m_new)
        return tuple(out)

    def block_pair(i, ms):
        score(1, 2 * i + 1)
        ms = consume(0, 2 * i, ms, False)
        score(0, 2 * i + 2)
        return consume(1, 2 * i + 1, ms, False)

    score(0, 0)
    ms = lax.fori_loop(0, qi // 2, block_pair,
                       (jnp.full((1, 2 * tq), -jnp.inf, F32),) * heads)

    @pl.when(qi % 2 == 0)
    def _():
        consume(0, qi, ms, True)

    @pl.when(qi % 2 == 1)
    def _():
        score(1, qi)
        consume(1, qi, consume(0, qi - 1, ms, False), True)

    lam = (jnp.exp(jnp.sum(lq1_ref[...] * lk1_ref[...], axis=-1, keepdims=True))
           - jnp.exp(jnp.sum(lq2_ref[...] * lk2_ref[...], axis=-1, keepdims=True))
           + lambda_init)
    for h in range(heads):
        acc = acc_ref[h]
        o_all = acc[:DIFF_DV] * (1.0 / acc[DIFF_DV:DIFF_DV + 1])
        ot = o_all[:, :tq] - lam * o_all[:, tq:]
        ot = ot * lax.rsqrt(jnp.mean(ot * ot, axis=0, keepdims=True) + EPS)
        o_ref[0, :, head_cols(h, DIFF_DV)] = (
            ot.T * dn_ref[...] * (1.0 - lambda_init)).astype(o_ref.dtype)


def _diff_attn(dq, dk, dvt, lq1, lk1, lq2, lk2, diff_norm, lambda_init, tq, heads):
    bsz, seq, _ = dq.shape
    vec = lambda n: pl.BlockSpec((1, n), lambda b, g, i: (0, 0))
    return pl.pallas_call(
        functools.partial(_diff_kernel, tq=tq, heads=heads, lambda_init=lambda_init),
        grid=(bsz, DIFF_HEADS // heads, seq // tq),
        in_specs=[
            pl.BlockSpec((1, tq, heads * 2 * DIFF_DQK), lambda b, g, i: (b, i, g)),
            pl.BlockSpec((1, seq, heads * 2 * DIFF_DQK), lambda b, g, i: (b, 0, g)),
            pl.BlockSpec((1, heads * DIFF_DV, seq), lambda b, g, i: (b, g, 0)),
            vec(DIFF_DQK), vec(DIFF_DQK), vec(DIFF_DQK), vec(DIFF_DQK), vec(DIFF_DV),
        ],
        out_specs=pl.BlockSpec((1, tq, heads * DIFF_DV), lambda b, g, i: (b, i, g)),
        out_shape=jax.ShapeDtypeStruct((bsz, seq, DIFF_WIDTH), BF16),
        scratch_shapes=[pltpu.VMEM((heads, DIFF_DV + _SUM_ROWS, 2 * tq), F32),
                        pltpu.VMEM((2, heads, tq, 2 * tq), F32)],
        compiler_params=pltpu.CompilerParams(
            dimension_semantics=("parallel", "parallel", "parallel"),
            vmem_limit_bytes=VMEM_LIMIT),
        name="diff_attn",
    )(dq, dk, dvt, lq1, lk1, lq2, lk2, diff_norm)


def _out_mlp_kernel(x_ref, go_ref, do_ref, ada_ref, pn_mix_ref, pre_mlp_ref, pn_mlp_ref,
                    wo_ref, wu_ref, wd_ref, o_ref, u_ref, *, ff_chunk, row_parts):
    tm = x_ref.shape[1]
    parts = [slice(p * tm // row_parts, (p + 1) * tm // row_parts) for p in range(row_parts)]
    gt_a, sh_m, sc_m, gt_m = _ada_rows(ada_ref, 2, 4)
    ys = [_dot(go_ref[0, r, :], wo_ref[:GLA_WIDTH, :]) + _dot(do_ref[0, r, :], wo_ref[GLA_WIDTH:, :])
          for r in parts]
    x1s = [x_ref[0, r, :] + gt_a * (_rms(y) * pn_mix_ref[...]) for r, y in zip(parts, ys)]
    hs = [(_rms(x1) * pre_mlp_ref[...] * (1.0 + sc_m) + sh_m).astype(BF16) for x1 in x1s]
    d_ff = wu_ref.shape[1]
    for f in range(d_ff // ff_chunk):
        cols = slice(f * ff_chunk, (f + 1) * ff_chunk)
        for r, h in zip(parts, hs):
            u = jnp.maximum(_dot(h, wu_ref[:, cols]), 0.0)
            u_ref[r, cols] = (u * u).astype(BF16)
    y2s = [_dot(u_ref[r, :], wd_ref[...]) for r in parts]
    for r, x1, y2 in zip(parts, x1s, y2s):
        o_ref[0, r, :] = x1 + gt_m * (_rms(y2) * pn_mlp_ref[...])


def _out_mlp(x, go, do, ada, post_mix, pre_mlp, post_mlp, w_out, w_up, w_down, tm, ff_chunk):
    bsz, seq, d = x.shape
    d_ff = w_up.shape[1]
    row = lambda b, i: (b, i, 0)
    const2 = lambda b, i: (0, 0)
    resident = functools.partial(pl.BlockSpec, index_map=const2, pipeline_mode=pl.Buffered(1))
    return pl.pallas_call(
        functools.partial(_out_mlp_kernel, ff_chunk=ff_chunk, row_parts=tm // 256),
        grid=(bsz, seq // tm),
        in_specs=[
            pl.BlockSpec((1, tm, d), row),
            pl.BlockSpec((1, tm, GLA_WIDTH), row),
            pl.BlockSpec((1, tm, DIFF_WIDTH), row),
            pl.BlockSpec((bsz, N_ADA * d), const2),
            pl.BlockSpec((1, d), const2),
            pl.BlockSpec((1, d), const2),
            pl.BlockSpec((1, d), const2),
            resident((d, d)),
            resident((d, d_ff)),
            resident((d_ff, d)),
        ],
        out_specs=pl.BlockSpec((1, tm, d), row),
        out_shape=jax.ShapeDtypeStruct((bsz, seq, d), F32),
        scratch_shapes=[pltpu.VMEM((tm, d_ff), BF16)],
        compiler_params=pltpu.CompilerParams(
            dimension_semantics=("parallel", "parallel"), vmem_limit_bytes=VMEM_LIMIT),
        name="out_mlp",
    )(x, go, do, ada, post_mix, pre_mlp, post_mlp, w_out, w_up, w_down)


def _pack_w_in(w_in):
    sizes = (GLA_QK, GLA_QK, GLA_WIDTH, GLA_GATE_RANK, GLA_WIDTH, DIFF_QK, DIFF_QK, DIFF_WIDTH)
    offsets = [sum(sizes[:n]) for n in range(1, len(sizes))]
    gq, gk, gv, lr, og, dq, dk, dv = jnp.split(w_in, offsets, axis=-1)
    lr = jnp.pad(lr, ((0, 0), (0, GATE_PAD - GLA_GATE_RANK)))
    return jnp.concatenate([gq, gk, gv, og, dq, dk, dv, lr], axis=-1).astype(BF16)


def kernel(x, c, positions, ada_w, ada_b, pre_norm_mix, post_norm_mix, w_in, gla_gate_w, gla_gate_b, gla_norm, lambda_q1, lambda_k1, lambda_q2, lambda_k2, diff_norm, w_out, pre_norm_mlp, post_norm_mlp, w_up, w_down):
    depth = ada_w.shape[0]
    bsz, seq, d = x.shape
    vec = lambda t: t.reshape(1, -1)
    for l in range(depth):
        lambda_init = 0.8 - 0.6 * math.exp(-0.3 * l)
        ada, cos, sin = _ada_rope(c, ada_w[l], ada_b[l], positions)
        gate_w_pad = jnp.pad(gla_gate_w[l], ((0, GATE_PAD - GLA_GATE_RANK), (0, 0))).astype(BF16)
        gq, gk, gv, gog, glog, dq, dk, dvt = _inproj(
            x, ada, vec(pre_norm_mix[l]), cos, sin, _pack_w_in(w_in[l]), gate_w_pad,
            vec(gla_gate_b[l]), tm=512)
        go = _gla(gq, gk, gv, glog, gog, vec(gla_norm[l]), ts=512)
        do = _diff_attn(dq, dk, dvt, vec(lambda_q1[l]), vec(lambda_k1[l]), vec(lambda_q2[l]),
                        vec(lambda_k2[l]), vec(diff_norm[l]), lambda_init, tq=256, heads=4)
        x = _out_mlp(x, go, do, ada, vec(post_norm_mix[l]), vec(pre_norm_mlp[l]),
                     vec(post_norm_mlp[l]), w_out[l].astype(BF16), w_up[l].astype(BF16),
                     w_down[l].astype(BF16), tm=512, ff_chunk=1024)
    return x
```

```python
import functools
import math

import jax
import jax.numpy as jnp
from jax import lax
from jax.experimental import pallas as pl
from jax.experimental.pallas import tpu as pltpu

F32 = jnp.float32
BF16 = jnp.bfloat16

GLA_HEADS = 4
GLA_DK = 64
GLA_DV = 128
GLA_QK = GLA_HEADS * GLA_DK
GLA_WIDTH = GLA_HEADS * GLA_DV
GLA_GATE_RANK = 16
GLA_GATE_NORM = 16.0
GLA_CHUNK = 64
DIFF_HEADS = 4
DIFF_DQK = 64
DIFF_DV = 128
DIFF_QK = DIFF_HEADS * 2 * DIFF_DQK
DIFF_WIDTH = DIFF_HEADS * DIFF_DV
ROPE_THETA = 10000.0
EPS = 1e-6
N_ADA = 6

LANES = 128
GATE_PAD = LANES
ROPE_HALF = DIFF_DQK // 2
_SUM_ROWS = 16
LOG2E = math.log2(math.e)
VMEM_LIMIT = 56 * 1024 * 1024

_COL_GQ = 0
_COL_GK = _COL_GQ + GLA_QK
_COL_GV = _COL_GK + GLA_QK
_COL_GOG = _COL_GV + GLA_WIDTH
_COL_DQ = _COL_GOG + GLA_WIDTH
_COL_DK = _COL_DQ + DIFF_QK
_COL_DV = _COL_DK + DIFF_QK
_COL_LR = _COL_DV + DIFF_WIDTH
_COL_END = _COL_LR + GATE_PAD


def _dot(a, b):
    return jnp.dot(a, b, preferred_element_type=F32)


def _dot_nt(a, b):
    return lax.dot_general(a, b, (((1,), (1,)), ((), ())), preferred_element_type=F32)


def _dot_tn(a, b):
    return lax.dot_general(a, b, (((0,), (0,)), ((), ())), preferred_element_type=F32)


def _rms(t):
    return t * lax.rsqrt(jnp.mean(t * t, axis=-1, keepdims=True) + EPS)


def _silu(t):
    return t * (1.0 / (1.0 + jnp.exp(-t)))


def _ada_kernel(c_ref, w_ref, b_ref, pos_ref, freq_ref, o_ref, cos_ref, sin_ref):
    ca = _silu(c_ref[...]).astype(BF16)
    o_ref[...] = _dot(ca, w_ref[...].astype(BF16)) + b_ref[...]
    ang = pos_ref[...] * freq_ref[...]
    cos_ref[...] = jnp.cos(ang)
    sin_ref[...] = jnp.sin(ang)


def _ada_rope(c, ada_w, ada_b, positions, steps=8):
    bsz, d = c.shape
    n = ada_w.shape[1]
    tn = n // steps
    per_row = LANES // ROPE_HALF
    inv_freq = 1.0 / (ROPE_THETA ** (jnp.arange(0, DIFF_DQK, 2, dtype=F32) / DIFF_DQK))
    freq = jnp.tile(inv_freq, per_row).reshape(1, LANES)
    pos = jnp.repeat(positions.astype(F32).reshape(-1, per_row), ROPE_HALF, axis=1)
    rows = pos.shape[0]
    tr = rows // steps
    table = (pl.BlockSpec((tr, LANES), lambda j: (j, 0)), jax.ShapeDtypeStruct((rows, LANES), F32))
    ada, cos, sin = pl.pallas_call(
        _ada_kernel,
        grid=(steps,),
        in_specs=[
            pl.BlockSpec((bsz, d), lambda j: (0, 0)),
            pl.BlockSpec((d, tn), lambda j: (0, j)),
            pl.BlockSpec((1, tn), lambda j: (0, j)),
            table[0],
            pl.BlockSpec((1, LANES), lambda j: (0, 0)),
        ],
        out_specs=[pl.BlockSpec((bsz, tn), lambda j: (0, j)), table[0], table[0]],
        out_shape=[jax.ShapeDtypeStruct((bsz, n), F32), table[1], table[1]],
        compiler_params=pltpu.CompilerParams(
            dimension_semantics=("parallel",), vmem_limit_bytes=VMEM_LIMIT),
        name="ada_ln",
    )(c, ada_w, ada_b.reshape(1, n), pos, freq)
    shape = positions.shape + (ROPE_HALF,)
    return ada, cos.reshape(shape), sin.reshape(shape)


def _ada_rows(ada_ref, first, count):
    b = pl.program_id(0)
    d = ada_ref.shape[1] // N_ADA
    return [ada_ref[pl.ds(b, 1), n * d:(n + 1) * d] for n in range(first, first + count)]


def _inproj_kernel(x_ref, ada_ref, pn_ref, cos_ref, sin_ref, w_ref, gw_ref, gb_ref,
                   gq_ref, gk_ref, gv_ref, gog_ref, glog_ref, dq_ref, dk_ref, dvt_ref):
    x = x_ref[0]
    shift, scale = _ada_rows(ada_ref, 0, 2)
    h = _rms(x) * pn_ref[...] * (1.0 + scale) + shift
    hb = h.astype(BF16)

    def proj(lo, hi):
        return _dot(hb, w_ref[:, lo:hi])

    dvt_ref[0] = proj(_COL_DV, _COL_LR).T.astype(BF16)

    c32, s32 = cos_ref[0], sin_ref[0]
    cs = jnp.concatenate([c32, s32, c32, s32], axis=1)
    lane = lax.broadcasted_iota(jnp.int32, cs.shape, 1)
    first_half = (lane % DIFF_DQK) < ROPE_HALF
    cos = jnp.where(first_half, cs, pltpu.roll(cs, ROPE_HALF, 1))
    sin = jnp.where(first_half, -pltpu.roll(cs, LANES - ROPE_HALF, 1), cs)

    def rope_store(out_ref, lo, scale):
        t = proj(lo, lo + DIFF_QK)
        for c in range(DIFF_QK // LANES):
            tc = t[:, c * LANES:(c + 1) * LANES]
            partner = jnp.where(first_half, pltpu.roll(tc, LANES - ROPE_HALF, 1),
                                pltpu.roll(tc, ROPE_HALF, 1))
            out_ref[0, :, c * LANES:(c + 1) * LANES] = ((tc * cos + partner * sin) * scale).astype(BF16)

    rope_store(dq_ref, _COL_DQ, DIFF_DQK ** -0.5 * LOG2E)
    rope_store(dk_ref, _COL_DK, 1.0)

    lr = proj(_COL_LR, _COL_END).astype(BF16)
    z = _dot(lr, gw_ref[...]) + gb_ref[...]
    log_sig = jnp.minimum(z, 0.0) - jnp.log1p(jnp.exp(-jnp.abs(z)))
    glog_ref[0] = log_sig * (1.0 / GLA_GATE_NORM)

    gq_ref[0] = (proj(_COL_GQ, _COL_GK) * (GLA_DK ** -0.5)).astype(BF16)
    gk_ref[0] = proj(_COL_GK, _COL_GV).astype(BF16)
    gv_ref[0] = proj(_COL_GV, _COL_GOG).astype(BF16)
    gog_ref[0] = proj(_COL_GOG, _COL_DQ).astype(BF16)


def _inproj(x, ada, pre_norm, cos, sin, w_pack, gate_w_pad, gate_b, tm):
    bsz, seq, d = x.shape
    grid = (bsz, seq // tm)
    row = lambda b, i: (b, i, 0)
    const2 = lambda b, i: (0, 0)

    def out(width, dtype):
        return (pl.BlockSpec((1, tm, width), row), jax.ShapeDtypeStruct((bsz, seq, width), dtype))

    dvt = (pl.BlockSpec((1, DIFF_WIDTH, tm), lambda b, i: (b, 0, i)),
           jax.ShapeDtypeStruct((bsz, DIFF_WIDTH, seq), BF16))
    outs = [out(GLA_QK, BF16), out(GLA_QK, BF16), out(GLA_WIDTH, BF16), out(GLA_WIDTH, BF16),
            out(GLA_QK, F32), out(DIFF_QK, BF16), out(DIFF_QK, BF16), dvt]
    return pl.pallas_call(
        _inproj_kernel,
        grid=grid,
        in_specs=[
            pl.BlockSpec((1, tm, d), row),
            pl.BlockSpec((bsz, N_ADA * d), const2),
            pl.BlockSpec((1, d), const2),
            pl.BlockSpec((1, tm, ROPE_HALF), row),
            pl.BlockSpec((1, tm, ROPE_HALF), row),
            pl.BlockSpec((d, _COL_END), const2),
            pl.BlockSpec((GATE_PAD, GLA_QK), const2),
            pl.BlockSpec((1, GLA_QK), const2),
        ],
        out_specs=[o[0] for o in outs],
        out_shape=[o[1] for o in outs],
        compiler_params=pltpu.CompilerParams(
            dimension_semantics=("parallel", "parallel"), vmem_limit_bytes=VMEM_LIMIT),
        name="in_proj",
    )(x, ada, pre_norm, cos, sin, w_pack, gate_w_pad, gate_b)


def _head_stack(t, lane_head):
    return jnp.concatenate(
        [jnp.where(lane_head == h, t, jnp.zeros_like(t)) for h in range(GLA_HEADS)], axis=0)


def _gla_kernel(q_ref, k_ref, v_ref, g_ref, og_ref, gn_ref, o_ref, state_ref, *, chunks):
    C = GLA_CHUNK

    @pl.when(pl.program_id(1) == 0)
    def _():
        state_ref[...] = jnp.zeros_like(state_ref)

    row = lax.broadcasted_iota(jnp.int32, (C, C), 0)
    col = lax.broadcasted_iota(jnp.int32, (C, C), 1)
    cum_mat = (row >= col).astype(BF16)
    srow = lax.broadcasted_iota(jnp.int32, (GLA_HEADS * C, C), 0)
    scol = lax.broadcasted_iota(jnp.int32, (GLA_HEADS * C, C), 1)
    causal = (srow % C) >= scol
    lane_head = lax.broadcasted_iota(jnp.int32, (C, GLA_QK), 1) // GLA_DK
    state_head = lax.broadcasted_iota(jnp.int32, (GLA_DV, GLA_QK), 1) // GLA_DK
    gn = gn_ref[...]

    chunk_rows = [slice(c * C, (c + 1) * C) for c in range(chunks)]

    cum = []
    for rows in chunk_rows:
        g = g_ref[0, rows, :]
        g_hi = g.astype(BF16)
        g_lo = (g - g_hi.astype(F32)).astype(BF16)
        cum.append(_dot(cum_mat, g_hi) + _dot(cum_mat, g_lo))

    q_ins, scores, upds, decays = [], [], [], []
    for rows, b in zip(chunk_rows, cum):
        b_last = b[C - 1:C, :]
        b_mid = b[C // 2 - 1:C // 2, :]
        q = q_ref[0, rows, :].astype(F32)
        k = k_ref[0, rows, :].astype(F32)
        q_ins.append((q * jnp.exp(b)).astype(BF16))
        q_mid = (q * jnp.exp(b - b_mid)).astype(BF16)
        k_mid = (k * jnp.exp(b_mid - b)).astype(BF16)
        k_out = (k * jnp.exp(b_last - b)).astype(BF16)
        s = _dot_nt(_head_stack(q_mid, lane_head), k_mid)
        scores.append(jnp.where(causal, s, 0.0).astype(BF16))
        upd = _dot_tn(v_ref[0, rows, :], k_out)
        own = upd[:GLA_DV, :]
        for h in range(1, GLA_HEADS):
            own = jnp.where(state_head == h, upd[h * GLA_DV:(h + 1) * GLA_DV, :], own)
        upds.append(own)
        decays.append(jnp.exp(b_last))

    state = state_ref[...]
    states = []
    for upd, decay in zip(upds, decays):
        states.append(state.astype(BF16))
        state = state * decay + upd
    state_ref[...] = state

    for rows, q_in, s, st in zip(chunk_rows, q_ins, scores, states):
        inter = _dot_nt(_head_stack(q_in, lane_head), st)
        for h in range(GLA_HEADS):
            hr = slice(h * C, (h + 1) * C)
            hv = slice(h * GLA_DV, (h + 1) * GLA_DV)
            o = inter[hr, :] + _dot(s[hr, :], v_ref[0, rows, hv])
            og = og_ref[0, rows, hv].astype(F32)
            o_ref[0, rows, hv] = (_rms(o) * gn * _silu(og)).astype(o_ref.dtype)


def _gla(gq, gk, gv, glog, gog, gla_norm, ts):
    bsz, seq, _ = gq.shape
    row = lambda b, i: (b, i, 0)
    return pl.pallas_call(
        functools.partial(_gla_kernel, chunks=ts // GLA_CHUNK),
        grid=(bsz, seq // ts),
        in_specs=[
            pl.BlockSpec((1, ts, GLA_QK), row),
            pl.BlockSpec((1, ts, GLA_QK), row),
            pl.BlockSpec((1, ts, GLA_WIDTH), row),
            pl.BlockSpec((1, ts, GLA_QK), row),
            pl.BlockSpec((1, ts, GLA_WIDTH), row),
            pl.BlockSpec((1, GLA_DV), lambda b, i: (0, 0)),
        ],
        out_specs=pl.BlockSpec((1, ts, GLA_WIDTH), row),
        out_shape=jax.ShapeDtypeStruct((bsz, seq, GLA_WIDTH), BF16),
        scratch_shapes=[pltpu.VMEM((GLA_DV, GLA_QK), F32)],
        compiler_params=pltpu.CompilerParams(
            dimension_semantics=("parallel", "arbitrary"), vmem_limit_bytes=VMEM_LIMIT),
        name="gla",
    )(gq, gk, gv, glog, gog, gla_norm)


def _diff_kernel(q_ref, k_ref, vt_ref, lq1_ref, lk1_ref, lq2_ref, lk2_ref, dn_ref, o_ref,
                 acc_ref, s_ref, *, tq, heads, lambda_init):
    qi = pl.program_id(2)
    lane = lax.broadcasted_iota(jnp.int32, (tq, 2 * DIFF_DQK), 1)
    ones_rows = jnp.ones((_SUM_ROWS, tq), BF16)

    def head_cols(h, width):
        return slice(h * width, (h + 1) * width)

    qs = []
    for h in range(heads):
        q = q_ref[0, :, head_cols(h, 2 * DIFF_DQK)]
        zero = jnp.zeros_like(q)
        qs.append(jnp.concatenate([jnp.where(lane < DIFF_DQK, q, zero),
                                   jnp.where(lane >= DIFF_DQK, q, zero)], axis=0))

    def score(slot, j):
        for h in range(heads):
            kb = k_ref[0, j * tq:(j + 1) * tq, head_cols(h, 2 * DIFF_DQK)]
            s_ref[slot, h] = _dot_nt(kb, qs[h])

    def consume(slot, j, ms, masked):
        out = []
        for h in range(heads):
            vtb = vt_ref[0, head_cols(h, DIFF_DV), j * tq:(j + 1) * tq]
            m_new, p = [], []
            for c in range(2 * tq // LANES):
                cols = slice(c * LANES, (c + 1) * LANES)
                st = s_ref[slot, h, :, cols]
                if masked:
                    key = lax.broadcasted_iota(jnp.int32, st.shape, 0)
                    qry = lax.broadcasted_iota(jnp.int32, st.shape, 1) + (c * LANES) % tq
                    st = jnp.where(key <= qry, st, -jnp.inf)
                mc = jnp.max(st, axis=0, keepdims=True)
                if ms is not None:
                    mc = jnp.maximum(ms[h][:, cols], mc)
                p.append(jnp.exp2(st - mc).astype(BF16))
                m_new.append(mc)
            m_new = jnp.concatenate(m_new, axis=1)
            v_aug = jnp.concatenate([vtb, ones_rows], axis=0)
            pv = _dot(v_aug, jnp.concatenate(p, axis=1))
            if ms is None:
                acc_ref[h] = pv
            else:
                acc_ref[h] = jnp.exp2(ms[h] - m_new) * acc_ref[h] + pv
            out.append(m_new)
        return tuple(out)

    def finish():
        lam = (jnp.exp(jnp.sum(lq1_ref[...] * lk1_ref[...], axis=-1, keepdims=True))
               - jnp.exp(jnp.sum(lq2_ref[...] * lk2_ref[...], axis=-1, keepdims=True))
               + lambda_init)
        for h in range(heads):
            acc = acc_ref[h]
            o_all = acc[:DIFF_DV] * (1.0 / acc[DIFF_DV:DIFF_DV + 1])
            ot = o_all[:, :tq] - lam * o_all[:, tq:]
            ot = ot * lax.rsqrt(jnp.mean(ot * ot, axis=0, keepdims=True) + EPS)
            o_ref[0, :, head_cols(h, DIFF_DV)] = (
                ot.T * dn_ref[...] * (1.0 - lambda_init)).astype(o_ref.dtype)

    def run(n):
        score(0, 0)
        ms = None
        for j in range(n + 1):
            if j < n:
                score((j + 1) % 2, j + 1)
            ms = consume(j % 2, j, ms, j == n)
        finish()

    for n in range(k_ref.shape[1] // tq):
        pl.when(qi == n)(functools.partial(run, n))


def _diff_attn(dq, dk, dvt, lq1, lk1, lq2, lk2, diff_norm, lambda_init, tq, heads):
    bsz, seq, _ = dq.shape
    vec = lambda n: pl.BlockSpec((1, n), lambda b, g, i: (0, 0))
    return pl.pallas_call(
        functools.partial(_diff_kernel, tq=tq, heads=heads, lambda_init=lambda_init),
        grid=(bsz, DIFF_HEADS // heads, seq // tq),
        in_specs=[
            pl.BlockSpec((1, tq, heads * 2 * DIFF_DQK), lambda b, g, i: (b, i, g)),
            pl.BlockSpec((1, seq, heads * 2 * DIFF_DQK), lambda b, g, i: (b, 0, g)),
            pl.BlockSpec((1, heads * DIFF_DV, seq), lambda b, g, i: (b, g, 0)),
            vec(DIFF_DQK), vec(DIFF_DQK), vec(DIFF_DQK), vec(DIFF_DQK), vec(DIFF_DV),
        ],
        out_specs=pl.BlockSpec((1, tq, heads * DIFF_DV), lambda b, g, i: (b, i, g)),
        out_shape=jax.ShapeDtypeStruct((bsz, seq, DIFF_WIDTH), BF16),
        scratch_shapes=[pltpu.VMEM((heads, DIFF_DV + _SUM_ROWS, 2 * tq), F32),
                        pltpu.VMEM((2, heads, tq, 2 * tq), F32)],
        compiler_params=pltpu.CompilerParams(
            dimension_semantics=("parallel", "parallel", "parallel"),
            vmem_limit_bytes=VMEM_LIMIT),
        name="diff_attn",
    )(dq, dk, dvt, lq1, lk1, lq2, lk2, diff_norm)


def _out_mlp_kernel(x_ref, go_ref, do_ref, ada_ref, pn_mix_ref, pre_mlp_ref, pn_mlp_ref,
                    wo_ref, wu_ref, wd_ref, o_ref, u_ref, *, ff_chunk, row_parts):
    tm = x_ref.shape[1]
    parts = [slice(p * tm // row_parts, (p + 1) * tm // row_parts) for p in range(row_parts)]
    gt_a, sh_m, sc_m, gt_m = _ada_rows(ada_ref, 2, 4)
    ys = [_dot(go_ref[0, r, :], wo_ref[:GLA_WIDTH, :]) + _dot(do_ref[0, r, :], wo_ref[GLA_WIDTH:, :])
          for r in parts]
    x1s = [x_ref[0, r, :] + gt_a * (_rms(y) * pn_mix_ref[...]) for r, y in zip(parts, ys)]
    hs = [(_rms(x1) * pre_mlp_ref[...] * (1.0 + sc_m) + sh_m).astype(BF16) for x1 in x1s]
    d_ff = wu_ref.shape[1]
    for f in range(d_ff // ff_chunk):
        cols = slice(f * ff_chunk, (f + 1) * ff_chunk)
        for r, h in zip(parts, hs):
            u = jnp.maximum(_dot(h, wu_ref[:, cols]), 0.0)
            u_ref[r, cols] = (u * u).astype(BF16)
    y2s = [_dot(u_ref[r, :], wd_ref[...]) for r in parts]
    for r, x1, y2 in zip(parts, x1s, y2s):
        o_ref[0, r, :] = x1 + gt_m * (_rms(y2) * pn_mlp_ref[...])


def _out_mlp(x, go, do, ada, post_mix, pre_mlp, post_mlp, w_out, w_up, w_down, tm, ff_chunk):
    bsz, seq, d = x.shape
    d_ff = w_up.shape[1]
    row = lambda b, i: (b, i, 0)
    const2 = lambda b, i: (0, 0)
    resident = functools.partial(pl.BlockSpec, index_map=const2, pipeline_mode=pl.Buffered(1))
    return pl.pallas_call(
        functools.partial(_out_mlp_kernel, ff_chunk=ff_chunk, row_parts=tm // 256),
        grid=(bsz, seq // tm),
        in_specs=[
            pl.BlockSpec((1, tm, d), row),
            pl.BlockSpec((1, tm, GLA_WIDTH), row),
            pl.BlockSpec((1, tm, DIFF_WIDTH), row),
            pl.BlockSpec((bsz, N_ADA * d), const2),
            pl.BlockSpec((1, d), const2),
            pl.BlockSpec((1, d), const2),
            pl.BlockSpec((1, d), const2),
            resident((d, d)),
            resident((d, d_ff)),
            resident((d_ff, d)),
        ],
        out_specs=pl.BlockSpec((1, tm, d), row),
        out_shape=jax.ShapeDtypeStruct((bsz, seq, d), F32),
        scratch_shapes=[pltpu.VMEM((tm, d_ff), BF16)],
        compiler_params=pltpu.CompilerParams(
            dimension_semantics=("parallel", "parallel"), vmem_limit_bytes=VMEM_LIMIT),
        name="out_mlp",
    )(x, go, do, ada, post_mix, pre_mlp, post_mlp, w_out, w_up, w_down)


def _pack_w_in(w_in):
    sizes = (GLA_QK, GLA_QK, GLA_WIDTH, GLA_GATE_RANK, GLA_WIDTH, DIFF_QK, DIFF_QK, DIFF_WIDTH)
    offsets = [sum(sizes[:n]) for n in range(1, len(sizes))]
    gq, gk, gv, lr, og, dq, dk, dv = jnp.split(w_in, offsets, axis=-1)
    lr = jnp.pad(lr, ((0, 0), (0, GATE_PAD - GLA_GATE_RANK)))
    return jnp.concatenate([gq, gk, gv, og, dq, dk, dv, lr], axis=-1).astype(BF16)


def kernel(x, c, positions, ada_w, ada_b, pre_norm_mix, post_norm_mix, w_in, gla_gate_w, gla_gate_b, gla_norm, lambda_q1, lambda_k1, lambda_q2, lambda_k2, diff_norm, w_out, pre_norm_mlp, post_norm_mlp, w_up, w_down):
    depth = ada_w.shape[0]
    bsz, seq, d = x.shape
    vec = lambda t: t.reshape(1, -1)
    for l in range(depth):
        lambda_init = 0.8 - 0.6 * math.exp(-0.3 * l)
        ada, cos, sin = _ada_rope(c, ada_w[l], ada_b[l], positions)
        gate_w_pad = jnp.pad(gla_gate_w[l], ((0, GATE_PAD - GLA_GATE_RANK), (0, 0))).astype(BF16)
        gq, gk, gv, gog, glog, dq, dk, dvt = _inproj(
            x, ada, vec(pre_norm_mix[l]), cos, sin, _pack_w_in(w_in[l]), gate_w_pad,
            vec(gla_gate_b[l]), tm=512)
        go = _gla(gq, gk, gv, glog, gog, vec(gla_norm[l]), ts=512)
        do = _diff_attn(dq, dk, dvt, vec(lambda_q1[l]), vec(lambda_k1[l]), vec(lambda_q2[l]),
                        vec(lambda_k2[l]), vec(diff_norm[l]), lambda_init, tq=256, heads=4)
        x = _out_mlp(x, go, do, ada, vec(post_norm_mix[l]), vec(pre_norm_mlp[l]),
                     vec(post_norm_mlp[l]), w_out[l].astype(BF16), w_up[l].astype(BF16),
                     w_down[l].astype(BF16), tm=512, ff_chunk=1024)
    return x
```

```python
import functools
import math

import jax
import jax.numpy as jnp
from jax import lax
from jax.experimental import pallas as pl
from jax.experimental.pallas import tpu as pltpu

F32 = jnp.float32
BF16 = jnp.bfloat16

GLA_HEADS = 4
GLA_DK = 64
GLA_DV = 128
GLA_QK = GLA_HEADS * GLA_DK
GLA_WIDTH = GLA_HEADS * GLA_DV
GLA_GATE_RANK = 16
GLA_GATE_NORM = 16.0
GLA_CHUNK = 64
DIFF_HEADS = 4
DIFF_DQK = 64
DIFF_DV = 128
DIFF_QK = DIFF_HEADS * 2 * DIFF_DQK
DIFF_WIDTH = DIFF_HEADS * DIFF_DV
ROPE_THETA = 10000.0
EPS = 1e-6
N_ADA = 6

LANES = 128
GATE_PAD = LANES
ROPE_HALF = DIFF_DQK // 2
_SUM_ROWS = 16
LOG2E = math.log2(math.e)
VMEM_LIMIT = 56 * 1024 * 1024

_COL_GQ = 0
_COL_GK = _COL_GQ + GLA_QK
_COL_GV = _COL_GK + GLA_QK
_COL_GOG = _COL_GV + GLA_WIDTH
_COL_DQ = _COL_GOG + GLA_WIDTH
_COL_DK = _COL_DQ + DIFF_QK
_COL_DV = _COL_DK + DIFF_QK
_COL_LR = _COL_DV + DIFF_WIDTH
_COL_END = _COL_LR + GATE_PAD


def _dot(a, b):
    return jnp.dot(a, b, preferred_element_type=F32)


def _dot_nt(a, b):
    return lax.dot_general(a, b, (((1,), (1,)), ((), ())), preferred_element_type=F32)


def _dot_tn(a, b):
    return lax.dot_general(a, b, (((0,), (0,)), ((), ())), preferred_element_type=F32)


def _rms(t):
    return t * lax.rsqrt(jnp.mean(t * t, axis=-1, keepdims=True) + EPS)


def _silu(t):
    return t * (1.0 / (1.0 + jnp.exp(-t)))


def _ada_kernel(c_ref, w_ref, b_ref, pos_ref, freq_ref, win_ref, o_ref, cos_ref, sin_ref, wp_ref):
    ca = _silu(c_ref[...]).astype(BF16)
    o_ref[...] = _dot(ca, w_ref[...].astype(BF16)) + b_ref[...]
    ang = pos_ref[...] * freq_ref[...]
    cos_ref[...] = jnp.cos(ang)
    sin_ref[...] = jnp.sin(ang)
    w = win_ref[...]
    lr_lo = _COL_GV + GLA_WIDTH
    lr_hi = lr_lo + GLA_GATE_RANK
    wp_ref[:, :lr_lo] = w[:, :lr_lo].astype(BF16)
    wp_ref[:, lr_lo:_COL_LR] = w[:, lr_hi:].astype(BF16)
    lr = jnp.concatenate([w[:, lr_lo:lr_hi], jnp.zeros((w.shape[0], GATE_PAD - GLA_GATE_RANK), F32)], axis=1)
    wp_ref[:, _COL_LR:] = lr.astype(BF16)


def _ada_rope(c, ada_w, ada_b, positions, w_in, steps=8):
    bsz, d = c.shape
    n = ada_w.shape[1]
    tn = n // steps
    per_row = LANES // ROPE_HALF
    inv_freq = 1.0 / (ROPE_THETA ** (jnp.arange(0, DIFF_DQK, 2, dtype=F32) / DIFF_DQK))
    freq = jnp.tile(inv_freq, per_row).reshape(1, LANES)
    pos = jnp.repeat(positions.astype(F32).reshape(-1, per_row), ROPE_HALF, axis=1)
    rows = pos.shape[0]
    tr = rows // steps
    table = (pl.BlockSpec((tr, LANES), lambda j: (j, 0)), jax.ShapeDtypeStruct((rows, LANES), F32))
    w_rows, w_cols = w_in.shape
    assert w_cols == _COL_LR + GLA_GATE_RANK and w_rows % (steps * 16) == 0, w_in.shape
    slab = lambda j: (j, 0)
    ada, cos, sin, w_pack = pl.pallas_call(
        _ada_kernel,
        grid=(steps,),
        in_specs=[
            pl.BlockSpec((bsz, d), lambda j: (0, 0)),
            pl.BlockSpec((d, tn), lambda j: (0, j)),
            pl.BlockSpec((1, tn), lambda j: (0, j)),
            table[0],
            pl.BlockSpec((1, LANES), lambda j: (0, 0)),
            pl.BlockSpec((w_rows // steps, w_cols), slab),
        ],
        out_specs=[pl.BlockSpec((bsz, tn), lambda j: (0, j)), table[0], table[0],
                   pl.BlockSpec((w_rows // steps, _COL_END), slab)],
        out_shape=[jax.ShapeDtypeStruct((bsz, n), F32), table[1], table[1],
                   jax.ShapeDtypeStruct((w_rows, _COL_END), BF16)],
        compiler_params=pltpu.CompilerParams(
            dimension_semantics=("parallel",), vmem_limit_bytes=VMEM_LIMIT),
        name="ada_ln",
    )(c, ada_w, ada_b.reshape(1, n), pos, freq, w_in)
    shape = positions.shape + (ROPE_HALF,)
    return ada, cos.reshape(shape), sin.reshape(shape), w_pack


def _ada_rows(ada_ref, first, count):
    b = pl.program_id(0)
    d = ada_ref.shape[1] // N_ADA
    return [ada_ref[pl.ds(b, 1), n * d:(n + 1) * d] for n in range(first, first + count)]


def _inproj_kernel(x_ref, ada_ref, pn_ref, cos_ref, sin_ref, w_ref, gw_ref, gb_ref, *rest, n_cast):
    cast_src, rest = rest[:n_cast], rest[n_cast:]
    gq_ref, gk_ref, gv_ref, gog_ref, glog_ref, dq_ref, dk_ref, dvt_ref = rest[:len(rest) - n_cast]
    for src, dst in zip(cast_src, rest[len(rest) - n_cast:]):
        dst[...] = src[...].astype(dst.dtype)

    x = x_ref[0]
    shift, scale = _ada_rows(ada_ref, 0, 2)
    h = _rms(x) * pn_ref[...] * (1.0 + scale) + shift
    hb = h.astype(BF16)

    def proj(lo, hi):
        return _dot(hb, w_ref[:, lo:hi])

    dvt_ref[0] = proj(_COL_DV, _COL_LR).T.astype(BF16)

    c32, s32 = cos_ref[0], sin_ref[0]
    cs = jnp.concatenate([c32, s32, c32, s32], axis=1)
    lane = lax.broadcasted_iota(jnp.int32, cs.shape, 1)
    first_half = (lane % DIFF_DQK) < ROPE_HALF
    cos = jnp.where(first_half, cs, pltpu.roll(cs, ROPE_HALF, 1))
    sin = jnp.where(first_half, -pltpu.roll(cs, LANES - ROPE_HALF, 1), cs)

    def rope_store(out_ref, lo, scale):
        t = proj(lo, lo + DIFF_QK)
        for c in range(DIFF_QK // LANES):
            tc = t[:, c * LANES:(c + 1) * LANES]
            partner = jnp.where(first_half, pltpu.roll(tc, LANES - ROPE_HALF, 1),
                                pltpu.roll(tc, ROPE_HALF, 1))
            out_ref[0, :, c * LANES:(c + 1) * LANES] = ((tc * cos + partner * sin) * scale).astype(BF16)

    rope_store(dq_ref, _COL_DQ, DIFF_DQK ** -0.5 * LOG2E)
    rope_store(dk_ref, _COL_DK, 1.0)

    lr = proj(_COL_LR, _COL_END).astype(BF16)
    z = _dot(lr, gw_ref[...]) + gb_ref[...]
    log_sig = jnp.minimum(z, 0.0) - jnp.log1p(jnp.exp(-jnp.abs(z)))
    glog_ref[0] = log_sig * (1.0 / GLA_GATE_NORM)

    gq_ref[0] = (proj(_COL_GQ, _COL_GK) * (GLA_DK ** -0.5)).astype(BF16)
    gk_ref[0] = proj(_COL_GK, _COL_GV).astype(BF16)
    gv_ref[0] = proj(_COL_GV, _COL_GOG).astype(BF16)
    gog_ref[0] = proj(_COL_GOG, _COL_DQ).astype(BF16)


def _inproj(x, ada, pre_norm, cos, sin, w_pack, gate_w_pad, gate_b, later_weights, tm):
    bsz, seq, d = x.shape
    grid = (bsz, seq // tm)
    row = lambda b, i: (b, i, 0)
    const2 = lambda b, i: (0, 0)

    def out(width, dtype):
        return (pl.BlockSpec((1, tm, width), row), jax.ShapeDtypeStruct((bsz, seq, width), dtype))

    dvt = (pl.BlockSpec((1, DIFF_WIDTH, tm), lambda b, i: (b, 0, i)),
           jax.ShapeDtypeStruct((bsz, DIFF_WIDTH, seq), BF16))
    outs = [out(GLA_QK, BF16), out(GLA_QK, BF16), out(GLA_WIDTH, BF16), out(GLA_WIDTH, BF16),
            out(GLA_QK, F32), out(DIFF_QK, BF16), out(DIFF_QK, BF16), dvt]

    steps = grid[0] * grid[1]
    slab = lambda b, i: (b * grid[1] + i, 0)
    cast_in, cast_out = [], []
    for w in later_weights:
        rows, cols = w.shape
        assert rows % (steps * 16) == 0, (rows, steps)
        cast_in.append(pl.BlockSpec((rows // steps, cols), slab))
        cast_out.append((pl.BlockSpec((rows // steps, cols), slab),
                         jax.ShapeDtypeStruct((rows, cols), BF16)))
    res = pl.pallas_call(
        functools.partial(_inproj_kernel, n_cast=len(later_weights)),
        grid=grid,
        in_specs=[
            pl.BlockSpec((1, tm, d), row),
            pl.BlockSpec((bsz, N_ADA * d), const2),
            pl.BlockSpec((1, d), const2),
            pl.BlockSpec((1, tm, ROPE_HALF), row),
            pl.BlockSpec((1, tm, ROPE_HALF), row),
            pl.BlockSpec((d, _COL_END), const2),
            pl.BlockSpec((GATE_PAD, GLA_QK), const2),
            pl.BlockSpec((1, GLA_QK), const2),
        ] + cast_in,
        out_specs=[o[0] for o in outs + cast_out],
        out_shape=[o[1] for o in outs + cast_out],
        compiler_params=pltpu.CompilerParams(
            dimension_semantics=("parallel", "parallel"), vmem_limit_bytes=VMEM_LIMIT),
        name="in_proj",
    )(x, ada, pre_norm, cos, sin, w_pack, gate_w_pad, gate_b, *later_weights)
    return res[:len(outs)], res[len(outs):]


def _head_stack(t, lane_head):
    return jnp.concatenate(
        [jnp.where(lane_head == h, t, jnp.zeros_like(t)) for h in range(GLA_HEADS)], axis=0)


def _gla_kernel(q_ref, k_ref, v_ref, g_ref, og_ref, gn_ref, o_ref, state_ref, *, chunks):
    C = GLA_CHUNK

    @pl.when(pl.program_id(1) == 0)
    def _():
        state_ref[...] = jnp.zeros_like(state_ref)

    row = lax.broadcasted_iota(jnp.int32, (C, C), 0)
    col = lax.broadcasted_iota(jnp.int32, (C, C), 1)
    cum_mat = (row >= col).astype(BF16)
    srow = lax.broadcasted_iota(jnp.int32, (GLA_HEADS * C, C), 0)
    scol = lax.broadcasted_iota(jnp.int32, (GLA_HEADS * C, C), 1)
    causal = (srow % C) >= scol
    lane_head = lax.broadcasted_iota(jnp.int32, (C, GLA_QK), 1) // GLA_DK
    state_head = lax.broadcasted_iota(jnp.int32, (GLA_DV, GLA_QK), 1) // GLA_DK
    gn = gn_ref[...]

    chunk_rows = [slice(c * C, (c + 1) * C) for c in range(chunks)]

    cum = []
    for rows in chunk_rows:
        g = g_ref[0, rows, :]
        g_hi = g.astype(BF16)
        g_lo = (g - g_hi.astype(F32)).astype(BF16)
        cum.append(_dot(cum_mat, g_hi) + _dot(cum_mat, g_lo))

    q_ins, scores, upds, decays = [], [], [], []
    for rows, b in zip(chunk_rows, cum):
        b_last = b[C - 1:C, :]
        b_mid = b[C // 2 - 1:C // 2, :]
        q = q_ref[0, rows, :].astype(F32)
        k = k_ref[0, rows, :].astype(F32)
        q_ins.append((q * jnp.exp(b)).astype(BF16))
        q_mid = (q * jnp.exp(b - b_mid)).astype(BF16)
        k_mid = (k * jnp.exp(b_mid - b)).astype(BF16)
        k_out = (k * jnp.exp(b_last - b)).astype(BF16)
        s = _dot_nt(_head_stack(q_mid, lane_head), k_mid)
        scores.append(jnp.where(causal, s, 0.0).astype(BF16))
        upd = _dot_tn(v_ref[0, rows, :], k_out)
        own = upd[:GLA_DV, :]
        for h in range(1, GLA_HEADS):
            own = jnp.where(state_head == h, upd[h * GLA_DV:(h + 1) * GLA_DV, :], own)
        upds.append(own)
        decays.append(jnp.exp(b_last))

    state = state_ref[...]
    states = []
    for upd, decay in zip(upds, decays):
        states.append(state.astype(BF16))
        state = state * decay + upd
    state_ref[...] = state

    for rows, q_in, s, st in zip(chunk_rows, q_ins, scores, states):
        inter = _dot_nt(_head_stack(q_in, lane_head), st)
        for h in range(GLA_HEADS):
            hr = slice(h * C, (h + 1) * C)
            hv = slice(h * GLA_DV, (h + 1) * GLA_DV)
            o = inter[hr, :] + _dot(s[hr, :], v_ref[0, rows, hv])
            og = og_ref[0, rows, hv].astype(F32)
            o_ref[0, rows, hv] = (_rms(o) * gn * _silu(og)).astype(o_ref.dtype)


def _gla(gq, gk, gv, glog, gog, gla_norm, ts):
    bsz, seq, _ = gq.shape
    row = lambda b, i: (b, i, 0)
    return pl.pallas_call(
        functools.partial(_gla_kernel, chunks=ts // GLA_CHUNK),
        grid=(bsz, seq // ts),
        in_specs=[
            pl.BlockSpec((1, ts, GLA_QK), row),
            pl.BlockSpec((1, ts, GLA_QK), row),
            pl.BlockSpec((1, ts, GLA_WIDTH), row),
            pl.BlockSpec((1, ts, GLA_QK), row),
            pl.BlockSpec((1, ts, GLA_WIDTH), row),
            pl.BlockSpec((1, GLA_DV), lambda b, i: (0, 0)),
        ],
        out_specs=pl.BlockSpec((1, ts, GLA_WIDTH), row),
        out_shape=jax.ShapeDtypeStruct((bsz, seq, GLA_WIDTH), BF16),
        scratch_shapes=[pltpu.VMEM((GLA_DV, GLA_QK), F32)],
        compiler_params=pltpu.CompilerParams(
            dimension_semantics=("parallel", "arbitrary"), vmem_limit_bytes=VMEM_LIMIT),
        name="gla",
    )(gq, gk, gv, glog, gog, gla_norm)


def _diff_kernel(q_ref, k_ref, vt_ref, lq1_ref, lk1_ref, lq2_ref, lk2_ref, dn_ref, o_ref,
                 acc_ref, s_ref, *, tq, heads, lambda_init):
    qi = pl.program_id(2)
    lane = lax.broadcasted_iota(jnp.int32, (tq, 2 * DIFF_DQK), 1)
    ones_rows = jnp.ones((_SUM_ROWS, tq), BF16)

    def head_cols(h, width):
        return slice(h * width, (h + 1) * width)

    qs = []
    for h in range(heads):
        q = q_ref[0, :, head_cols(h, 2 * DIFF_DQK)]
        zero = jnp.zeros_like(q)
        qs.append(jnp.concatenate([jnp.where(lane < DIFF_DQK, q, zero),
                                   jnp.where(lane >= DIFF_DQK, q, zero)], axis=0))

    def score(slot, j):
        for h in range(heads):
            kb = k_ref[0, j * tq:(j + 1) * tq, head_cols(h, 2 * DIFF_DQK)]
            s_ref[slot, h] = _dot_nt(kb, qs[h])

    def consume(slot, j, ms, masked):
        out = []
        for h in range(heads):
            vtb = vt_ref[0, head_cols(h, DIFF_DV), j * tq:(j + 1) * tq]
            m_new, p = [], []
            for c in range(2 * tq // LANES):
                cols = slice(c * LANES, (c + 1) * LANES)
                st = s_ref[slot, h, :, cols]
                if masked:
                    key = lax.broadcasted_iota(jnp.int32, st.shape, 0)
                    qry = lax.broadcasted_iota(jnp.int32, st.shape, 1) + (c * LANES) % tq
                    st = jnp.where(key <= qry, st, -jnp.inf)
                mc = jnp.max(st, axis=0, keepdims=True)
                if ms is not None:
                    mc = jnp.maximum(ms[h][:, cols], mc)
                p.append(jnp.exp2(st - mc).astype(BF16))
                m_new.append(mc)
            m_new = jnp.concatenate(m_new, axis=1)
            v_aug = jnp.concatenate([vtb, ones_rows], axis=0)
            pv = _dot(v_aug, jnp.concatenate(p, axis=1))
            if ms is None:
                acc_ref[h] = pv
            else:
                acc_ref[h] = jnp.exp2(ms[h] - m_new) * acc_ref[h] + pv
            out.append(m_new)
        return tuple(out)

    def finish():
        lam = (jnp.exp(jnp.sum(lq1_ref[...] * lk1_ref[...], axis=-1, keepdims=True))
               - jnp.exp(jnp.sum(lq2_ref[...] * lk2_ref[...], axis=-1, keepdims=True))
               + lambda_init)
        for h in range(heads):
            acc = acc_ref[h]
            o_all = acc[:DIFF_DV] * (1.0 / acc[DIFF_DV:DIFF_DV + 1])
            ot = o_all[:, :tq] - lam * o_all[:, tq:]
            ot = ot * lax.rsqrt(jnp.mean(ot * ot, axis=0, keepdims=True) + EPS)
            o_ref[0, :, head_cols(h, DIFF_DV)] = (
                ot.T * dn_ref[...] * (1.0 - lambda_init)).astype(o_ref.dtype)

    def run(n):
        score(0, 0)
        ms = None
        for j in range(n + 1):
            if j < n:
                score((j + 1) % 2, j + 1)
            ms = consume(j % 2, j, ms, j == n)
        finish()

    for n in range(k_ref.shape[1] // tq):
        pl.when(qi == n)(functools.partial(run, n))


def _diff_attn(dq, dk, dvt, lq1, lk1, lq2, lk2, diff_norm, lambda_init, tq, heads):
    bsz, seq, _ = dq.shape
    vec = lambda n: pl.BlockSpec((1, n), lambda b, g, i: (0, 0))
    return pl.pallas_call(
        functools.partial(_diff_kernel, tq=tq, heads=heads, lambda_init=lambda_init),
        grid=(bsz, DIFF_HEADS // heads, seq // tq),
        in_specs=[
            pl.BlockSpec((1, tq, heads * 2 * DIFF_DQK), lambda b, g, i: (b, i, g)),
            pl.BlockSpec((1, seq, heads * 2 * DIFF_DQK), lambda b, g, i: (b, 0, g)),
            pl.BlockSpec((1, heads * DIFF_DV, seq), lambda b, g, i: (b, g, 0)),
            vec(DIFF_DQK), vec(DIFF_DQK), vec(DIFF_DQK), vec(DIFF_DQK), vec(DIFF_DV),
        ],
        out_specs=pl.BlockSpec((1, tq, heads * DIFF_DV), lambda b, g, i: (b, i, g)),
        out_shape=jax.ShapeDtypeStruct((bsz, seq, DIFF_WIDTH), BF16),
        scratch_shapes=[pltpu.VMEM((heads, DIFF_DV + _SUM_ROWS, 2 * tq), F32),
                        pltpu.VMEM((2, heads, tq, 2 * tq), F32)],
        compiler_params=pltpu.CompilerParams(
            dimension_semantics=("parallel", "parallel", "parallel"),
            vmem_limit_bytes=VMEM_LIMIT),
        name="diff_attn",
    )(dq, dk, dvt, lq1, lk1, lq2, lk2, diff_norm)


def _out_mlp_kernel(x_ref, go_ref, do_ref, ada_ref, pn_mix_ref, pre_mlp_ref, pn_mlp_ref,
                    wo_ref, wu_ref, wd_ref, o_ref, u_ref, *, ff_chunk, row_parts):
    tm = x_ref.shape[1]
    parts = [slice(p * tm // row_parts, (p + 1) * tm // row_parts) for p in range(row_parts)]
    gt_a, sh_m, sc_m, gt_m = _ada_rows(ada_ref, 2, 4)
    ys = [_dot(go_ref[0, r, :], wo_ref[:GLA_WIDTH, :]) + _dot(do_ref[0, r, :], wo_ref[GLA_WIDTH:, :])
          for r in parts]
    x1s = [x_ref[0, r, :] + gt_a * (_rms(y) * pn_mix_ref[...]) for r, y in zip(parts, ys)]
    hs = [(_rms(x1) * pre_mlp_ref[...] * (1.0 + sc_m) + sh_m).astype(BF16) for x1 in x1s]
    d_ff = wu_ref.shape[1]
    for f in range(d_ff // ff_chunk):
        cols = slice(f * ff_chunk, (f + 1) * ff_chunk)
        for r, h in zip(parts, hs):
            u = jnp.maximum(_dot(h, wu_ref[:, cols]), 0.0)
            u_ref[r, cols] = (u * u).astype(BF16)
    y2s = [_dot(u_ref[r, :], wd_ref[...]) for r in parts]
    for r, x1, y2 in zip(parts, x1s, y2s):
        o_ref[0, r, :] = x1 + gt_m * (_rms(y2) * pn_mlp_ref[...])


def _out_mlp(x, go, do, ada, post_mix, pre_mlp, post_mlp, w_out, w_up, w_down, tm, ff_chunk):
    bsz, seq, d = x.shape
    d_ff = w_up.shape[1]
    row = lambda b, i: (b, i, 0)
    const2 = lambda b, i: (0, 0)
    resident = functools.partial(pl.BlockSpec, index_map=const2, pipeline_mode=pl.Buffered(1))
    return pl.pallas_call(
        functools.partial(_out_mlp_kernel, ff_chunk=ff_chunk, row_parts=tm // 256),
        grid=(bsz, seq // tm),
        in_specs=[
            pl.BlockSpec((1, tm, d), row),
            pl.BlockSpec((1, tm, GLA_WIDTH), row),
            pl.BlockSpec((1, tm, DIFF_WIDTH), row),
            pl.BlockSpec((bsz, N_ADA * d), const2),
            pl.BlockSpec((1, d), const2),
            pl.BlockSpec((1, d), const2),
            pl.BlockSpec((1, d), const2),
            resident((d, d)),
            resident((d, d_ff)),
            resident((d_ff, d)),
        ],
        out_specs=pl.BlockSpec((1, tm, d), row),
        out_shape=jax.ShapeDtypeStruct((bsz, seq, d), F32),
        scratch_shapes=[pltpu.VMEM((tm, d_ff), BF16)],
        compiler_params=pltpu.CompilerParams(
            dimension_semantics=("parallel", "parallel"), vmem_limit_bytes=VMEM_LIMIT),
        name="out_mlp",
    )(x, go, do, ada, post_mix, pre_mlp, post_mlp, w_out, w_up, w_down)


def kernel(x, c, positions, ada_w, ada_b, pre_norm_mix, post_norm_mix, w_in, gla_gate_w, gla_gate_b, gla_norm, lambda_q1, lambda_k1, lambda_q2, lambda_k2, diff_norm, w_out, pre_norm_mlp, post_norm_mlp, w_up, w_down):
    depth = ada_w.shape[0]
    bsz, seq, d = x.shape
    vec = lambda t: t.reshape(1, -1)
    for l in range(depth):
        lambda_init = 0.8 - 0.6 * math.exp(-0.3 * l)
        ada, cos, sin, w_pack = _ada_rope(c, ada_w[l], ada_b[l], positions, w_in[l])
        gate_w_pad = jnp.pad(gla_gate_w[l], ((0, GATE_PAD - GLA_GATE_RANK), (0, 0))).astype(BF16)
        (gq, gk, gv, gog, glog, dq, dk, dvt), (w_out_b, w_up_b, w_down_b) = _inproj(
            x, ada, vec(pre_norm_mix[l]), cos, sin, w_pack, gate_w_pad,
            vec(gla_gate_b[l]), (w_out[l], w_up[l], w_down[l]), tm=512)
        go = _gla(gq, gk, gv, glog, gog, vec(gla_norm[l]), ts=512)
        do = _diff_attn(dq, dk, dvt, vec(lambda_q1[l]), vec(lambda_k1[l]), vec(lambda_q2[l]),
                        vec(lambda_k2[l]), vec(diff_norm[l]), lambda_init, tq=256, heads=4)
        x = _out_mlp(x, go, do, ada, vec(post_norm_mix[l]), vec(pre_norm_mlp[l]),
                     vec(post_norm_mlp[l]), w_out_b, w_up_b, w_down_b, tm=512, ff_chunk=1024)
    return x
```

```python
import functools
import math

import jax
import jax.numpy as jnp
from jax import lax
from jax.experimental import pallas as pl
from jax.experimental.pallas import tpu as pltpu

F32 = jnp.float32
BF16 = jnp.bfloat16

GLA_HEADS = 4
GLA_DK = 64
GLA_DV = 128
GLA_QK = GLA_HEADS * GLA_DK
GLA_WIDTH = GLA_HEADS * GLA_DV
GLA_GATE_RANK = 16
GLA_GATE_NORM = 16.0
GLA_CHUNK = 64
DIFF_HEADS = 4
DIFF_DQK = 64
DIFF_DV = 128
DIFF_QK = DIFF_HEADS * 2 * DIFF_DQK
DIFF_WIDTH = DIFF_HEADS * DIFF_DV
ROPE_THETA = 10000.0
EPS = 1e-6
N_ADA = 6

LANES = 128
ROPE_HALF = DIFF_DQK // 2
_SUM_ROWS = 16
LOG2E = math.log2(math.e)
VMEM_LIMIT = 56 * 1024 * 1024

_COL_GQ = 0
_COL_GK = _COL_GQ + GLA_QK
_COL_GV = _COL_GK + GLA_QK
_COL_LR = _COL_GV + GLA_WIDTH
_COL_GOG = _COL_LR + GLA_GATE_RANK
_COL_DQ = _COL_GOG + GLA_WIDTH
_COL_DK = _COL_DQ + DIFF_QK
_COL_DV = _COL_DK + DIFF_QK
_COL_END = _COL_DV + DIFF_WIDTH


def _dot(a, b):
    return jnp.dot(a, b, preferred_element_type=F32)


def _dot_nt(a, b):
    return lax.dot_general(a, b, (((1,), (1,)), ((), ())), preferred_element_type=F32)


def _dot_tn(a, b):
    return lax.dot_general(a, b, (((0,), (0,)), ((), ())), preferred_element_type=F32)


def _rms(t):
    return t * lax.rsqrt(jnp.mean(t * t, axis=-1, keepdims=True) + EPS)


def _silu(t):
    return t * (1.0 / (1.0 + jnp.exp(-t)))


def _ada_kernel(c_ref, w_ref, b_ref, pos_ref, freq_ref, o_ref, cos_ref, sin_ref):
    ca = _silu(c_ref[...]).astype(BF16)
    o_ref[...] = _dot(ca, w_ref[...].astype(BF16)) + b_ref[...]
    ang = pos_ref[...] * freq_ref[...]
    cos_ref[...] = jnp.cos(ang)
    sin_ref[...] = jnp.sin(ang)


def _ada_rope(c, ada_w, ada_b, positions, steps=8):
    bsz, d = c.shape
    n = ada_w.shape[1]
    tn = n // steps
    per_row = LANES // ROPE_HALF
    inv_freq = 1.0 / (ROPE_THETA ** (jnp.arange(0, DIFF_DQK, 2, dtype=F32) / DIFF_DQK))
    freq = jnp.tile(inv_freq, per_row).reshape(1, LANES)
    pos = jnp.repeat(positions.astype(F32).reshape(-1, per_row), ROPE_HALF, axis=1)
    rows = pos.shape[0]
    tr = rows // steps
    table = (pl.BlockSpec((tr, LANES), lambda j: (j, 0)), jax.ShapeDtypeStruct((rows, LANES), F32))
    ada, cos, sin = pl.pallas_call(
        _ada_kernel,
        grid=(steps,),
        in_specs=[
            pl.BlockSpec((bsz, d), lambda j: (0, 0)),
            pl.BlockSpec((d, tn), lambda j: (0, j)),
            pl.BlockSpec((1, tn), lambda j: (0, j)),
            table[0],
            pl.BlockSpec((1, LANES), lambda j: (0, 0)),
        ],
        out_specs=[pl.BlockSpec((bsz, tn), lambda j: (0, j)), table[0], table[0]],
        out_shape=[jax.ShapeDtypeStruct((bsz, n), F32), table[1], table[1]],
        compiler_params=pltpu.CompilerParams(
            dimension_semantics=("parallel",), vmem_limit_bytes=VMEM_LIMIT),
        name="ada_ln",
    )(c, ada_w, ada_b.reshape(1, n), pos, freq)
    shape = positions.shape + (ROPE_HALF,)
    return ada, cos.reshape(shape), sin.reshape(shape)


def _ada_rows(ada_ref, first, count):
    b = pl.program_id(0)
    d = ada_ref.shape[1] // N_ADA
    return [ada_ref[pl.ds(b, 1), n * d:(n + 1) * d] for n in range(first, first + count)]


def _inproj_kernel(x_ref, ada_ref, pn_ref, cos_ref, sin_ref, wt_ref, gw_ref, gb_ref, *rest, n_cast):
    cast_src, rest = rest[:n_cast], rest[n_cast:]
    gq_ref, gk_ref, gv_ref, gog_ref, glog_ref, dq_ref, dk_ref, dvt_ref = rest[:8]
    wb_ref = rest[-1]
    for src, dst in zip(cast_src, rest[8:8 + n_cast]):
        dst[...] = src[...].astype(dst.dtype)

    @pl.when((pl.program_id(0) == 0) & (pl.program_id(1) == 0))
    def _():
        wb_ref[...] = wt_ref[...].astype(BF16)

    x = x_ref[0]
    shift, scale = _ada_rows(ada_ref, 0, 2)
    h = _rms(x) * pn_ref[...] * (1.0 + scale) + shift
    hb = h.astype(BF16)

    def proj(lo, hi):
        return _dot_nt(hb, wb_ref[lo:hi, :])

    dvt_ref[0] = proj(_COL_DV, _COL_END).T.astype(BF16)

    c32, s32 = cos_ref[0], sin_ref[0]
    cs = jnp.concatenate([c32, s32, c32, s32], axis=1)
    lane = lax.broadcasted_iota(jnp.int32, cs.shape, 1)
    first_half = (lane % DIFF_DQK) < ROPE_HALF
    cos = jnp.where(first_half, cs, pltpu.roll(cs, ROPE_HALF, 1))
    sin = jnp.where(first_half, -pltpu.roll(cs, LANES - ROPE_HALF, 1), cs)

    def rope_store(out_ref, lo, scale):
        t = proj(lo, lo + DIFF_QK)
        for c in range(DIFF_QK // LANES):
            tc = t[:, c * LANES:(c + 1) * LANES]
            partner = jnp.where(first_half, pltpu.roll(tc, LANES - ROPE_HALF, 1),
                                pltpu.roll(tc, ROPE_HALF, 1))
            out_ref[0, :, c * LANES:(c + 1) * LANES] = ((tc * cos + partner * sin) * scale).astype(BF16)

    rope_store(dq_ref, _COL_DQ, DIFF_DQK ** -0.5 * LOG2E)
    rope_store(dk_ref, _COL_DK, 1.0)

    lr = proj(_COL_LR, _COL_GOG).astype(BF16)
    z = _dot(lr, gw_ref[...]) + gb_ref[...]
    log_sig = jnp.minimum(z, 0.0) - jnp.log1p(jnp.exp(-jnp.abs(z)))
    glog_ref[0] = log_sig * (1.0 / GLA_GATE_NORM)

    gq_ref[0] = (proj(_COL_GQ, _COL_GK) * (GLA_DK ** -0.5)).astype(BF16)
    gk_ref[0] = proj(_COL_GK, _COL_GV).astype(BF16)
    gv_ref[0] = proj(_COL_GV, _COL_LR).astype(BF16)
    gog_ref[0] = proj(_COL_GOG, _COL_DQ).astype(BF16)


def _inproj(x, ada, pre_norm, cos, sin, w_in_t, gate_w, gate_b, later_weights, tm):
    bsz, seq, d = x.shape
    grid = (bsz, seq // tm)
    row = lambda b, i: (b, i, 0)
    const2 = lambda b, i: (0, 0)

    def out(width, dtype):
        return (pl.BlockSpec((1, tm, width), row), jax.ShapeDtypeStruct((bsz, seq, width), dtype))

    dvt = (pl.BlockSpec((1, DIFF_WIDTH, tm), lambda b, i: (b, 0, i)),
           jax.ShapeDtypeStruct((bsz, DIFF_WIDTH, seq), BF16))
    outs = [out(GLA_QK, BF16), out(GLA_QK, BF16), out(GLA_WIDTH, BF16), out(GLA_WIDTH, BF16),
            out(GLA_QK, F32), out(DIFF_QK, BF16), out(DIFF_QK, BF16), dvt]

    steps = grid[0] * grid[1]
    slab = lambda b, i: (b * grid[1] + i, 0)
    cast_in, cast_out = [], []
    for w in later_weights:
        rows, cols = w.shape
        assert rows % (steps * 16) == 0, (rows, steps)
        cast_in.append(pl.BlockSpec((rows // steps, cols), slab))
        cast_out.append((pl.BlockSpec((rows // steps, cols), slab),
                         jax.ShapeDtypeStruct((rows, cols), BF16)))
    res = pl.pallas_call(
        functools.partial(_inproj_kernel, n_cast=len(later_weights)),
        grid=grid,
        in_specs=[
            pl.BlockSpec((1, tm, d), row),
            pl.BlockSpec((bsz, N_ADA * d), const2),
            pl.BlockSpec((1, d), const2),
            pl.BlockSpec((1, tm, ROPE_HALF), row),
            pl.BlockSpec((1, tm, ROPE_HALF), row),
            pl.BlockSpec((_COL_END, d), const2, pipeline_mode=pl.Buffered(1)),
            pl.BlockSpec((GLA_GATE_RANK, GLA_QK), const2),
            pl.BlockSpec((1, GLA_QK), const2),
        ] + cast_in,
        out_specs=[o[0] for o in outs + cast_out],
        out_shape=[o[1] for o in outs + cast_out],
        scratch_shapes=[pltpu.VMEM((_COL_END, d), BF16)],
        compiler_params=pltpu.CompilerParams(
            dimension_semantics=("arbitrary", "arbitrary"), vmem_limit_bytes=VMEM_LIMIT),
        name="in_proj",
    )(x, ada, pre_norm, cos, sin, w_in_t, gate_w, gate_b, *later_weights)
    return res[:len(outs)], res[len(outs):]


def _head_stack(t, lane_head):
    return jnp.concatenate(
        [jnp.where(lane_head == h, t, jnp.zeros_like(t)) for h in range(GLA_HEADS)], axis=0)


def _gla_kernel(q_ref, k_ref, v_ref, g_ref, og_ref, gn_ref, o_ref, state_ref, *, chunks):
    C = GLA_CHUNK

    @pl.when(pl.program_id(1) == 0)
    def _():
        state_ref[...] = jnp.zeros_like(state_ref)

    row = lax.broadcasted_iota(jnp.int32, (C, C), 0)
    col = lax.broadcasted_iota(jnp.int32, (C, C), 1)
    cum_mat = (row >= col).astype(BF16)
    srow = lax.broadcasted_iota(jnp.int32, (GLA_HEADS * C, C), 0)
    scol = lax.broadcasted_iota(jnp.int32, (GLA_HEADS * C, C), 1)
    causal = (srow % C) >= scol
    lane_head = lax.broadcasted_iota(jnp.int32, (C, GLA_QK), 1) // GLA_DK
    state_head = lax.broadcasted_iota(jnp.int32, (GLA_DV, GLA_QK), 1) // GLA_DK
    gn = gn_ref[...]

    chunk_rows = [slice(c * C, (c + 1) * C) for c in range(chunks)]

    cum = []
    for rows in chunk_rows:
        g = g_ref[0, rows, :]
        g_hi = g.astype(BF16)
        g_lo = (g - g_hi.astype(F32)).astype(BF16)
        cum.append(_dot(cum_mat, g_hi) + _dot(cum_mat, g_lo))

    q_ins, scores, upds, decays = [], [], [], []
    for rows, b in zip(chunk_rows, cum):
        b_last = b[C - 1:C, :]
        b_mid = b[C // 2 - 1:C // 2, :]
        q = q_ref[0, rows, :].astype(F32)
        k = k_ref[0, rows, :].astype(F32)
        q_ins.append((q * jnp.exp(b)).astype(BF16))
        q_mid = (q * jnp.exp(b - b_mid)).astype(BF16)
        k_mid = (k * jnp.exp(b_mid - b)).astype(BF16)
        k_out = (k * jnp.exp(b_last - b)).astype(BF16)
        s = _dot_nt(_head_stack(q_mid, lane_head), k_mid)
        scores.append(jnp.where(causal, s, 0.0).astype(BF16))
        upd = _dot_tn(v_ref[0, rows, :], k_out)
        own = upd[:GLA_DV, :]
        for h in range(1, GLA_HEADS):
            own = jnp.where(state_head == h, upd[h * GLA_DV:(h + 1) * GLA_DV, :], own)
        upds.append(own)
        decays.append(jnp.exp(b_last))

    state = state_ref[...]
    states = []
    for upd, decay in zip(upds, decays):
        states.append(state.astype(BF16))
        state = state * decay + upd
    state_ref[...] = state

    for rows, q_in, s, st in zip(chunk_rows, q_ins, scores, states):
        inter = _dot_nt(_head_stack(q_in, lane_head), st)
        for h in range(GLA_HEADS):
            hr = slice(h * C, (h + 1) * C)
            hv = slice(h * GLA_DV, (h + 1) * GLA_DV)
            o = inter[hr, :] + _dot(s[hr, :], v_ref[0, rows, hv])
            og = og_ref[0, rows, hv].astype(F32)
            o_ref[0, rows, hv] = (_rms(o) * gn * _silu(og)).astype(o_ref.dtype)


def _gla(gq, gk, gv, glog, gog, gla_norm, ts):
    bsz, seq, _ = gq.shape
    row = lambda b, i: (b, i, 0)
    return pl.pallas_call(
        functools.partial(_gla_kernel, chunks=ts // GLA_CHUNK),
        grid=(bsz, seq // ts),
        in_specs=[
            pl.BlockSpec((1, ts, GLA_QK), row),
            pl.BlockSpec((1, ts, GLA_QK), row),
            pl.BlockSpec((1, ts, GLA_WIDTH), row),
            pl.BlockSpec((1, ts, GLA_QK), row),
            pl.BlockSpec((1, ts, GLA_WIDTH), row),
            pl.BlockSpec((1, GLA_DV), lambda b, i: (0, 0)),
        ],
        out_specs=pl.BlockSpec((1, ts, GLA_WIDTH), row),
        out_shape=jax.ShapeDtypeStruct((bsz, seq, GLA_WIDTH), BF16),
        scratch_shapes=[pltpu.VMEM((GLA_DV, GLA_QK), F32)],
        compiler_params=pltpu.CompilerParams(
            dimension_semantics=("parallel", "arbitrary"), vmem_limit_bytes=VMEM_LIMIT),
        name="gla",
    )(gq, gk, gv, glog, gog, gla_norm)


def _diff_kernel(q_ref, k_ref, vt_ref, lq1_ref, lk1_ref, lq2_ref, lk2_ref, dn_ref, o_ref,
                 acc_ref, s_ref, *, tq, heads, lambda_init):
    qi = pl.program_id(2)
    lane = lax.broadcasted_iota(jnp.int32, (tq, 2 * DIFF_DQK), 1)
    ones_rows = jnp.ones((_SUM_ROWS, tq), BF16)

    def head_cols(h, width):
        return slice(h * width, (h + 1) * width)

    qs = []
    for h in range(heads):
        q = q_ref[0, :, head_cols(h, 2 * DIFF_DQK)]
        zero = jnp.zeros_like(q)
        qs.append(jnp.concatenate([jnp.where(lane < DIFF_DQK, q, zero),
                                   jnp.where(lane >= DIFF_DQK, q, zero)], axis=0))

    def score(slot, j):
        for h in range(heads):
            kb = k_ref[0, j * tq:(j + 1) * tq, head_cols(h, 2 * DIFF_DQK)]
            s_ref[slot, h] = _dot_nt(kb, qs[h])

    def consume(slot, j, ms, masked):
        out = []
        for h in range(heads):
            vtb = vt_ref[0, head_cols(h, DIFF_DV), j * tq:(j + 1) * tq]
            m_new, p = [], []
            for c in range(2 * tq // LANES):
                cols = slice(c * LANES, (c + 1) * LANES)
                st = s_ref[slot, h, :, cols]
                if masked:
                    key = lax.broadcasted_iota(jnp.int32, st.shape, 0)
                    qry = lax.broadcasted_iota(jnp.int32, st.shape, 1) + (c * LANES) % tq
                    st = jnp.where(key <= qry, st, -jnp.inf)
                mc = jnp.max(st, axis=0, keepdims=True)
                if ms is not None:
                    mc = jnp.maximum(ms[h][:, cols], mc)
                p.append(jnp.exp2(st - mc).astype(BF16))
                m_new.append(mc)
            m_new = jnp.concatenate(m_new, axis=1)
            v_aug = jnp.concatenate([vtb, ones_rows], axis=0)
            pv = _dot(v_aug, jnp.concatenate(p, axis=1))
            if ms is None:
                acc_ref[h] = pv
            else:
                acc_ref[h] = jnp.exp2(ms[h] - m_new) * acc_ref[h] + pv
            out.append(m_new)
        return tuple(out)

    def finish():
        lam = (jnp.exp(jnp.sum(lq1_ref[...] * lk1_ref[...], axis=-1, keepdims=True))
               - jnp.exp(jnp.sum(lq2_ref[...] * lk2_ref[...], axis=-1, keepdims=True))
               + lambda_init)
        for h in range(heads):
            acc = acc_ref[h]
            o_all = acc[:DIFF_DV] * (1.0 / acc[DIFF_DV:DIFF_DV + 1])
            ot = o_all[:, :tq] - lam * o_all[:, tq:]
            ot = ot * lax.rsqrt(jnp.mean(ot * ot, axis=0, keepdims=True) + EPS)
            o_ref[0, :, head_cols(h, DIFF_DV)] = (
                ot.T * dn_ref[...] * (1.0 - lambda_init)).astype(o_ref.dtype)

    def run(n):
        score(0, 0)
        ms = None
        for j in range(n + 1):
            if j < n:
                score((j + 1) % 2, j + 1)
            ms = consume(j % 2, j, ms, j == n)
        finish()

    for n in range(k_ref.shape[1] // tq):
        pl.when(qi == n)(functools.partial(run, n))


def _diff_attn(dq, dk, dvt, lq1, lk1, lq2, lk2, diff_norm, lambda_init, tq, heads):
    bsz, seq, _ = dq.shape
    vec = lambda n: pl.BlockSpec((1, n), lambda b, g, i: (0, 0))
    return pl.pallas_call(
        functools.partial(_diff_kernel, tq=tq, heads=heads, lambda_init=lambda_init),
        grid=(bsz, DIFF_HEADS // heads, seq // tq),
        in_specs=[
            pl.BlockSpec((1, tq, heads * 2 * DIFF_DQK), lambda b, g, i: (b, i, g)),
            pl.BlockSpec((1, seq, heads * 2 * DIFF_DQK), lambda b, g, i: (b, 0, g)),
            pl.BlockSpec((1, heads * DIFF_DV, seq), lambda b, g, i: (b, g, 0)),
            vec(DIFF_DQK), vec(DIFF_DQK), vec(DIFF_DQK), vec(DIFF_DQK), vec(DIFF_DV),
        ],
        out_specs=pl.BlockSpec((1, tq, heads * DIFF_DV), lambda b, g, i: (b, i, g)),
        out_shape=jax.ShapeDtypeStruct((bsz, seq, DIFF_WIDTH), BF16),
        scratch_shapes=[pltpu.VMEM((heads, DIFF_DV + _SUM_ROWS, 2 * tq), F32),
                        pltpu.VMEM((2, heads, tq, 2 * tq), F32)],
        compiler_params=pltpu.CompilerParams(
            dimension_semantics=("parallel", "parallel", "parallel"),
            vmem_limit_bytes=VMEM_LIMIT),
        name="diff_attn",
    )(dq, dk, dvt, lq1, lk1, lq2, lk2, diff_norm)


def _out_mlp_kernel(x_ref, go_ref, do_ref, ada_ref, pn_mix_ref, pre_mlp_ref, pn_mlp_ref,
                    wo_ref, wu_ref, wd_ref, o_ref, u_ref, *, ff_chunk, row_parts):
    tm = x_ref.shape[1]
    parts = [slice(p * tm // row_parts, (p + 1) * tm // row_parts) for p in range(row_parts)]
    gt_a, sh_m, sc_m, gt_m = _ada_rows(ada_ref, 2, 4)
    ys = [_dot(go_ref[0, r, :], wo_ref[:GLA_WIDTH, :]) + _dot(do_ref[0, r, :], wo_ref[GLA_WIDTH:, :])
          for r in parts]
    x1s = [x_ref[0, r, :] + gt_a * (_rms(y) * pn_mix_ref[...]) for r, y in zip(parts, ys)]
    hs = [(_rms(x1) * pre_mlp_ref[...] * (1.0 + sc_m) + sh_m).astype(BF16) for x1 in x1s]
    d_ff = wu_ref.shape[1]
    for f in range(d_ff // ff_chunk):
        cols = slice(f * ff_chunk, (f + 1) * ff_chunk)
        for r, h in zip(parts, hs):
            u = jnp.maximum(_dot(h, wu_ref[:, cols]), 0.0)
            u_ref[r, cols] = (u * u).astype(BF16)
    y2s = [_dot(u_ref[r, :], wd_ref[...]) for r in parts]
    for r, x1, y2 in zip(parts, x1s, y2s):
        o_ref[0, r, :] = x1 + gt_m * (_rms(y2) * pn_mlp_ref[...])


def _out_mlp(x, go, do, ada, post_mix, pre_mlp, post_mlp, w_out, w_up, w_down, tm, ff_chunk):
    bsz, seq, d = x.shape
    d_ff = w_up.shape[1]
    row = lambda b, i: (b, i, 0)
    const2 = lambda b, i: (0, 0)
    resident = functools.partial(pl.BlockSpec, index_map=const2, pipeline_mode=pl.Buffered(1))
    return pl.pallas_call(
        functools.partial(_out_mlp_kernel, ff_chunk=ff_chunk, row_parts=tm // 256),
        grid=(bsz, seq // tm),
        in_specs=[
            pl.BlockSpec((1, tm, d), row),
            pl.BlockSpec((1, tm, GLA_WIDTH), row),
            pl.BlockSpec((1, tm, DIFF_WIDTH), row),
            pl.BlockSpec((bsz, N_ADA * d), const2),
            pl.BlockSpec((1, d), const2),
            pl.BlockSpec((1, d), const2),
            pl.BlockSpec((1, d), const2),
            resident((d, d)),
            resident((d, d_ff)),
            resident((d_ff, d)),
        ],
        out_specs=pl.BlockSpec((1, tm, d), row),
        out_shape=jax.ShapeDtypeStruct((bsz, seq, d), F32),
        scratch_shapes=[pltpu.VMEM((tm, d_ff), BF16)],
        compiler_params=pltpu.CompilerParams(
            dimension_semantics=("parallel", "parallel"), vmem_limit_bytes=VMEM_LIMIT),
        name="out_mlp",
    )(x, go, do, ada, post_mix, pre_mlp, post_mlp, w_out, w_up, w_down)


def kernel(x, c, positions, ada_w, ada_b, pre_norm_mix, post_norm_mix, w_in, gla_gate_w, gla_gate_b, gla_norm, lambda_q1, lambda_k1, lambda_q2, lambda_k2, diff_norm, w_out, pre_norm_mlp, post_norm_mlp, w_up, w_down):
    depth = ada_w.shape[0]
    bsz, seq, d = x.shape
    vec = lambda t: t.reshape(1, -1)
    for l in range(depth):
        lambda_init = 0.8 - 0.6 * math.exp(-0.3 * l)
        ada, cos, sin = _ada_rope(c, ada_w[l], ada_b[l], positions)
        (gq, gk, gv, gog, glog, dq, dk, dvt), (w_out_b, w_up_b, w_down_b) = _inproj(
            x, ada, vec(pre_norm_mix[l]), cos, sin, w_in[l].T, gla_gate_w[l].astype(BF16),
            vec(gla_gate_b[l]), (w_out[l], w_up[l], w_down[l]), tm=512)
        go = _gla(gq, gk, gv, glog, gog, vec(gla_norm[l]), ts=512)
        do = _diff_attn(dq, dk, dvt, vec(lambda_q1[l]), vec(lambda_k1[l]), vec(lambda_q2[l]),
                        vec(lambda_k2[l]), vec(diff_norm[l]), lambda_init, tq=256, heads=4)
        x = _out_mlp(x, go, do, ada, vec(post_norm_mix[l]), vec(pre_norm_mlp[l]),
                     vec(post_norm_mlp[l]), w_out_b, w_up_b, w_down_b, tm=512, ff_chunk=1024)
    return x
```

```python
import functools
import math

import jax
import jax.numpy as jnp
from jax import lax
from jax.experimental import pallas as pl
from jax.experimental.pallas import tpu as pltpu

F32 = jnp.float32
BF16 = jnp.bfloat16

GLA_HEADS = 4
GLA_DK = 64
GLA_DV = 128
GLA_QK = GLA_HEADS * GLA_DK
GLA_WIDTH = GLA_HEADS * GLA_DV
GLA_GATE_RANK = 16
GLA_GATE_NORM = 16.0
GLA_CHUNK = 64
DIFF_HEADS = 4
DIFF_DQK = 64
DIFF_DV = 128
DIFF_QK = DIFF_HEADS * 2 * DIFF_DQK
DIFF_WIDTH = DIFF_HEADS * DIFF_DV
ROPE_THETA = 10000.0
EPS = 1e-6
N_ADA = 6

LANES = 128
ROPE_HALF = DIFF_DQK // 2
_SUM_ROWS = 16
LOG2E = math.log2(math.e)
VMEM_LIMIT = 56 * 1024 * 1024

_COL_GQ = 0
_COL_GK = _COL_GQ + GLA_QK
_COL_GV = _COL_GK + GLA_QK
_COL_LR = _COL_GV + GLA_WIDTH
_COL_GOG = _COL_LR + GLA_GATE_RANK
_COL_DQ = _COL_GOG + GLA_WIDTH
_COL_DK = _COL_DQ + DIFF_QK
_COL_DV = _COL_DK + DIFF_QK
_COL_END = _COL_DV + DIFF_WIDTH


def _dot(a, b):
    return jnp.dot(a, b, preferred_element_type=F32)


def _dot_nt(a, b):
    return lax.dot_general(a, b, (((1,), (1,)), ((), ())), preferred_element_type=F32)


def _dot_tn(a, b):
    return lax.dot_general(a, b, (((0,), (0,)), ((), ())), preferred_element_type=F32)


def _rms(t):
    return t * lax.rsqrt(jnp.mean(t * t, axis=-1, keepdims=True) + EPS)


def _silu(t):
    return t * (1.0 / (1.0 + jnp.exp(-t)))


def _ada_kernel(c_ref, w_ref, b_ref, pos_ref, freq_ref, o_ref, cos_ref, sin_ref):
    ca = _silu(c_ref[...]).astype(BF16)
    o_ref[...] = _dot(ca, w_ref[...].astype(BF16)) + b_ref[...]
    ang = pos_ref[...] * freq_ref[...]
    cos_ref[...] = jnp.cos(ang)
    sin_ref[...] = jnp.sin(ang)


def _ada_rope(c, ada_w, ada_b, positions, steps=8):
    bsz, d = c.shape
    n = ada_w.shape[1]
    tn = n // steps
    per_row = LANES // ROPE_HALF
    inv_freq = 1.0 / (ROPE_THETA ** (jnp.arange(0, DIFF_DQK, 2, dtype=F32) / DIFF_DQK))
    freq = jnp.tile(inv_freq, per_row).reshape(1, LANES)
    pos = jnp.repeat(positions.astype(F32).reshape(-1, per_row), ROPE_HALF, axis=1)
    rows = pos.shape[0]
    tr = rows // steps
    table = (pl.BlockSpec((tr, LANES), lambda j: (j, 0)), jax.ShapeDtypeStruct((rows, LANES), F32))
    ada, cos, sin = pl.pallas_call(
        _ada_kernel,
        grid=(steps,),
        in_specs=[
            pl.BlockSpec((bsz, d), lambda j: (0, 0)),
            pl.BlockSpec((d, tn), lambda j: (0, j)),
            pl.BlockSpec((1, tn), lambda j: (0, j)),
            table[0],
            pl.BlockSpec((1, LANES), lambda j: (0, 0)),
        ],
        out_specs=[pl.BlockSpec((bsz, tn), lambda j: (0, j)), table[0], table[0]],
        out_shape=[jax.ShapeDtypeStruct((bsz, n), F32), table[1], table[1]],
        compiler_params=pltpu.CompilerParams(
            dimension_semantics=("parallel",), vmem_limit_bytes=VMEM_LIMIT),
        name="ada_ln",
    )(c, ada_w, ada_b.reshape(1, n), pos, freq)
    shape = positions.shape + (ROPE_HALF,)
    return ada, cos.reshape(shape), sin.reshape(shape)


def _ada_rows(ada_ref, first, count):
    b = pl.program_id(0)
    d = ada_ref.shape[1] // N_ADA
    return [ada_ref[pl.ds(b, 1), n * d:(n + 1) * d] for n in range(first, first + count)]


def _inproj_kernel(x_ref, ada_ref, pn_ref, cos_ref, sin_ref, wt_ref, gw_ref, gb_ref, *rest, n_cast):
    cast_src, rest = rest[:n_cast], rest[n_cast:]
    gq_ref, gk_ref, gv_ref, gog_ref, glog_ref, dq_ref, dk_ref, dvt_ref = rest[:8]
    wb_ref = rest[-1]
    for src, dst in zip(cast_src, rest[8:8 + n_cast]):
        dst[...] = src[...].astype(dst.dtype)

    @pl.when((pl.program_id(0) == 0) & (pl.program_id(1) == 0))
    def _():
        wb_ref[...] = wt_ref[...].astype(BF16)

    x = x_ref[0]
    shift, scale = _ada_rows(ada_ref, 0, 2)
    h = _rms(x) * pn_ref[...] * (1.0 + scale) + shift
    hb = h.astype(BF16)

    def proj(lo, hi):
        return _dot_nt(hb, wb_ref[lo:hi, :])

    dvt_ref[0] = proj(_COL_DV, _COL_END).T.astype(BF16)

    c32, s32 = cos_ref[0], sin_ref[0]
    cs = jnp.concatenate([c32, s32, c32, s32], axis=1)
    lane = lax.broadcasted_iota(jnp.int32, cs.shape, 1)
    first_half = (lane % DIFF_DQK) < ROPE_HALF
    cos = jnp.where(first_half, cs, pltpu.roll(cs, ROPE_HALF, 1))
    sin = jnp.where(first_half, -pltpu.roll(cs, LANES - ROPE_HALF, 1), cs)

    def rope_store(out_ref, lo, scale):
        t = proj(lo, lo + DIFF_QK)
        for c in range(DIFF_QK // LANES):
            tc = t[:, c * LANES:(c + 1) * LANES]
            partner = jnp.where(first_half, pltpu.roll(tc, LANES - ROPE_HALF, 1),
                                pltpu.roll(tc, ROPE_HALF, 1))
            out_ref[0, :, c * LANES:(c + 1) * LANES] = ((tc * cos + partner * sin) * scale).astype(BF16)

    rope_store(dq_ref, _COL_DQ, DIFF_DQK ** -0.5 * LOG2E)
    rope_store(dk_ref, _COL_DK, 1.0)

    lr = proj(_COL_LR, _COL_GOG).astype(BF16)
    z = _dot(lr, gw_ref[...]) + gb_ref[...]
    log_sig = jnp.minimum(z, 0.0) - jnp.log1p(jnp.exp(-jnp.abs(z)))
    glog_ref[0] = log_sig * (1.0 / GLA_GATE_NORM)

    gq_ref[0] = (proj(_COL_GQ, _COL_GK) * (GLA_DK ** -0.5)).astype(BF16)
    gk_ref[0] = proj(_COL_GK, _COL_GV).astype(BF16)
    gv_ref[0] = proj(_COL_GV, _COL_LR).astype(BF16)
    gog_ref[0] = proj(_COL_GOG, _COL_DQ).astype(BF16)


def _inproj(x, ada, pre_norm, cos, sin, w_in_t, gate_w, gate_b, later_weights, tm):
    bsz, seq, d = x.shape
    grid = (bsz, seq // tm)
    row = lambda b, i: (b, i, 0)
    const2 = lambda b, i: (0, 0)

    def out(width, dtype):
        return (pl.BlockSpec((1, tm, width), row), jax.ShapeDtypeStruct((bsz, seq, width), dtype))

    dvt = (pl.BlockSpec((1, DIFF_WIDTH, tm), lambda b, i: (b, 0, i)),
           jax.ShapeDtypeStruct((bsz, DIFF_WIDTH, seq), BF16))
    outs = [out(GLA_QK, BF16), out(GLA_QK, BF16), out(GLA_WIDTH, BF16), out(GLA_WIDTH, BF16),
            out(GLA_QK, F32), out(DIFF_QK, BF16), out(DIFF_QK, BF16), dvt]

    steps = grid[0] * grid[1]
    slab = lambda b, i: (b * grid[1] + i, 0)
    cast_in, cast_out = [], []
    for w in later_weights:
        rows, cols = w.shape
        assert rows % (steps * 16) == 0, (rows, steps)
        cast_in.append(pl.BlockSpec((rows // steps, cols), slab))
        cast_out.append((pl.BlockSpec((rows // steps, cols), slab),
                         jax.ShapeDtypeStruct((rows, cols), BF16)))
    res = pl.pallas_call(
        functools.partial(_inproj_kernel, n_cast=len(later_weights)),
        grid=grid,
        in_specs=[
            pl.BlockSpec((1, tm, d), row),
            pl.BlockSpec((bsz, N_ADA * d), const2),
            pl.BlockSpec((1, d), const2),
            pl.BlockSpec((1, tm, ROPE_HALF), row),
            pl.BlockSpec((1, tm, ROPE_HALF), row),
            pl.BlockSpec((_COL_END, d), const2, pipeline_mode=pl.Buffered(1)),
            pl.BlockSpec((GLA_GATE_RANK, GLA_QK), const2),
            pl.BlockSpec((1, GLA_QK), const2),
        ] + cast_in,
        out_specs=[o[0] for o in outs + cast_out],
        out_shape=[o[1] for o in outs + cast_out],
        scratch_shapes=[pltpu.VMEM((_COL_END, d), BF16)],
        compiler_params=pltpu.CompilerParams(
            dimension_semantics=("arbitrary", "arbitrary"), vmem_limit_bytes=VMEM_LIMIT),
        name="in_proj",
    )(x, ada, pre_norm, cos, sin, w_in_t, gate_w, gate_b, *later_weights)
    return res[:len(outs)], res[len(outs):]


def _head_stack(t, lane_head):
    return jnp.concatenate(
        [jnp.where(lane_head == h, t, jnp.zeros_like(t)) for h in range(GLA_HEADS)], axis=0)


def _gla_kernel(q_ref, k_ref, v_ref, g_ref, og_ref, gn_ref, o_ref, state_ref, *, chunks):
    C = GLA_CHUNK

    @pl.when(pl.program_id(1) == 0)
    def _():
        state_ref[...] = jnp.zeros_like(state_ref)

    row = lax.broadcasted_iota(jnp.int32, (C, C), 0)
    col = lax.broadcasted_iota(jnp.int32, (C, C), 1)
    cum_mat = (row >= col).astype(BF16)
    srow = lax.broadcasted_iota(jnp.int32, (GLA_HEADS * C, C), 0)
    scol = lax.broadcasted_iota(jnp.int32, (GLA_HEADS * C, C), 1)
    causal = (srow % C) >= scol
    lane_head = lax.broadcasted_iota(jnp.int32, (C, GLA_QK), 1) // GLA_DK
    state_head = lax.broadcasted_iota(jnp.int32, (GLA_DV, GLA_QK), 1) // GLA_DK
    gn = gn_ref[...]

    chunk_rows = [slice(c * C, (c + 1) * C) for c in range(chunks)]

    cum = []
    for rows in chunk_rows:
        g = g_ref[0, rows, :]
        g_hi = g.astype(BF16)
        g_lo = (g - g_hi.astype(F32)).astype(BF16)
        cum.append(_dot(cum_mat, g_hi) + _dot(cum_mat, g_lo))

    q_ins, scores, upds, decays = [], [], [], []
    for rows, b in zip(chunk_rows, cum):
        b_last = b[C - 1:C, :]
        b_mid = b[C // 2 - 1:C // 2, :]
        q = q_ref[0, rows, :].astype(F32)
        k = k_ref[0, rows, :].astype(F32)
        q_ins.append((q * jnp.exp(b)).astype(BF16))
        q_mid = (q * jnp.exp(b - b_mid)).astype(BF16)
        k_mid = (k * jnp.exp(b_mid - b)).astype(BF16)
        k_out = (k * jnp.exp(b_last - b)).astype(BF16)
        s = _dot_nt(_head_stack(q_mid, lane_head), k_mid)
        scores.append(jnp.where(causal, s, 0.0).astype(BF16))
        upd = _dot_tn(v_ref[0, rows, :], k_out)
        own = upd[:GLA_DV, :]
        for h in range(1, GLA_HEADS):
            own = jnp.where(state_head == h, upd[h * GLA_DV:(h + 1) * GLA_DV, :], own)
        upds.append(own)
        decays.append(jnp.exp(b_last))

    state = state_ref[...]
    states = []
    for upd, decay in zip(upds, decays):
        states.append(state.astype(BF16))
        state = state * decay + upd
    state_ref[...] = state

    for rows, q_in, s, st in zip(chunk_rows, q_ins, scores, states):
        inter = _dot_nt(_head_stack(q_in, lane_head), st)
        for h in range(GLA_HEADS):
            hr = slice(h * C, (h + 1) * C)
            hv = slice(h * GLA_DV, (h + 1) * GLA_DV)
            o = inter[hr, :] + _dot(s[hr, :], v_ref[0, rows, hv])
            og = og_ref[0, rows, hv].astype(F32)
            o_ref[0, rows, hv] = (_rms(o) * gn * _silu(og)).astype(o_ref.dtype)


def _gla(gq, gk, gv, glog, gog, gla_norm, ts):
    bsz, seq, _ = gq.shape
    row = lambda b, i: (b, i, 0)
    return pl.pallas_call(
        functools.partial(_gla_kernel, chunks=ts // GLA_CHUNK),
        grid=(bsz, seq // ts),
        in_specs=[
            pl.BlockSpec((1, ts, GLA_QK), row),
            pl.BlockSpec((1, ts, GLA_QK), row),
            pl.BlockSpec((1, ts, GLA_WIDTH), row),
            pl.BlockSpec((1, ts, GLA_QK), row),
            pl.BlockSpec((1, ts, GLA_WIDTH), row),
            pl.BlockSpec((1, GLA_DV), lambda b, i: (0, 0)),
        ],
        out_specs=pl.BlockSpec((1, ts, GLA_WIDTH), row),
        out_shape=jax.ShapeDtypeStruct((bsz, seq, GLA_WIDTH), BF16),
        scratch_shapes=[pltpu.VMEM((GLA_DV, GLA_QK), F32)],
        compiler_params=pltpu.CompilerParams(
            dimension_semantics=("parallel", "arbitrary"), vmem_limit_bytes=VMEM_LIMIT),
        name="gla",
    )(gq, gk, gv, glog, gog, gla_norm)


def _diff_kernel(q_ref, k_ref, vt_ref, lq1_ref, lk1_ref, lq2_ref, lk2_ref, dn_ref, o_ref,
                 acc_ref, s_ref, *, tq, heads, q_blocks, lambda_init):
    step = pl.program_id(2)
    lane = lax.broadcasted_iota(jnp.int32, (tq, 2 * DIFF_DQK), 1)
    ones_rows = jnp.ones((_SUM_ROWS, tq), BF16)

    def head_cols(h, width):
        return slice(h * width, (h + 1) * width)

    def q_rows(a):
        return slice(a * tq, (a + 1) * tq)

    qs = []
    for a in range(q_blocks):
        qs.append([])
        for h in range(heads):
            q = q_ref[0, q_rows(a), head_cols(h, 2 * DIFF_DQK)]
            zero = jnp.zeros_like(q)
            qs[a].append(jnp.concatenate([jnp.where(lane < DIFF_DQK, q, zero),
                                          jnp.where(lane >= DIFF_DQK, q, zero)], axis=0))

    def score(slot, a, j):
        for h in range(heads):
            kb = k_ref[0, j * tq:(j + 1) * tq, head_cols(h, 2 * DIFF_DQK)]
            s_ref[slot, a, h] = _dot_nt(kb, qs[a][h])

    def consume(slot, a, j, ms, masked):
        out = []
        for h in range(heads):
            vtb = vt_ref[0, head_cols(h, DIFF_DV), j * tq:(j + 1) * tq]
            m_new, p = [], []
            for c in range(2 * tq // LANES):
                cols = slice(c * LANES, (c + 1) * LANES)
                st = s_ref[slot, a, h, :, cols]
                if masked:
                    key = lax.broadcasted_iota(jnp.int32, st.shape, 0)
                    qry = lax.broadcasted_iota(jnp.int32, st.shape, 1) + (c * LANES) % tq
                    st = jnp.where(key <= qry, st, -jnp.inf)
                mc = jnp.max(st, axis=0, keepdims=True)
                if ms is not None:
                    mc = jnp.maximum(ms[h][:, cols], mc)
                p.append(jnp.exp2(st - mc).astype(BF16))
                m_new.append(mc)
            m_new = jnp.concatenate(m_new, axis=1)
            v_aug = jnp.concatenate([vtb, ones_rows], axis=0)
            pv = _dot(v_aug, jnp.concatenate(p, axis=1))
            if ms is None:
                acc_ref[a, h] = pv
            else:
                acc_ref[a, h] = jnp.exp2(ms[h] - m_new) * acc_ref[a, h] + pv
            out.append(m_new)
        return tuple(out)

    def finish(a):
        lam = (jnp.exp(jnp.sum(lq1_ref[...] * lk1_ref[...], axis=-1, keepdims=True))
               - jnp.exp(jnp.sum(lq2_ref[...] * lk2_ref[...], axis=-1, keepdims=True))
               + lambda_init)
        for h in range(heads):
            acc = acc_ref[a, h]
            o_all = acc[:DIFF_DV] * (1.0 / acc[DIFF_DV:DIFF_DV + 1])
            ot = o_all[:, :tq] - lam * o_all[:, tq:]
            ot = ot * lax.rsqrt(jnp.mean(ot * ot, axis=0, keepdims=True) + EPS)
            o_ref[0, q_rows(a), head_cols(h, DIFF_DV)] = (
                ot.T * dn_ref[...] * (1.0 - lambda_init)).astype(o_ref.dtype)

    def run(s):
        last = [s * q_blocks + a for a in range(q_blocks)]
        ms = [None] * q_blocks
        for a in range(q_blocks):
            score(0, a, 0)
        for j in range(max(last) + 1):
            for a in range(q_blocks):
                if j > last[a]:
                    continue
                if j < last[a]:
                    score((j + 1) % 2, a, j + 1)
                ms[a] = consume(j % 2, a, j, ms[a], j == last[a])
                if j == last[a]:
                    finish(a)

    for s in range(k_ref.shape[1] // (tq * q_blocks)):
        pl.when(step == s)(functools.partial(run, s))


def _diff_attn(dq, dk, dvt, lq1, lk1, lq2, lk2, diff_norm, lambda_init, tq, heads, q_blocks):
    bsz, seq, _ = dq.shape
    vec = lambda n: pl.BlockSpec((1, n), lambda b, g, i: (0, 0))
    tqs = tq * q_blocks
    return pl.pallas_call(
        functools.partial(_diff_kernel, tq=tq, heads=heads, q_blocks=q_blocks,
                          lambda_init=lambda_init),
        grid=(bsz, DIFF_HEADS // heads, seq // tqs),
        in_specs=[
            pl.BlockSpec((1, tqs, heads * 2 * DIFF_DQK), lambda b, g, i: (b, i, g)),
            pl.BlockSpec((1, seq, heads * 2 * DIFF_DQK), lambda b, g, i: (b, 0, g)),
            pl.BlockSpec((1, heads * DIFF_DV, seq), lambda b, g, i: (b, g, 0)),
            vec(DIFF_DQK), vec(DIFF_DQK), vec(DIFF_DQK), vec(DIFF_DQK), vec(DIFF_DV),
        ],
        out_specs=pl.BlockSpec((1, tqs, heads * DIFF_DV), lambda b, g, i: (b, i, g)),
        out_shape=jax.ShapeDtypeStruct((bsz, seq, DIFF_WIDTH), BF16),
        scratch_shapes=[pltpu.VMEM((q_blocks, heads, DIFF_DV + _SUM_ROWS, 2 * tq), F32),
                        pltpu.VMEM((2, q_blocks, heads, tq, 2 * tq), F32)],
        compiler_params=pltpu.CompilerParams(
            dimension_semantics=("parallel", "parallel", "parallel"),
            vmem_limit_bytes=VMEM_LIMIT),
        name="diff_attn",
    )(dq, dk, dvt, lq1, lk1, lq2, lk2, diff_norm)


def _out_mlp_kernel(x_ref, go_ref, do_ref, ada_ref, pn_mix_ref, pre_mlp_ref, pn_mlp_ref,
                    wo_ref, wu_ref, wd_ref, o_ref, u_ref, *, ff_chunk, row_parts):
    tm = x_ref.shape[1]
    parts = [slice(p * tm // row_parts, (p + 1) * tm // row_parts) for p in range(row_parts)]
    gt_a, sh_m, sc_m, gt_m = _ada_rows(ada_ref, 2, 4)
    ys = [_dot(go_ref[0, r, :], wo_ref[:GLA_WIDTH, :]) + _dot(do_ref[0, r, :], wo_ref[GLA_WIDTH:, :])
          for r in parts]
    x1s = [x_ref[0, r, :] + gt_a * (_rms(y) * pn_mix_ref[...]) for r, y in zip(parts, ys)]
    hs = [(_rms(x1) * pre_mlp_ref[...] * (1.0 + sc_m) + sh_m).astype(BF16) for x1 in x1s]
    d_ff = wu_ref.shape[1]
    for f in range(d_ff // ff_chunk):
        cols = slice(f * ff_chunk, (f + 1) * ff_chunk)
        for r, h in zip(parts, hs):
            u = jnp.maximum(_dot(h, wu_ref[:, cols]), 0.0)
            u_ref[r, cols] = (u * u).astype(BF16)
    y2s = [_dot(u_ref[r, :], wd_ref[...]) for r in parts]
    for r, x1, y2 in zip(parts, x1s, y2s):
        o_ref[0, r, :] = x1 + gt_m * (_rms(y2) * pn_mlp_ref[...])


def _out_mlp(x, go, do, ada, post_mix, pre_mlp, post_mlp, w_out, w_up, w_down, tm, ff_chunk):
    bsz, seq, d = x.shape
    d_ff = w_up.shape[1]
    row = lambda b, i: (b, i, 0)
    const2 = lambda b, i: (0, 0)
    resident = functools.partial(pl.BlockSpec, index_map=const2, pipeline_mode=pl.Buffered(1))
    return pl.pallas_call(
        functools.partial(_out_mlp_kernel, ff_chunk=ff_chunk, row_parts=tm // 256),
        grid=(bsz, seq // tm),
        in_specs=[
            pl.BlockSpec((1, tm, d), row),
            pl.BlockSpec((1, tm, GLA_WIDTH), row),
            pl.BlockSpec((1, tm, DIFF_WIDTH), row),
            pl.BlockSpec((bsz, N_ADA * d), const2),
            pl.BlockSpec((1, d), const2),
            pl.BlockSpec((1, d), const2),
            pl.BlockSpec((1, d), const2),
            resident((d, d)),
            resident((d, d_ff)),
            resident((d_ff, d)),
        ],
        out_specs=pl.BlockSpec((1, tm, d), row),
        out_shape=jax.ShapeDtypeStruct((bsz, seq, d), F32),
        scratch_shapes=[pltpu.VMEM((tm, d_ff), BF16)],
        compiler_params=pltpu.CompilerParams(
            dimension_semantics=("parallel", "parallel"), vmem_limit_bytes=VMEM_LIMIT),
        name="out_mlp",
    )(x, go, do, ada, post_mix, pre_mlp, post_mlp, w_out, w_up, w_down)


def kernel(x, c, positions, ada_w, ada_b, pre_norm_mix, post_norm_mix, w_in, gla_gate_w, gla_gate_b, gla_norm, lambda_q1, lambda_k1, lambda_q2, lambda_k2, diff_norm, w_out, pre_norm_mlp, post_norm_mlp, w_up, w_down):
    depth = ada_w.shape[0]
    bsz, seq, d = x.shape
    vec = lambda t: t.reshape(1, -1)
    for l in range(depth):
        lambda_init = 0.8 - 0.6 * math.exp(-0.3 * l)
        ada, cos, sin = _ada_rope(c, ada_w[l], ada_b[l], positions)
        (gq, gk, gv, gog, glog, dq, dk, dvt), (w_out_b, w_up_b, w_down_b) = _inproj(
            x, ada, vec(pre_norm_mix[l]), cos, sin, w_in[l].T, gla_gate_w[l].astype(BF16),
            vec(gla_gate_b[l]), (w_out[l], w_up[l], w_down[l]), tm=512)
        go = _gla(gq, gk, gv, glog, gog, vec(gla_norm[l]), ts=1024)
        do = _diff_attn(dq, dk, dvt, vec(lambda_q1[l]), vec(lambda_k1[l]), vec(lambda_q2[l]),
                        vec(lambda_k2[l]), vec(diff_norm[l]), lambda_init, tq=256, heads=4, q_blocks=2)
        x = _out_mlp(x, go, do, ada, vec(post_norm_mix[l]), vec(pre_norm_mlp[l]),
                     vec(post_norm_mlp[l]), w_out_b, w_up_b, w_down_b, tm=512, ff_chunk=1024)
    return x
```

```python
import functools
import math

import jax
import jax.numpy as jnp
from jax import lax
from jax.experimental import pallas as pl
from jax.experimental.pallas import tpu as pltpu

F32 = jnp.float32
BF16 = jnp.bfloat16

GLA_HEADS = 4
GLA_DK = 64
GLA_DV = 128
GLA_QK = GLA_HEADS * GLA_DK
GLA_WIDTH = GLA_HEADS * GLA_DV
GLA_GATE_RANK = 16
GLA_GATE_NORM = 16.0
GLA_CHUNK = 64
DIFF_HEADS = 4
DIFF_DQK = 64
DIFF_DV = 128
DIFF_QK = DIFF_HEADS * 2 * DIFF_DQK
DIFF_WIDTH = DIFF_HEADS * DIFF_DV
ROPE_THETA = 10000.0
EPS = 1e-6
N_ADA = 6

LANES = 128
ROPE_HALF = DIFF_DQK // 2
_SUM_ROWS = 16
LOG2E = math.log2(math.e)
VMEM_LIMIT = 56 * 1024 * 1024

_COL_GQ = 0
_COL_GK = _COL_GQ + GLA_QK
_COL_GV = _COL_GK + GLA_QK
_COL_LR = _COL_GV + GLA_WIDTH
_COL_GOG = _COL_LR + GLA_GATE_RANK
_COL_DQ = _COL_GOG + GLA_WIDTH
_COL_DK = _COL_DQ + DIFF_QK
_COL_DV = _COL_DK + DIFF_QK
_COL_END = _COL_DV + DIFF_WIDTH


def _dot(a, b):
    return jnp.dot(a, b, preferred_element_type=F32)


def _dot_nt(a, b):
    return lax.dot_general(a, b, (((1,), (1,)), ((), ())), preferred_element_type=F32)


def _dot_tn(a, b):
    return lax.dot_general(a, b, (((0,), (0,)), ((), ())), preferred_element_type=F32)


def _rms(t):
    return t * lax.rsqrt(jnp.mean(t * t, axis=-1, keepdims=True) + EPS)


def _silu(t):
    return t * (1.0 / (1.0 + jnp.exp(-t)))


def _ada_kernel(c_ref, w_ref, b_ref, pos_ref, freq_ref, o_ref, cos_ref, sin_ref):
    ca = _silu(c_ref[...]).astype(BF16)
    o_ref[...] = _dot(ca, w_ref[...].astype(BF16)) + b_ref[...]
    ang = pos_ref[...] * freq_ref[...]
    tr = ang.shape[0]
    for table_ref, t in ((cos_ref, jnp.cos(ang)), (sin_ref, jnp.sin(ang))):
        for g in range(LANES // ROPE_HALF):
            tg = t if g == 0 else pltpu.roll(t, LANES - g * ROPE_HALF, 1)
            table_ref[g * tr:(g + 1) * tr, :] = tg[:, :ROPE_HALF]


def _ada_rope(c, ada_w, ada_b, positions, steps=8):
    bsz, d = c.shape
    n = ada_w.shape[1]
    tn = n // steps
    per_row = LANES // ROPE_HALF
    inv_freq = 1.0 / (ROPE_THETA ** (jnp.arange(0, DIFF_DQK, 2, dtype=F32) / DIFF_DQK))
    freq = jnp.tile(inv_freq, per_row).reshape(1, LANES)
    tokens = positions.size
    tr = tokens // (steps * per_row)
    pos = positions.astype(F32).reshape(steps, per_row, tr).transpose(0, 2, 1)
    pos = jnp.repeat(pos.reshape(steps * tr, per_row), ROPE_HALF, axis=1)
    dense = pl.BlockSpec((tr, LANES), lambda j: (j, 0))
    table = (pl.BlockSpec((tr * per_row, ROPE_HALF), lambda j: (j, 0)),
             jax.ShapeDtypeStruct((tokens, ROPE_HALF), F32))
    ada, cos, sin = pl.pallas_call(
        _ada_kernel,
        grid=(steps,),
        in_specs=[
            pl.BlockSpec((bsz, d), lambda j: (0, 0)),
            pl.BlockSpec((d, tn), lambda j: (0, j)),
            pl.BlockSpec((1, tn), lambda j: (0, j)),
            dense,
            pl.BlockSpec((1, LANES), lambda j: (0, 0)),
        ],
        out_specs=[pl.BlockSpec((bsz, tn), lambda j: (0, j)), table[0], table[0]],
        out_shape=[jax.ShapeDtypeStruct((bsz, n), F32), table[1], table[1]],
        compiler_params=pltpu.CompilerParams(
            dimension_semantics=("parallel",), vmem_limit_bytes=VMEM_LIMIT),
        name="ada_ln",
    )(c, ada_w, ada_b.reshape(1, n), pos, freq)
    shape = positions.shape + (ROPE_HALF,)
    return ada, cos.reshape(shape), sin.reshape(shape)


def _ada_rows(ada_ref, first, count):
    b = pl.program_id(0)
    d = ada_ref.shape[1] // N_ADA
    return [ada_ref[pl.ds(b, 1), n * d:(n + 1) * d] for n in range(first, first + count)]


def _inproj_kernel(x_ref, ada_ref, pn_ref, cos_ref, sin_ref, wt_ref, gw_ref, gb_ref, *rest, n_cast):
    cast_src, rest = rest[:n_cast], rest[n_cast:]
    gq_ref, gk_ref, gv_ref, gog_ref, glog_ref, dq_ref, dk_ref, dvt_ref = rest[:8]
    wb_ref = rest[-1]
    for src, dst in zip(cast_src, rest[8:8 + n_cast]):
        dst[...] = src[...].astype(dst.dtype)

    @pl.when((pl.program_id(0) == 0) & (pl.program_id(1) == 0))
    def _():
        wb_ref[...] = wt_ref[...].astype(BF16)

    x = x_ref[0]
    shift, scale = _ada_rows(ada_ref, 0, 2)
    h = _rms(x) * pn_ref[...] * (1.0 + scale) + shift
    hb = h.astype(BF16)

    def proj(lo, hi):
        return _dot_nt(hb, wb_ref[lo:hi, :])

    dvt_ref[0] = proj(_COL_DV, _COL_END).T.astype(BF16)

    c32, s32 = cos_ref[0], sin_ref[0]
    cs = jnp.concatenate([c32, s32, c32, s32], axis=1)
    lane = lax.broadcasted_iota(jnp.int32, cs.shape, 1)
    first_half = (lane % DIFF_DQK) < ROPE_HALF
    cos = jnp.where(first_half, cs, pltpu.roll(cs, ROPE_HALF, 1))
    sin = jnp.where(first_half, -pltpu.roll(cs, LANES - ROPE_HALF, 1), cs)

    def rope_store(out_ref, lo, scale):
        t = proj(lo, lo + DIFF_QK)
        for c in range(DIFF_QK // LANES):
            tc = t[:, c * LANES:(c + 1) * LANES]
            partner = jnp.where(first_half, pltpu.roll(tc, LANES - ROPE_HALF, 1),
                                pltpu.roll(tc, ROPE_HALF, 1))
            out_ref[0, :, c * LANES:(c + 1) * LANES] = ((tc * cos + partner * sin) * scale).astype(BF16)

    rope_store(dq_ref, _COL_DQ, DIFF_DQK ** -0.5 * LOG2E)
    rope_store(dk_ref, _COL_DK, 1.0)

    lr = proj(_COL_LR, _COL_GOG).astype(BF16)
    z = _dot(lr, gw_ref[...]) + gb_ref[...]
    log_sig = jnp.minimum(z, 0.0) - jnp.log1p(jnp.exp(-jnp.abs(z)))
    glog_ref[0] = log_sig * (LOG2E / GLA_GATE_NORM)

    gq_ref[0] = (proj(_COL_GQ, _COL_GK) * (GLA_DK ** -0.5)).astype(BF16)
    gk_ref[0] = proj(_COL_GK, _COL_GV).astype(BF16)
    gv_ref[0] = proj(_COL_GV, _COL_LR).astype(BF16)
    gog_ref[0] = proj(_COL_GOG, _COL_DQ).astype(BF16)


def _inproj(x, ada, pre_norm, cos, sin, w_in_t, gate_w, gate_b, later_weights, tm):
    bsz, seq, d = x.shape
    grid = (bsz, seq // tm)
    row = lambda b, i: (b, i, 0)
    const2 = lambda b, i: (0, 0)

    def out(width, dtype):
        return (pl.BlockSpec((1, tm, width), row), jax.ShapeDtypeStruct((bsz, seq, width), dtype))

    dvt = (pl.BlockSpec((1, DIFF_WIDTH, tm), lambda b, i: (b, 0, i)),
           jax.ShapeDtypeStruct((bsz, DIFF_WIDTH, seq), BF16))
    outs = [out(GLA_QK, BF16), out(GLA_QK, BF16), out(GLA_WIDTH, BF16), out(GLA_WIDTH, BF16),
            out(GLA_QK, F32), out(DIFF_QK, BF16), out(DIFF_QK, BF16), dvt]

    steps = grid[0] * grid[1]
    slab = lambda b, i: (b * grid[1] + i, 0)
    cast_in, cast_out = [], []
    for w in later_weights:
        rows, cols = w.shape
        assert rows % (steps * 16) == 0, (rows, steps)
        cast_in.append(pl.BlockSpec((rows // steps, cols), slab))
        cast_out.append((pl.BlockSpec((rows // steps, cols), slab),
                         jax.ShapeDtypeStruct((rows, cols), BF16)))
    res = pl.pallas_call(
        functools.partial(_inproj_kernel, n_cast=len(later_weights)),
        grid=grid,
        in_specs=[
            pl.BlockSpec((1, tm, d), row),
            pl.BlockSpec((bsz, N_ADA * d), const2),
            pl.BlockSpec((1, d), const2),
            pl.BlockSpec((1, tm, ROPE_HALF), row),
            pl.BlockSpec((1, tm, ROPE_HALF), row),
            pl.BlockSpec((_COL_END, d), const2, pipeline_mode=pl.Buffered(1)),
            pl.BlockSpec((GLA_GATE_RANK, GLA_QK), const2),
            pl.BlockSpec((1, GLA_QK), const2),
        ] + cast_in,
        out_specs=[o[0] for o in outs + cast_out],
        out_shape=[o[1] for o in outs + cast_out],
        scratch_shapes=[pltpu.VMEM((_COL_END, d), BF16)],
        compiler_params=pltpu.CompilerParams(
            dimension_semantics=("arbitrary", "arbitrary"), vmem_limit_bytes=VMEM_LIMIT),
        name="in_proj",
    )(x, ada, pre_norm, cos, sin, w_in_t, gate_w, gate_b, *later_weights)
    return res[:len(outs)], res[len(outs):]


def _head_stack(t, lane_head):
    return jnp.concatenate(
        [jnp.where(lane_head == h, t, jnp.zeros_like(t)) for h in range(GLA_HEADS)], axis=0)


def _gla_kernel(q_ref, k_ref, v_ref, g_ref, og_ref, gn_ref, o_ref, state_ref, *, chunks):
    C = GLA_CHUNK

    @pl.when(pl.program_id(1) == 0)
    def _():
        state_ref[...] = jnp.zeros_like(state_ref)

    row = lax.broadcasted_iota(jnp.int32, (C, C), 0)
    col = lax.broadcasted_iota(jnp.int32, (C, C), 1)
    cum_mat = (row >= col).astype(BF16)
    lane_head = lax.broadcasted_iota(jnp.int32, (C, GLA_QK), 1) // GLA_DK
    key = lax.broadcasted_iota(jnp.int32, (C, GLA_HEADS * C), 1) % C
    causal = lax.broadcasted_iota(jnp.int32, (C, GLA_HEADS * C), 0) >= key
    state_head = lax.broadcasted_iota(jnp.int32, (GLA_DV, GLA_QK), 1) // GLA_DK
    gn = gn_ref[...]
    zeros_v = jnp.zeros((C, GLA_DV), BF16)

    chunk_rows = [slice(c * C, (c + 1) * C) for c in range(chunks)]

    cum = []
    for rows in chunk_rows:
        g = g_ref[0, rows, :]
        g_hi = g.astype(BF16)
        g_lo = (g - g_hi.astype(F32)).astype(BF16)
        cum.append(_dot(cum_mat, g_hi) + _dot(cum_mat, g_lo))

    q_ins, scores, upds, decays = [], [], [], []
    for rows, b in zip(chunk_rows, cum):
        b_last = b[C - 1:C, :]
        b_mid = b[C // 2 - 1:C // 2, :]
        q = q_ref[0, rows, :].astype(F32)
        k = k_ref[0, rows, :].astype(F32)
        q_ins.append((q * jnp.exp2(b)).astype(BF16))
        q_mid = (q * jnp.exp2(b - b_mid)).astype(BF16)
        k_mid = (k * jnp.exp2(b_mid - b)).astype(BF16)
        k_out = (k * jnp.exp2(b_last - b)).astype(BF16)
        s = _dot_nt(q_mid, _head_stack(k_mid, lane_head))
        scores.append(jnp.where(causal, s, 0.0).astype(BF16))
        v_stack = jnp.concatenate(
            [v_ref[0, rows, h * GLA_DV:(h + 1) * GLA_DV] for h in range(GLA_HEADS)], axis=0)
        upds.append(_dot_tn(v_stack, _head_stack(k_out, lane_head)))
        decays.append(jnp.exp2(b_last))

    state = state_ref[...]
    bd_states = []
    for upd, decay in zip(upds, decays):
        sb = state.astype(BF16)
        bd_states.append(jnp.concatenate(
            [jnp.where(state_head == h, sb, jnp.zeros_like(sb)) for h in range(GLA_HEADS)], axis=0))
        state = state * decay + upd
    state_ref[...] = state

    for rows, q_in, s, bd_state in zip(chunk_rows, q_ins, scores, bd_states):
        bd_v = jnp.concatenate(
            [jnp.concatenate([v_ref[0, rows, h * GLA_DV:(h + 1) * GLA_DV] if hh == h else zeros_v
                              for hh in range(GLA_HEADS)], axis=1)
             for h in range(GLA_HEADS)], axis=0)
        o_all = _dot_nt(q_in, bd_state) + _dot(s, bd_v)
        for h in range(GLA_HEADS):
            hv = slice(h * GLA_DV, (h + 1) * GLA_DV)
            og = og_ref[0, rows, hv].astype(F32)
            o_ref[0, rows, hv] = (_rms(o_all[:, hv]) * gn * _silu(og)).astype(o_ref.dtype)


def _gla(gq, gk, gv, glog, gog, gla_norm, ts):
    bsz, seq, _ = gq.shape
    row = lambda b, i: (b, i, 0)
    return pl.pallas_call(
        functools.partial(_gla_kernel, chunks=ts // GLA_CHUNK),
        grid=(bsz, seq // ts),
        in_specs=[
            pl.BlockSpec((1, ts, GLA_QK), row),
            pl.BlockSpec((1, ts, GLA_QK), row),
            pl.BlockSpec((1, ts, GLA_WIDTH), row),
            pl.BlockSpec((1, ts, GLA_QK), row),
            pl.BlockSpec((1, ts, GLA_WIDTH), row),
            pl.BlockSpec((1, GLA_DV), lambda b, i: (0, 0)),
        ],
        out_specs=pl.BlockSpec((1, ts, GLA_WIDTH), row),
        out_shape=jax.ShapeDtypeStruct((bsz, seq, GLA_WIDTH), BF16),
        scratch_shapes=[pltpu.VMEM((GLA_DV, GLA_QK), F32)],
        compiler_params=pltpu.CompilerParams(
            dimension_semantics=("parallel", "arbitrary"), vmem_limit_bytes=VMEM_LIMIT),
        name="gla",
    )(gq, gk, gv, glog, gog, gla_norm)


def _diff_kernel(q_ref, k_ref, vt_ref, lq1_ref, lk1_ref, lq2_ref, lk2_ref, dn_ref, o_ref,
                 acc_ref, s_ref, *, tq, heads, q_blocks, lambda_init):
    step = pl.program_id(2)
    lane = lax.broadcasted_iota(jnp.int32, (tq, 2 * DIFF_DQK), 1)
    ones_rows = jnp.ones((_SUM_ROWS, tq), BF16)

    def head_cols(h, width):
        return slice(h * width, (h + 1) * width)

    def q_rows(a):
        return slice(a * tq, (a + 1) * tq)

    qs = []
    for a in range(q_blocks):
        qs.append([])
        for h in range(heads):
            q = q_ref[0, q_rows(a), head_cols(h, 2 * DIFF_DQK)]
            zero = jnp.zeros_like(q)
            qs[a].append(jnp.concatenate([jnp.where(lane < DIFF_DQK, q, zero),
                                          jnp.where(lane >= DIFF_DQK, q, zero)], axis=0))

    def score(slot, a, j):
        for h in range(heads):
            kb = k_ref[0, j * tq:(j + 1) * tq, head_cols(h, 2 * DIFF_DQK)]
            s_ref[slot, a, h] = _dot_nt(kb, qs[a][h])

    def consume(slot, a, j, ms, masked):
        out = []
        for h in range(heads):
            vtb = vt_ref[0, head_cols(h, DIFF_DV), j * tq:(j + 1) * tq]
            m_new, p = [], []
            for c in range(2 * tq // LANES):
                cols = slice(c * LANES, (c + 1) * LANES)
                st = s_ref[slot, a, h, :, cols]
                if masked:
                    key = lax.broadcasted_iota(jnp.int32, st.shape, 0)
                    qry = lax.broadcasted_iota(jnp.int32, st.shape, 1) + (c * LANES) % tq
                    st = jnp.where(key <= qry, st, -jnp.inf)
                mc = jnp.max(st, axis=0, keepdims=True)
                if ms is not None:
                    mc = jnp.maximum(ms[h][:, cols], mc)
                p.append(jnp.exp2(st - mc).astype(BF16))
                m_new.append(mc)
            m_new = jnp.concatenate(m_new, axis=1)
            v_aug = jnp.concatenate([vtb, ones_rows], axis=0)
            pv = _dot(v_aug, jnp.concatenate(p, axis=1))
            if ms is None:
                acc_ref[a, h] = pv
            else:
                acc_ref[a, h] = jnp.exp2(ms[h] - m_new) * acc_ref[a, h] + pv
            out.append(m_new)
        return tuple(out)

    def finish(a):
        lam = (jnp.exp(jnp.sum(lq1_ref[...] * lk1_ref[...], axis=-1, keepdims=True))
               - jnp.exp(jnp.sum(lq2_ref[...] * lk2_ref[...], axis=-1, keepdims=True))
               + lambda_init)
        for h in range(heads):
            acc = acc_ref[a, h]
            o_all = acc[:DIFF_DV] * (1.0 / acc[DIFF_DV:DIFF_DV + 1])
            ot = o_all[:, :tq] - lam * o_all[:, tq:]
            ot = ot * lax.rsqrt(jnp.mean(ot * ot, axis=0, keepdims=True) + EPS)
            o_ref[0, q_rows(a), head_cols(h, DIFF_DV)] = (
                ot.T * dn_ref[...] * (1.0 - lambda_init)).astype(o_ref.dtype)

    def run(s):
        last = [s * q_blocks + a for a in range(q_blocks)]
        ms = [None] * q_blocks
        for a in range(q_blocks):
            score(0, a, 0)
        for j in range(max(last) + 1):
            for a in range(q_blocks):
                if j > last[a]:
                    continue
                if j < last[a]:
                    score((j + 1) % 2, a, j + 1)
                ms[a] = consume(j % 2, a, j, ms[a], j == last[a])
                if j == last[a]:
                    finish(a)

    for s in range(k_ref.shape[1] // (tq * q_blocks)):
        pl.when(step == s)(functools.partial(run, s))


def _diff_attn(dq, dk, dvt, lq1, lk1, lq2, lk2, diff_norm, lambda_init, tq, heads, q_blocks):
    bsz, seq, _ = dq.shape
    vec = lambda n: pl.BlockSpec((1, n), lambda b, g, i: (0, 0))
    tqs = tq * q_blocks
    return pl.pallas_call(
        functools.partial(_diff_kernel, tq=tq, heads=heads, q_blocks=q_blocks,
                          lambda_init=lambda_init),
        grid=(bsz, DIFF_HEADS // heads, seq // tqs),
        in_specs=[
            pl.BlockSpec((1, tqs, heads * 2 * DIFF_DQK), lambda b, g, i: (b, i, g)),
            pl.BlockSpec((1, seq, heads * 2 * DIFF_DQK), lambda b, g, i: (b, 0, g)),
            pl.BlockSpec((1, heads * DIFF_DV, seq), lambda b, g, i: (b, g, 0)),
            vec(DIFF_DQK), vec(DIFF_DQK), vec(DIFF_DQK), vec(DIFF_DQK), vec(DIFF_DV),
        ],
        out_specs=pl.BlockSpec((1, tqs, heads * DIFF_DV), lambda b, g, i: (b, i, g)),
        out_shape=jax.ShapeDtypeStruct((bsz, seq, DIFF_WIDTH), BF16),
        scratch_shapes=[pltpu.VMEM((q_blocks, heads, DIFF_DV + _SUM_ROWS, 2 * tq), F32),
                        pltpu.VMEM((2, q_blocks, heads, tq, 2 * tq), F32)],
        compiler_params=pltpu.CompilerParams(
            dimension_semantics=("parallel", "parallel", "parallel"),
            vmem_limit_bytes=VMEM_LIMIT),
        name="diff_attn",
    )(dq, dk, dvt, lq1, lk1, lq2, lk2, diff_norm)


def _out_mlp_kernel(x_ref, go_ref, do_ref, ada_ref, pn_mix_ref, pre_mlp_ref, pn_mlp_ref,
                    wo_ref, wu_ref, wd_ref, o_ref, u_ref, *, ff_chunk, row_parts):
    tm = x_ref.shape[1]
    parts = [slice(p * tm // row_parts, (p + 1) * tm // row_parts) for p in range(row_parts)]
    gt_a, sh_m, sc_m, gt_m = _ada_rows(ada_ref, 2, 4)
    ys = [_dot(go_ref[0, r, :], wo_ref[:GLA_WIDTH, :]) + _dot(do_ref[0, r, :], wo_ref[GLA_WIDTH:, :])
          for r in parts]
    x1s = [x_ref[0, r, :] + gt_a * (_rms(y) * pn_mix_ref[...]) for r, y in zip(parts, ys)]
    hs = [(_rms(x1) * pre_mlp_ref[...] * (1.0 + sc_m) + sh_m).astype(BF16) for x1 in x1s]
    d_ff = wu_ref.shape[1]
    for f in range(d_ff // ff_chunk):
        cols = slice(f * ff_chunk, (f + 1) * ff_chunk)
        for r, h in zip(parts, hs):
            u = jnp.maximum(_dot(h, wu_ref[:, cols]), 0.0)
            u_ref[r, cols] = (u * u).astype(BF16)
    y2s = [_dot(u_ref[r, :], wd_ref[...]) for r in parts]
    for r, x1, y2 in zip(parts, x1s, y2s):
        o_ref[0, r, :] = x1 + gt_m * (_rms(y2) * pn_mlp_ref[...])


def _out_mlp(x, go, do, ada, post_mix, pre_mlp, post_mlp, w_out, w_up, w_down, tm, ff_chunk):
    bsz, seq, d = x.shape
    d_ff = w_up.shape[1]
    row = lambda b, i: (b, i, 0)
    const2 = lambda b, i: (0, 0)
    resident = functools.partial(pl.BlockSpec, index_map=const2, pipeline_mode=pl.Buffered(1))
    return pl.pallas_call(
        functools.partial(_out_mlp_kernel, ff_chunk=ff_chunk, row_parts=tm // 256),
        grid=(bsz, seq // tm),
        in_specs=[
            pl.BlockSpec((1, tm, d), row),
            pl.BlockSpec((1, tm, GLA_WIDTH), row),
            pl.BlockSpec((1, tm, DIFF_WIDTH), row),
            pl.BlockSpec((bsz, N_ADA * d), const2),
            pl.BlockSpec((1, d), const2),
            pl.BlockSpec((1, d), const2),
            pl.BlockSpec((1, d), const2),
            resident((d, d)),
            resident((d, d_ff)),
            resident((d_ff, d)),
        ],
        out_specs=pl.BlockSpec((1, tm, d), row),
        out_shape=jax.ShapeDtypeStruct((bsz, seq, d), F32),
        scratch_shapes=[pltpu.VMEM((tm, d_ff), BF16)],
        compiler_params=pltpu.CompilerParams(
            dimension_semantics=("parallel", "parallel"), vmem_limit_bytes=VMEM_LIMIT),
        name="out_mlp",
    )(x, go, do, ada, post_mix, pre_mlp, post_mlp, w_out, w_up, w_down)


def kernel(x, c, positions, ada_w, ada_b, pre_norm_mix, post_norm_mix, w_in, gla_gate_w, gla_gate_b, gla_norm, lambda_q1, lambda_k1, lambda_q2, lambda_k2, diff_norm, w_out, pre_norm_mlp, post_norm_mlp, w_up, w_down):
    depth = ada_w.shape[0]
    bsz, seq, d = x.shape
    vec = lambda t: t.reshape(1, -1)
    for l in range(depth):
        lambda_init = 0.8 - 0.6 * math.exp(-0.3 * l)
        ada, cos, sin = _ada_rope(c, ada_w[l], ada_b[l], positions)
        (gq, gk, gv, gog, glog, dq, dk, dvt), (w_out_b, w_up_b, w_down_b) = _inproj(
            x, ada, vec(pre_norm_mix[l]), cos, sin, w_in[l].T, gla_gate_w[l].astype(BF16),
            vec(gla_gate_b[l]), (w_out[l], w_up[l], w_down[l]), tm=1024)
        go = _gla(gq, gk, gv, glog, gog, vec(gla_norm[l]), ts=1024)
        do = _diff_attn(dq, dk, dvt, vec(lambda_q1[l]), vec(lambda_k1[l]), vec(lambda_q2[l]),
                        vec(lambda_k2[l]), vec(diff_norm[l]), lambda_init, tq=256, heads=4, q_blocks=2)
        x = _out_mlp(x, go, do, ada, vec(post_norm_mix[l]), vec(pre_norm_mlp[l]),
                     vec(post_norm_mlp[l]), w_out_b, w_up_b, w_down_b, tm=512, ff_chunk=1024)
    return x
```

```python
import functools
import math

import jax
import jax.numpy as jnp
from jax import lax
from jax.experimental import pallas as pl
from jax.experimental.pallas import tpu as pltpu

F32 = jnp.float32
BF16 = jnp.bfloat16

GLA_HEADS = 4
GLA_DK = 64
GLA_DV = 128
GLA_QK = GLA_HEADS * GLA_DK
GLA_WIDTH = GLA_HEADS * GLA_DV
GLA_GATE_RANK = 16
GLA_GATE_NORM = 16.0
GLA_CHUNK = 64
DIFF_HEADS = 4
DIFF_DQK = 64
DIFF_DV = 128
DIFF_QK = DIFF_HEADS * 2 * DIFF_DQK
DIFF_WIDTH = DIFF_HEADS * DIFF_DV
ROPE_THETA = 10000.0
EPS = 1e-6
N_ADA = 6

LANES = 128
ROPE_HALF = DIFF_DQK // 2
_SUM_ROWS = 16
LOG2E = math.log2(math.e)
VMEM_LIMIT = 56 * 1024 * 1024

_SRC_GROUPS = (("gq", GLA_QK), ("gk", GLA_QK), ("gv", GLA_WIDTH), ("lr", GLA_GATE_RANK),
               ("gog", GLA_WIDTH), ("dq", DIFF_QK), ("dk", DIFF_QK), ("dv", DIFF_WIDTH))
_SRC_LO = {name: sum(w for _, w in _SRC_GROUPS[:n]) for n, (name, _) in enumerate(_SRC_GROUPS)}
_SRC_END = sum(w for _, w in _SRC_GROUPS)
GATE_PAD = LANES
_COL_GQ = 0
_COL_GK = _COL_GQ + GLA_QK
_COL_GV = _COL_GK + GLA_QK
_COL_GOG = _COL_GV + GLA_WIDTH
_COL_DQ = _COL_GOG + GLA_WIDTH
_COL_DK = _COL_DQ + DIFF_QK
_COL_DV = _COL_DK + DIFF_QK
_COL_LR = _COL_DV + DIFF_WIDTH
_COL_END = _COL_LR + GATE_PAD
_PACKED_LO = {"gq": _COL_GQ, "gk": _COL_GK, "gv": _COL_GV, "gog": _COL_GOG, "dq": _COL_DQ,
              "dk": _COL_DK, "dv": _COL_DV, "lr": _COL_LR}


def _dot(a, b):
    return jnp.dot(a, b, preferred_element_type=F32)


def _dot_nt(a, b):
    return lax.dot_general(a, b, (((1,), (1,)), ((), ())), preferred_element_type=F32)


def _dot_tn(a, b):
    return lax.dot_general(a, b, (((0,), (0,)), ((), ())), preferred_element_type=F32)


def _rms(t):
    return t * lax.rsqrt(jnp.mean(t * t, axis=-1, keepdims=True) + EPS)


def _silu(t):
    return t * (1.0 / (1.0 + jnp.exp(-t)))


def _ada_kernel(c_ref, w_ref, b_ref, pos_ref, freq_ref, o_ref, cos_ref, sin_ref):
    ca = _silu(c_ref[...]).astype(BF16)
    o_ref[...] = _dot(ca, w_ref[...].astype(BF16)) + b_ref[...]
    ang = pos_ref[...] * freq_ref[...]
    tr = ang.shape[0]
    for table_ref, t in ((cos_ref, jnp.cos(ang)), (sin_ref, jnp.sin(ang))):
        for g in range(LANES // ROPE_HALF):
            tg = t if g == 0 else pltpu.roll(t, LANES - g * ROPE_HALF, 1)
            table_ref[g * tr:(g + 1) * tr, :] = tg[:, :ROPE_HALF]


def _ada_rope(c, ada_w, ada_b, positions, steps=8):
    bsz, d = c.shape
    n = ada_w.shape[1]
    tn = n // steps
    per_row = LANES // ROPE_HALF
    inv_freq = 1.0 / (ROPE_THETA ** (jnp.arange(0, DIFF_DQK, 2, dtype=F32) / DIFF_DQK))
    freq = jnp.tile(inv_freq, per_row).reshape(1, LANES)
    tokens = positions.size
    tr = tokens // (steps * per_row)
    pos = positions.astype(F32).reshape(steps, per_row, tr).transpose(0, 2, 1)
    pos = jnp.repeat(pos.reshape(steps * tr, per_row), ROPE_HALF, axis=1)
    dense = pl.BlockSpec((tr, LANES), lambda j: (j, 0))
    table = (pl.BlockSpec((tr * per_row, ROPE_HALF), lambda j: (j, 0)),
             jax.ShapeDtypeStruct((tokens, ROPE_HALF), F32))
    ada, cos, sin = pl.pallas_call(
        _ada_kernel,
        grid=(steps,),
        in_specs=[
            pl.BlockSpec((bsz, d), lambda j: (0, 0)),
            pl.BlockSpec((d, tn), lambda j: (0, j)),
            pl.BlockSpec((1, tn), lambda j: (0, j)),
            dense,
            pl.BlockSpec((1, LANES), lambda j: (0, 0)),
        ],
        out_specs=[pl.BlockSpec((bsz, tn), lambda j: (0, j)), table[0], table[0]],
        out_shape=[jax.ShapeDtypeStruct((bsz, n), F32), table[1], table[1]],
        compiler_params=pltpu.CompilerParams(
            dimension_semantics=("parallel",), vmem_limit_bytes=VMEM_LIMIT),
        name="ada_ln",
    )(c, ada_w, ada_b.reshape(1, n), pos, freq)
    shape = positions.shape + (ROPE_HALF,)
    return ada, cos.reshape(shape), sin.reshape(shape)


def _ada_rows(ada_ref, first, count):
    b = pl.program_id(0)
    d = ada_ref.shape[1] // N_ADA
    return [ada_ref[pl.ds(b, 1), n * d:(n + 1) * d] for n in range(first, first + count)]


def _inproj_kernel(x_ref, ada_ref, pn_ref, cos_ref, sin_ref, wt_ref, gw_ref, gb_ref, *rest, n_cast):
    cast_src, rest = rest[:n_cast], rest[n_cast:]
    gq_ref, gk_ref, gv_ref, gog_ref, glog_ref, dq_ref, dk_ref, dvt_ref = rest[:8]
    wb_ref = rest[-1]
    for src, dst in zip(cast_src, rest[8:8 + n_cast]):
        dst[...] = src[...].astype(dst.dtype)

    @pl.when((pl.program_id(0) == 0) & (pl.program_id(1) == 0))
    def _():
        for name, width in _SRC_GROUPS:
            lo, dst = _SRC_LO[name], _PACKED_LO[name]
            if name == "lr":
                blk = wt_ref[lo:lo + GATE_PAD, :].T
                keep = lax.broadcasted_iota(jnp.int32, blk.shape, 1) < width
                wb_ref[:, dst:dst + GATE_PAD] = jnp.where(keep, blk, 0.0).astype(BF16)
            else:
                wb_ref[:, dst:dst + width] = wt_ref[lo:lo + width, :].T.astype(BF16)

    x = x_ref[0]
    shift, scale = _ada_rows(ada_ref, 0, 2)
    h = _rms(x) * pn_ref[...] * (1.0 + scale) + shift
    hb = h.astype(BF16)

    def proj(lo, hi):
        return _dot(hb, wb_ref[:, lo:hi])

    dvt_ref[0] = proj(_COL_DV, _COL_LR).T.astype(BF16)

    c32, s32 = cos_ref[0], sin_ref[0]
    cs = jnp.concatenate([c32, s32, c32, s32], axis=1)
    lane = lax.broadcasted_iota(jnp.int32, cs.shape, 1)
    first_half = (lane % DIFF_DQK) < ROPE_HALF
    cos = jnp.where(first_half, cs, pltpu.roll(cs, ROPE_HALF, 1))
    sin = jnp.where(first_half, -pltpu.roll(cs, LANES - ROPE_HALF, 1), cs)

    def rope_store(out_ref, lo, scale):
        t = proj(lo, lo + DIFF_QK)
        for c in range(DIFF_QK // LANES):
            tc = t[:, c * LANES:(c + 1) * LANES]
            partner = jnp.where(first_half, pltpu.roll(tc, LANES - ROPE_HALF, 1),
                                pltpu.roll(tc, ROPE_HALF, 1))
            out_ref[0, :, c * LANES:(c + 1) * LANES] = ((tc * cos + partner * sin) * scale).astype(BF16)

    rope_store(dq_ref, _COL_DQ, DIFF_DQK ** -0.5 * LOG2E)
    rope_store(dk_ref, _COL_DK, 1.0)

    lr = proj(_COL_LR, _COL_END).astype(BF16)
    z = _dot(lr, gw_ref[...]) + gb_ref[...]
    log_sig = jnp.minimum(z, 0.0) - jnp.log1p(jnp.exp(-jnp.abs(z)))
    glog_ref[0] = log_sig * (LOG2E / GLA_GATE_NORM)

    gq_ref[0] = (proj(_COL_GQ, _COL_GK) * (GLA_DK ** -0.5)).astype(BF16)
    gk_ref[0] = proj(_COL_GK, _COL_GV).astype(BF16)
    gv_ref[0] = proj(_COL_GV, _COL_GOG).astype(BF16)
    gog_ref[0] = proj(_COL_GOG, _COL_DQ).astype(BF16)


def _inproj(x, ada, pre_norm, cos, sin, w_in_t, gate_w, gate_b, later_weights, tm):
    bsz, seq, d = x.shape
    grid = (bsz, seq // tm)
    row = lambda b, i: (b, i, 0)
    const2 = lambda b, i: (0, 0)

    def out(width, dtype):
        return (pl.BlockSpec((1, tm, width), row), jax.ShapeDtypeStruct((bsz, seq, width), dtype))

    dvt = (pl.BlockSpec((1, DIFF_WIDTH, tm), lambda b, i: (b, 0, i)),
           jax.ShapeDtypeStruct((bsz, DIFF_WIDTH, seq), BF16))
    outs = [out(GLA_QK, BF16), out(GLA_QK, BF16), out(GLA_WIDTH, BF16), out(GLA_WIDTH, BF16),
            out(GLA_QK, F32), out(DIFF_QK, BF16), out(DIFF_QK, BF16), dvt]

    steps = grid[0] * grid[1]
    slab = lambda b, i: (b * grid[1] + i, 0)
    cast_in, cast_out = [], []
    for w in later_weights:
        rows, cols = w.shape
        assert rows % (steps * 16) == 0, (rows, steps)
        cast_in.append(pl.BlockSpec((rows // steps, cols), slab))
        cast_out.append((pl.BlockSpec((rows // steps, cols), slab),
                         jax.ShapeDtypeStruct((rows, cols), BF16)))
    res = pl.pallas_call(
        functools.partial(_inproj_kernel, n_cast=len(later_weights)),
        grid=grid,
        in_specs=[
            pl.BlockSpec((1, tm, d), row),
            pl.BlockSpec((bsz, N_ADA * d), const2),
            pl.BlockSpec((1, d), const2),
            pl.BlockSpec((1, tm, ROPE_HALF), row),
            pl.BlockSpec((1, tm, ROPE_HALF), row),
            pl.BlockSpec((_SRC_END, d), const2, pipeline_mode=pl.Buffered(1)),
            pl.BlockSpec((GATE_PAD, GLA_QK), const2),
            pl.BlockSpec((1, GLA_QK), const2),
        ] + cast_in,
        out_specs=[o[0] for o in outs + cast_out],
        out_shape=[o[1] for o in outs + cast_out],
        scratch_shapes=[pltpu.VMEM((d, _COL_END), BF16)],
        compiler_params=pltpu.CompilerParams(
            dimension_semantics=("arbitrary", "arbitrary"), vmem_limit_bytes=VMEM_LIMIT),
        name="in_proj",
    )(x, ada, pre_norm, cos, sin, w_in_t, gate_w, gate_b, *later_weights)
    return res[:len(outs)], res[len(outs):]


def _head_stack(t, lane_head):
    return jnp.concatenate(
        [jnp.where(lane_head == h, t, jnp.zeros_like(t)) for h in range(GLA_HEADS)], axis=0)


def _gla_kernel(q_ref, k_ref, v_ref, g_ref, og_ref, gn_ref, o_ref, state_ref, *, chunks):
    C = GLA_CHUNK

    @pl.when(pl.program_id(1) == 0)
    def _():
        state_ref[...] = jnp.zeros_like(state_ref)

    row = lax.broadcasted_iota(jnp.int32, (C, C), 0)
    col = lax.broadcasted_iota(jnp.int32, (C, C), 1)
    cum_mat = (row >= col).astype(BF16)
    lane_head = lax.broadcasted_iota(jnp.int32, (C, GLA_QK), 1) // GLA_DK
    key = lax.broadcasted_iota(jnp.int32, (C, GLA_HEADS * C), 1) % C
    causal = lax.broadcasted_iota(jnp.int32, (C, GLA_HEADS * C), 0) >= key
    gn = gn_ref[...]
    zeros_v = jnp.zeros((C, GLA_DV), BF16)

    chunk_rows = [slice(c * C, (c + 1) * C) for c in range(chunks)]

    cum = []
    for rows in chunk_rows:
        g = g_ref[0, rows, :]
        g_hi = g.astype(BF16)
        g_lo = (g - g_hi.astype(F32)).astype(BF16)
        cum.append(_dot(cum_mat, g_hi) + _dot(cum_mat, g_lo))

    q_ins, scores, upds, decays = [], [], [], []
    for rows, b in zip(chunk_rows, cum):
        b_last = b[C - 1:C, :]
        b_mid = b[C // 2 - 1:C // 2, :]
        q = q_ref[0, rows, :].astype(F32)
        k = k_ref[0, rows, :].astype(F32)
        q_ins.append((q * jnp.exp2(b)).astype(BF16))
        q_mid = (q * jnp.exp2(b - b_mid)).astype(BF16)
        k_mid = (k * jnp.exp2(b_mid - b)).astype(BF16)
        k_out = (k * jnp.exp2(b_last - b)).astype(BF16)
        s = _dot_nt(q_mid, _head_stack(k_mid, lane_head))
        scores.append(jnp.where(causal, s, 0.0).astype(BF16))
        v_stack = jnp.concatenate(
            [v_ref[0, rows, h * GLA_DV:(h + 1) * GLA_DV] for h in range(GLA_HEADS)], axis=0)
        upds.append(_dot_tn(v_stack, _head_stack(k_out, lane_head)))
        decays.append(jnp.exp2(b_last))

    state = state_ref[...]
    zeros_s = jnp.zeros((GLA_DK, GLA_DV), BF16)
    bd_states = []
    for upd, decay in zip(upds, decays):
        s_nat = state.T.astype(BF16)
        bd_states.append(jnp.concatenate(
            [jnp.concatenate([s_nat[h * GLA_DK:(h + 1) * GLA_DK] if hh == h else zeros_s
                              for hh in range(GLA_HEADS)], axis=1)
             for h in range(GLA_HEADS)], axis=0))
        state = state * decay + upd
    state_ref[...] = state

    for rows, q_in, s, bd_state in zip(chunk_rows, q_ins, scores, bd_states):
        bd_v = jnp.concatenate(
            [jnp.concatenate([v_ref[0, rows, h * GLA_DV:(h + 1) * GLA_DV] if hh == h else zeros_v
                              for hh in range(GLA_HEADS)], axis=1)
             for h in range(GLA_HEADS)], axis=0)
        o_all = _dot(q_in, bd_state) + _dot(s, bd_v)
        for h in range(GLA_HEADS):
            hv = slice(h * GLA_DV, (h + 1) * GLA_DV)
            og = og_ref[0, rows, hv].astype(F32)
            o_ref[0, rows, hv] = (_rms(o_all[:, hv]) * gn * _silu(og)).astype(o_ref.dtype)


def _gla(gq, gk, gv, glog, gog, gla_norm, ts):
    bsz, seq, _ = gq.shape
    row = lambda b, i: (b, i, 0)
    return pl.pallas_call(
        functools.partial(_gla_kernel, chunks=ts // GLA_CHUNK),
        grid=(bsz, seq // ts),
        in_specs=[
            pl.BlockSpec((1, ts, GLA_QK), row),
            pl.BlockSpec((1, ts, GLA_QK), row),
            pl.BlockSpec((1, ts, GLA_WIDTH), row),
            pl.BlockSpec((1, ts, GLA_QK), row),
            pl.BlockSpec((1, ts, GLA_WIDTH), row),
            pl.BlockSpec((1, GLA_DV), lambda b, i: (0, 0)),
        ],
        out_specs=pl.BlockSpec((1, ts, GLA_WIDTH), row),
        out_shape=jax.ShapeDtypeStruct((bsz, seq, GLA_WIDTH), BF16),
        scratch_shapes=[pltpu.VMEM((GLA_DV, GLA_QK), F32)],
        compiler_params=pltpu.CompilerParams(
            dimension_semantics=("parallel", "arbitrary"), vmem_limit_bytes=VMEM_LIMIT),
        name="gla",
    )(gq, gk, gv, glog, gog, gla_norm)


def _diff_kernel(q_ref, k_ref, vt_ref, lq1_ref, lk1_ref, lq2_ref, lk2_ref, dn_ref, o_ref,
                 acc_ref, s_ref, *, tq, heads, q_blocks, lambda_init):
    step = pl.program_id(2)
    lane = lax.broadcasted_iota(jnp.int32, (tq, 2 * DIFF_DQK), 1)
    ones_rows = jnp.ones((_SUM_ROWS, tq), BF16)

    def head_cols(h, width):
        return slice(h * width, (h + 1) * width)

    def q_rows(a):
        return slice(a * tq, (a + 1) * tq)

    qs = []
    for a in range(q_blocks):
        qs.append([])
        for h in range(heads):
            q = q_ref[0, q_rows(a), head_cols(h, 2 * DIFF_DQK)]
            zero = jnp.zeros_like(q)
            qs[a].append(jnp.concatenate([jnp.where(lane < DIFF_DQK, q, zero),
                                          jnp.where(lane >= DIFF_DQK, q, zero)], axis=0))

    def score(slot, a, j):
        for h in range(heads):
            kb = k_ref[0, j * tq:(j + 1) * tq, head_cols(h, 2 * DIFF_DQK)]
            s_ref[slot, a, h] = _dot_nt(kb, qs[a][h])

    def consume(slot, a, j, ms, masked):
        out = []
        for h in range(heads):
            vtb = vt_ref[0, head_cols(h, DIFF_DV), j * tq:(j + 1) * tq]
            m_new, p = [], []
            for c in range(2 * tq // LANES):
                cols = slice(c * LANES, (c + 1) * LANES)
                st = s_ref[slot, a, h, :, cols]
                if masked:
                    key = lax.broadcasted_iota(jnp.int32, st.shape, 0)
                    qry = lax.broadcasted_iota(jnp.int32, st.shape, 1) + (c * LANES) % tq
                    st = jnp.where(key <= qry, st, -jnp.inf)
                mc = jnp.max(st, axis=0, keepdims=True)
                if ms is not None:
                    mc = jnp.maximum(ms[h][:, cols], mc)
                p.append(jnp.exp2(st - mc).astype(BF16))
                m_new.append(mc)
            m_new = jnp.concatenate(m_new, axis=1)
            v_aug = jnp.concatenate([vtb, ones_rows], axis=0)
            pv = _dot(v_aug, jnp.concatenate(p, axis=1))
            if ms is None:
                acc_ref[a, h] = pv
            else:
                acc_ref[a, h] = jnp.exp2(ms[h] - m_new) * acc_ref[a, h] + pv
            out.append(m_new)
        return tuple(out)

    def finish(a):
        lam = (jnp.exp(jnp.sum(lq1_ref[...] * lk1_ref[...], axis=-1, keepdims=True))
               - jnp.exp(jnp.sum(lq2_ref[...] * lk2_ref[...], axis=-1, keepdims=True))
               + lambda_init)
        for h in range(heads):
            acc = acc_ref[a, h]
            o_all = acc[:DIFF_DV] * (1.0 / acc[DIFF_DV:DIFF_DV + 1])
            ot = o_all[:, :tq] - lam * o_all[:, tq:]
            ot = ot * lax.rsqrt(jnp.mean(ot * ot, axis=0, keepdims=True) + EPS)
            o_ref[0, q_rows(a), head_cols(h, DIFF_DV)] = (
                ot.T * dn_ref[...] * (1.0 - lambda_init)).astype(o_ref.dtype)

    def run(s):
        last = [s * q_blocks + a for a in range(q_blocks)]
        ms = [None] * q_blocks
        for a in range(q_blocks):
            score(0, a, 0)
        for j in range(max(last) + 1):
            for a in range(q_blocks):
                if j > last[a]:
                    continue
                if j < last[a]:
                    score((j + 1) % 2, a, j + 1)
                ms[a] = consume(j % 2, a, j, ms[a], j == last[a])
                if j == last[a]:
                    finish(a)

    for s in range(k_ref.shape[1] // (tq * q_blocks)):
        pl.when(step == s)(functools.partial(run, s))


def _diff_attn(dq, dk, dvt, lq1, lk1, lq2, lk2, diff_norm, lambda_init, tq, heads, q_blocks):
    bsz, seq, _ = dq.shape
    vec = lambda n: pl.BlockSpec((1, n), lambda b, g, i: (0, 0))
    tqs = tq * q_blocks
    return pl.pallas_call(
        functools.partial(_diff_kernel, tq=tq, heads=heads, q_blocks=q_blocks,
                          lambda_init=lambda_init),
        grid=(bsz, DIFF_HEADS // heads, seq // tqs),
        in_specs=[
            pl.BlockSpec((1, tqs, heads * 2 * DIFF_DQK), lambda b, g, i: (b, i, g)),
            pl.BlockSpec((1, seq, heads * 2 * DIFF_DQK), lambda b, g, i: (b, 0, g)),
            pl.BlockSpec((1, heads * DIFF_DV, seq), lambda b, g, i: (b, g, 0)),
            vec(DIFF_DQK), vec(DIFF_DQK), vec(DIFF_DQK), vec(DIFF_DQK), vec(DIFF_DV),
        ],
        out_specs=pl.BlockSpec((1, tqs, heads * DIFF_DV), lambda b, g, i: (b, i, g)),
        out_shape=jax.ShapeDtypeStruct((bsz, seq, DIFF_WIDTH), BF16),
        scratch_shapes=[pltpu.VMEM((q_blocks, heads, DIFF_DV + _SUM_ROWS, 2 * tq), F32),
                        pltpu.VMEM((2, q_blocks, heads, tq, 2 * tq), F32)],
        compiler_params=pltpu.CompilerParams(
            dimension_semantics=("parallel", "parallel", "parallel"),
            vmem_limit_bytes=VMEM_LIMIT),
        name="diff_attn",
    )(dq, dk, dvt, lq1, lk1, lq2, lk2, diff_norm)


def _out_mlp_kernel(x_ref, go_ref, do_ref, ada_ref, pn_mix_ref, pre_mlp_ref, pn_mlp_ref,
                    wo_ref, wu_ref, wd_ref, o_ref, u_ref, *, ff_chunk, row_parts):
    tm = x_ref.shape[1]
    parts = [slice(p * tm // row_parts, (p + 1) * tm // row_parts) for p in range(row_parts)]
    gt_a, sh_m, sc_m, gt_m = _ada_rows(ada_ref, 2, 4)
    ys = [_dot(go_ref[0, r, :], wo_ref[:GLA_WIDTH, :]) + _dot(do_ref[0, r, :], wo_ref[GLA_WIDTH:, :])
          for r in parts]
    x1s = [x_ref[0, r, :] + gt_a * (_rms(y) * pn_mix_ref[...]) for r, y in zip(parts, ys)]
    hs = [(_rms(x1) * pre_mlp_ref[...] * (1.0 + sc_m) + sh_m).astype(BF16) for x1 in x1s]
    d_ff = wu_ref.shape[1]
    for f in range(d_ff // ff_chunk):
        cols = slice(f * ff_chunk, (f + 1) * ff_chunk)
        for r, h in zip(parts, hs):
            u = jnp.maximum(_dot(h, wu_ref[:, cols]), 0.0)
            u_ref[r, cols] = (u * u).astype(BF16)
    y2s = [_dot(u_ref[r, :], wd_ref[...]) for r in parts]
    for r, x1, y2 in zip(parts, x1s, y2s):
        o_ref[0, r, :] = x1 + gt_m * (_rms(y2) * pn_mlp_ref[...])


def _out_mlp(x, go, do, ada, post_mix, pre_mlp, post_mlp, w_out, w_up, w_down, tm, ff_chunk):
    bsz, seq, d = x.shape
    d_ff = w_up.shape[1]
    row = lambda b, i: (b, i, 0)
    const2 = lambda b, i: (0, 0)
    resident = functools.partial(pl.BlockSpec, index_map=const2, pipeline_mode=pl.Buffered(1))
    return pl.pallas_call(
        functools.partial(_out_mlp_kernel, ff_chunk=ff_chunk, row_parts=tm // 256),
        grid=(bsz, seq // tm),
        in_specs=[
            pl.BlockSpec((1, tm, d), row),
            pl.BlockSpec((1, tm, GLA_WIDTH), row),
            pl.BlockSpec((1, tm, DIFF_WIDTH), row),
            pl.BlockSpec((bsz, N_ADA * d), const2),
            pl.BlockSpec((1, d), const2),
            pl.BlockSpec((1, d), const2),
            pl.BlockSpec((1, d), const2),
            resident((d, d)),
            resident((d, d_ff)),
            resident((d_ff, d)),
        ],
        out_specs=pl.BlockSpec((1, tm, d), row),
        out_shape=jax.ShapeDtypeStruct((bsz, seq, d), F32),
        scratch_shapes=[pltpu.VMEM((tm, d_ff), BF16)],
        compiler_params=pltpu.CompilerParams(
            dimension_semantics=("parallel", "parallel"), vmem_limit_bytes=VMEM_LIMIT),
        name="out_mlp",
    )(x, go, do, ada, post_mix, pre_mlp, post_mlp, w_out, w_up, w_down)


def kernel(x, c, positions, ada_w, ada_b, pre_norm_mix, post_norm_mix, w_in, gla_gate_w, gla_gate_b, gla_norm, lambda_q1, lambda_k1, lambda_q2, lambda_k2, diff_norm, w_out, pre_norm_mlp, post_norm_mlp, w_up, w_down):
    depth = ada_w.shape[0]
    bsz, seq, d = x.shape
    vec = lambda t: t.reshape(1, -1)
    for l in range(depth):
        lambda_init = 0.8 - 0.6 * math.exp(-0.3 * l)
        ada, cos, sin = _ada_rope(c, ada_w[l], ada_b[l], positions)
        gate_w = jnp.pad(gla_gate_w[l], ((0, GATE_PAD - GLA_GATE_RANK), (0, 0))).astype(BF16)
        (gq, gk, gv, gog, glog, dq, dk, dvt), (w_out_b, w_up_b, w_down_b) = _inproj(
            x, ada, vec(pre_norm_mix[l]), cos, sin, w_in[l].T, gate_w,
            vec(gla_gate_b[l]), (w_out[l], w_up[l], w_down[l]), tm=1024)
        go = _gla(gq, gk, gv, glog, gog, vec(gla_norm[l]), ts=1024)
        do = _diff_attn(dq, dk, dvt, vec(lambda_q1[l]), vec(lambda_k1[l]), vec(lambda_q2[l]),
                        vec(lambda_k2[l]), vec(diff_norm[l]), lambda_init, tq=256, heads=2, q_blocks=4)
        x = _out_mlp(x, go, do, ada, vec(post_norm_mix[l]), vec(pre_norm_mlp[l]),
                     vec(post_norm_mlp[l]), w_out_b, w_up_b, w_down_b, tm=512, ff_chunk=1024)
    return x
```

```python
import functools
import math

import jax
import jax.numpy as jnp
from jax import lax
from jax.experimental import pallas as pl
from jax.experimental.pallas import tpu as pltpu

F32 = jnp.float32
BF16 = jnp.bfloat16

GLA_HEADS = 4
GLA_DK = 64
GLA_DV = 128
GLA_QK = GLA_HEADS * GLA_DK
GLA_WIDTH = GLA_HEADS * GLA_DV
GLA_GATE_RANK = 16
GLA_GATE_NORM = 16.0
GLA_CHUNK = 64
DIFF_HEADS = 4
DIFF_DQK = 64
DIFF_DV = 128
DIFF_QK = DIFF_HEADS * 2 * DIFF_DQK
DIFF_WIDTH = DIFF_HEADS * DIFF_DV
ROPE_THETA = 10000.0
EPS = 1e-6
N_ADA = 6

LANES = 128
ROPE_HALF = DIFF_DQK // 2
_SUM_ROWS = 16
LOG2E = math.log2(math.e)
VMEM_LIMIT = 56 * 1024 * 1024

_COL_GQ = 0
_COL_GK = _COL_GQ + GLA_QK
_COL_GV = _COL_GK + GLA_QK
_COL_LR = _COL_GV + GLA_WIDTH
_COL_GOG = _COL_LR + GLA_GATE_RANK
_COL_DQ = _COL_GOG + GLA_WIDTH
_COL_DK = _COL_DQ + DIFF_QK
_COL_DV = _COL_DK + DIFF_QK
_COL_END = _COL_DV + DIFF_WIDTH


def _dot(a, b):
    return jnp.dot(a, b, preferred_element_type=F32)


def _dot_nt(a, b):
    return lax.dot_general(a, b, (((1,), (1,)), ((), ())), preferred_element_type=F32)


def _dot_tn(a, b):
    return lax.dot_general(a, b, (((0,), (0,)), ((), ())), preferred_element_type=F32)


def _rms(t):
    return t * lax.rsqrt(jnp.mean(t * t, axis=-1, keepdims=True) + EPS)


def _silu(t):
    return t * (1.0 / (1.0 + jnp.exp(-t)))


def _ada_kernel(c_ref, w_ref, b_ref, pos_ref, freq_ref, o_ref, cos_ref, sin_ref):
    ca = _silu(c_ref[...]).astype(BF16)
    o_ref[...] = _dot(ca, w_ref[...].astype(BF16)) + b_ref[...]
    ang = pos_ref[...] * freq_ref[...]
    tr = ang.shape[0]
    for table_ref, t in ((cos_ref, jnp.cos(ang)), (sin_ref, jnp.sin(ang))):
        for g in range(LANES // ROPE_HALF):
            tg = t if g == 0 else pltpu.roll(t, LANES - g * ROPE_HALF, 1)
            table_ref[g * tr:(g + 1) * tr, :] = tg[:, :ROPE_HALF]


def _ada_rope(c, ada_w, ada_b, positions, steps=4):
    bsz, d = c.shape
    n = ada_w.shape[1]
    tn = n // steps
    per_row = LANES // ROPE_HALF
    inv_freq = 1.0 / (ROPE_THETA ** (jnp.arange(0, DIFF_DQK, 2, dtype=F32) / DIFF_DQK))
    freq = jnp.tile(inv_freq, per_row).reshape(1, LANES)
    tokens = positions.size
    tr = tokens // (steps * per_row)
    pos = positions.astype(F32).reshape(steps, per_row, tr).transpose(0, 2, 1)
    pos = jnp.repeat(pos.reshape(steps * tr, per_row), ROPE_HALF, axis=1)
    dense = pl.BlockSpec((tr, LANES), lambda j: (j, 0))
    table = (pl.BlockSpec((tr * per_row, ROPE_HALF), lambda j: (j, 0)),
             jax.ShapeDtypeStruct((tokens, ROPE_HALF), F32))
    ada, cos, sin = pl.pallas_call(
        _ada_kernel,
        grid=(steps,),
        in_specs=[
            pl.BlockSpec((bsz, d), lambda j: (0, 0)),
            pl.BlockSpec((d, tn), lambda j: (0, j)),
            pl.BlockSpec((1, tn), lambda j: (0, j)),
            dense,
            pl.BlockSpec((1, LANES), lambda j: (0, 0)),
        ],
        out_specs=[pl.BlockSpec((bsz, tn), lambda j: (0, j)), table[0], table[0]],
        out_shape=[jax.ShapeDtypeStruct((bsz, n), F32), table[1], table[1]],
        compiler_params=pltpu.CompilerParams(
            dimension_semantics=("parallel",), vmem_limit_bytes=VMEM_LIMIT),
        name="ada_ln",
    )(c, ada_w, ada_b.reshape(1, n), pos, freq)
    shape = positions.shape + (ROPE_HALF,)
    return ada, cos.reshape(shape), sin.reshape(shape)


def _ada_rows(ada_ref, first, count):
    b = pl.program_id(0)
    d = ada_ref.shape[1] // N_ADA
    return [ada_ref[pl.ds(b, 1), n * d:(n + 1) * d] for n in range(first, first + count)]


def _inproj_kernel(x_ref, ada_ref, pn_ref, cos_ref, sin_ref, wt_ref, gw_ref, gb_ref, *rest, n_cast):
    cast_src, rest = rest[:n_cast], rest[n_cast:]
    gq_ref, gk_ref, gv_ref, gog_ref, glog_ref, dq_ref, dk_ref, dvt_ref = rest[:8]
    wb_ref = rest[-1]
    for src, dst in zip(cast_src, rest[8:8 + n_cast]):
        dst[...] = src[...].astype(dst.dtype)

    @pl.when((pl.program_id(0) == 0) & (pl.program_id(1) == 0))
    def _():
        wb_ref[...] = wt_ref[...].astype(BF16)

    x = x_ref[0]
    shift, scale = _ada_rows(ada_ref, 0, 2)
    h = _rms(x) * pn_ref[...] * (1.0 + scale) + shift
    hb = h.astype(BF16)

    def proj(lo, hi):
        return _dot_nt(hb, wb_ref[lo:hi, :])

    dvt_ref[0] = proj(_COL_DV, _COL_END).T.astype(BF16)

    c32, s32 = cos_ref[0], sin_ref[0]
    cs = jnp.concatenate([c32, s32, c32, s32], axis=1)
    lane = lax.broadcasted_iota(jnp.int32, cs.shape, 1)
    first_half = (lane % DIFF_DQK) < ROPE_HALF
    cos = jnp.where(first_half, cs, pltpu.roll(cs, ROPE_HALF, 1))
    sin = jnp.where(first_half, -pltpu.roll(cs, LANES - ROPE_HALF, 1), cs)

    def rope_store(out_ref, lo, scale):
        t = proj(lo, lo + DIFF_QK)
        for c in range(DIFF_QK // LANES):
            tc = t[:, c * LANES:(c + 1) * LANES]
            partner = jnp.where(first_half, pltpu.roll(tc, LANES - ROPE_HALF, 1),
                                pltpu.roll(tc, ROPE_HALF, 1))
            out_ref[0, :, c * LANES:(c + 1) * LANES] = ((tc * cos + partner * sin) * scale).astype(BF16)

    rope_store(dq_ref, _COL_DQ, DIFF_DQK ** -0.5 * LOG2E)
    rope_store(dk_ref, _COL_DK, 1.0)

    lr = proj(_COL_LR, _COL_GOG).astype(BF16)
    z = _dot(lr, gw_ref[...]) + gb_ref[...]
    log_sig = jnp.minimum(z, 0.0) - jnp.log1p(jnp.exp(-jnp.abs(z)))
    glog_ref[0] = log_sig * (LOG2E / GLA_GATE_NORM)

    gq_ref[0] = (proj(_COL_GQ, _COL_GK) * (GLA_DK ** -0.5)).astype(BF16)
    gk_ref[0] = proj(_COL_GK, _COL_GV).astype(BF16)
    gv_ref[0] = proj(_COL_GV, _COL_LR).astype(BF16)
    gog_ref[0] = proj(_COL_GOG, _COL_DQ).astype(BF16)


def _inproj(x, ada, pre_norm, cos, sin, w_in_t, gate_w, gate_b, later_weights, tm):
    bsz, seq, d = x.shape
    grid = (bsz, seq // tm)
    row = lambda b, i: (b, i, 0)
    const2 = lambda b, i: (0, 0)

    def out(width, dtype):
        return (pl.BlockSpec((1, tm, width), row), jax.ShapeDtypeStruct((bsz, seq, width), dtype))

    dvt = (pl.BlockSpec((1, DIFF_WIDTH, tm), lambda b, i: (b, 0, i)),
           jax.ShapeDtypeStruct((bsz, DIFF_WIDTH, seq), BF16))
    outs = [out(GLA_QK, BF16), out(GLA_QK, BF16), out(GLA_WIDTH, BF16), out(GLA_WIDTH, BF16),
            out(GLA_QK, F32), out(DIFF_QK, BF16), out(DIFF_QK, BF16), dvt]

    steps = grid[0] * grid[1]
    slab = lambda b, i: (b * grid[1] + i, 0)
    cast_in, cast_out = [], []
    for w in later_weights:
        rows, cols = w.shape
        assert rows % (steps * 16) == 0, (rows, steps)
        cast_in.append(pl.BlockSpec((rows // steps, cols), slab))
        cast_out.append((pl.BlockSpec((rows // steps, cols), slab),
                         jax.ShapeDtypeStruct((rows, cols), BF16)))
    res = pl.pallas_call(
        functools.partial(_inproj_kernel, n_cast=len(later_weights)),
        grid=grid,
        in_specs=[
            pl.BlockSpec((1, tm, d), row),
            pl.BlockSpec((bsz, N_ADA * d), const2),
            pl.BlockSpec((1, d), const2),
            pl.BlockSpec((1, tm, ROPE_HALF), row),
            pl.BlockSpec((1, tm, ROPE_HALF), row),
            pl.BlockSpec((_COL_END, d), const2, pipeline_mode=pl.Buffered(1)),
            pl.BlockSpec((GLA_GATE_RANK, GLA_QK), const2),
            pl.BlockSpec((1, GLA_QK), const2),
        ] + cast_in,
        out_specs=[o[0] for o in outs + cast_out],
        out_shape=[o[1] for o in outs + cast_out],
        scratch_shapes=[pltpu.VMEM((_COL_END, d), BF16)],
        compiler_params=pltpu.CompilerParams(
            dimension_semantics=("arbitrary", "arbitrary"), vmem_limit_bytes=VMEM_LIMIT),
        name="in_proj",
    )(x, ada, pre_norm, cos, sin, w_in_t, gate_w, gate_b, *later_weights)
    return res[:len(outs)], res[len(outs):]


def _head_stack(t, lane_head):
    return jnp.concatenate(
        [jnp.where(lane_head == h, t, jnp.zeros_like(t)) for h in range(GLA_HEADS)], axis=0)


def _gla_kernel(q_ref, k_ref, v_ref, g_ref, og_ref, gn_ref, o_ref, state_ref, *, chunks):
    C = GLA_CHUNK

    @pl.when(pl.program_id(1) == 0)
    def _():
        state_ref[...] = jnp.zeros_like(state_ref)

    row = lax.broadcasted_iota(jnp.int32, (C, C), 0)
    col = lax.broadcasted_iota(jnp.int32, (C, C), 1)
    cum_mat = (row >= col).astype(BF16)
    lane_head = lax.broadcasted_iota(jnp.int32, (C, GLA_QK), 1) // GLA_DK
    key = lax.broadcasted_iota(jnp.int32, (C, GLA_HEADS * C), 1) % C
    causal = lax.broadcasted_iota(jnp.int32, (C, GLA_HEADS * C), 0) >= key
    gn = gn_ref[...]
    zeros_v = jnp.zeros((C, GLA_DV), BF16)

    chunk_rows = [slice(c * C, (c + 1) * C) for c in range(chunks)]

    cum = []
    for rows in chunk_rows:
        g = g_ref[0, rows, :]
        g_hi = g.astype(BF16)
        g_lo = (g - g_hi.astype(F32)).astype(BF16)
        cum.append(_dot(cum_mat, g_hi) + _dot(cum_mat, g_lo))

    q_ins, scores, upds, decays = [], [], [], []
    for rows, b in zip(chunk_rows, cum):
        b_last = b[C - 1:C, :]
        b_mid = b[C // 2 - 1:C // 2, :]
        q = q_ref[0, rows, :].astype(F32)
        k = k_ref[0, rows, :].astype(F32)
        q_ins.append((q * jnp.exp2(b)).astype(BF16))
        q_mid = (q * jnp.exp2(b - b_mid)).astype(BF16)
        k_mid = (k * jnp.exp2(b_mid - b)).astype(BF16)
        k_out = (k * jnp.exp2(b_last - b)).astype(BF16)
        s = _dot_nt(q_mid, _head_stack(k_mid, lane_head))
        scores.append(jnp.where(causal, s, 0.0).astype(BF16))
        v_stack = jnp.concatenate(
            [v_ref[0, rows, h * GLA_DV:(h + 1) * GLA_DV] for h in range(GLA_HEADS)], axis=0)
        upds.append(_dot_tn(v_stack, _head_stack(k_out, lane_head)))
        decays.append(jnp.exp2(b_last))

    state = state_ref[...]
    zeros_s = jnp.zeros((GLA_DK, GLA_DV), BF16)
    bd_states = []
    for upd, decay in zip(upds, decays):
        s_nat = state.T.astype(BF16)
        bd_states.append(jnp.concatenate(
            [jnp.concatenate([s_nat[h * GLA_DK:(h + 1) * GLA_DK] if hh == h else zeros_s
                              for hh in range(GLA_HEADS)], axis=1)
             for h in range(GLA_HEADS)], axis=0))
        state = state * decay + upd
    state_ref[...] = state

    for rows, q_in, s, bd_state in zip(chunk_rows, q_ins, scores, bd_states):
        bd_v = jnp.concatenate(
            [jnp.concatenate([v_ref[0, rows, h * GLA_DV:(h + 1) * GLA_DV] if hh == h else zeros_v
                              for hh in range(GLA_HEADS)], axis=1)
             for h in range(GLA_HEADS)], axis=0)
        o_all = _dot(q_in, bd_state) + _dot(s, bd_v)
        for h in range(GLA_HEADS):
            hv = slice(h * GLA_DV, (h + 1) * GLA_DV)
            og = og_ref[0, rows, hv].astype(F32)
            o_ref[0, rows, hv] = (_rms(o_all[:, hv]) * gn * _silu(og)).astype(o_ref.dtype)


def _gla(gq, gk, gv, glog, gog, gla_norm, ts):
    bsz, seq, _ = gq.shape
    row = lambda b, i: (b, i, 0)
    return pl.pallas_call(
        functools.partial(_gla_kernel, chunks=ts // GLA_CHUNK),
        grid=(bsz, seq // ts),
        in_specs=[
            pl.BlockSpec((1, ts, GLA_QK), row),
            pl.BlockSpec((1, ts, GLA_QK), row),
            pl.BlockSpec((1, ts, GLA_WIDTH), row),
            pl.BlockSpec((1, ts, GLA_QK), row),
            pl.BlockSpec((1, ts, GLA_WIDTH), row),
            pl.BlockSpec((1, GLA_DV), lambda b, i: (0, 0)),
        ],
        out_specs=pl.BlockSpec((1, ts, GLA_WIDTH), row),
        out_shape=jax.ShapeDtypeStruct((bsz, seq, GLA_WIDTH), BF16),
        scratch_shapes=[pltpu.VMEM((GLA_DV, GLA_QK), F32)],
        compiler_params=pltpu.CompilerParams(
            dimension_semantics=("parallel", "arbitrary"), vmem_limit_bytes=VMEM_LIMIT),
        name="gla",
    )(gq, gk, gv, glog, gog, gla_norm)


def _diff_kernel(q_ref, k_ref, vt_ref, lq1_ref, lk1_ref, lq2_ref, lk2_ref, dn_ref, o_ref,
                 acc_ref, s_ref, *, tq, heads, q_blocks, lambda_init):
    step = pl.program_id(2)
    lane = lax.broadcasted_iota(jnp.int32, (tq, 2 * DIFF_DQK), 1)
    ones_rows = jnp.ones((_SUM_ROWS, tq), BF16)

    def head_cols(h, width):
        return slice(h * width, (h + 1) * width)

    def q_rows(a):
        return slice(a * tq, (a + 1) * tq)

    qs = []
    for a in range(q_blocks):
        qs.append([])
        for h in range(heads):
            q = q_ref[0, q_rows(a), head_cols(h, 2 * DIFF_DQK)]
            zero = jnp.zeros_like(q)
            qs[a].append(jnp.concatenate([jnp.where(lane < DIFF_DQK, q, zero),
                                          jnp.where(lane >= DIFF_DQK, q, zero)], axis=0))

    def score(slot, a, j):
        for h in range(heads):
            kb = k_ref[0, j * tq:(j + 1) * tq, head_cols(h, 2 * DIFF_DQK)]
            s_ref[slot, a, h] = _dot_nt(kb, qs[a][h])

    def consume(slot, a, j, ms, masked):
        out = []
        for h in range(heads):
            vtb = vt_ref[0, head_cols(h, DIFF_DV), j * tq:(j + 1) * tq]
            m_new, p = [], []
            for c in range(2 * tq // LANES):
                cols = slice(c * LANES, (c + 1) * LANES)
                st = s_ref[slot, a, h, :, cols]
                if masked:
                    key = lax.broadcasted_iota(jnp.int32, st.shape, 0)
                    qry = lax.broadcasted_iota(jnp.int32, st.shape, 1) + (c * LANES) % tq
                    st = jnp.where(key <= qry, st, -jnp.inf)
                mc = jnp.max(st, axis=0, keepdims=True)
                if ms is not None:
                    mc = jnp.maximum(ms[h][:, cols], mc)
                p.append(jnp.exp2(st - mc).astype(BF16))
                m_new.append(mc)
            m_new = jnp.concatenate(m_new, axis=1)
            v_aug = jnp.concatenate([vtb, ones_rows], axis=0)
            pv = _dot(v_aug, jnp.concatenate(p, axis=1))
            if ms is None:
                acc_ref[a, h] = pv
            else:
                acc_ref[a, h] = jnp.exp2(ms[h] - m_new) * acc_ref[a, h] + pv
            out.append(m_new)
        return tuple(out)

    def finish(a):
        lam = (jnp.exp(jnp.sum(lq1_ref[...] * lk1_ref[...], axis=-1, keepdims=True))
               - jnp.exp(jnp.sum(lq2_ref[...] * lk2_ref[...], axis=-1, keepdims=True))
               + lambda_init)
        for h in range(heads):
            acc = acc_ref[a, h]
            o_all = acc[:DIFF_DV] * (1.0 / acc[DIFF_DV:DIFF_DV + 1])
            ot = o_all[:, :tq] - lam * o_all[:, tq:]
            ot = ot * lax.rsqrt(jnp.mean(ot * ot, axis=0, keepdims=True) + EPS)
            o_ref[0, q_rows(a), head_cols(h, DIFF_DV)] = (
                ot.T * dn_ref[...] * (1.0 - lambda_init)).astype(o_ref.dtype)

    def run(s):
        last = [s * q_blocks + a for a in range(q_blocks)]
        ms = [None] * q_blocks
        for a in range(q_blocks):
            score(0, a, 0)
        for j in range(max(last) + 1):
            for a in range(q_blocks):
                if j > last[a]:
                    continue
                if j < last[a]:
                    score((j + 1) % 2, a, j + 1)
                ms[a] = consume(j % 2, a, j, ms[a], j == last[a])
                if j == last[a]:
                    finish(a)

    for s in range(k_ref.shape[1] // (tq * q_blocks)):
        pl.when(step == s)(functools.partial(run, s))


def _diff_attn(dq, dk, dvt, lq1, lk1, lq2, lk2, diff_norm, lambda_init, tq, heads, q_blocks):
    bsz, seq, _ = dq.shape
    vec = lambda n: pl.BlockSpec((1, n), lambda b, g, i: (0, 0))
    tqs = tq * q_blocks
    return pl.pallas_call(
        functools.partial(_diff_kernel, tq=tq, heads=heads, q_blocks=q_blocks,
                          lambda_init=lambda_init),
        grid=(bsz, DIFF_HEADS // heads, seq // tqs),
        in_specs=[
            pl.BlockSpec((1, tqs, heads * 2 * DIFF_DQK), lambda b, g, i: (b, i, g)),
            pl.BlockSpec((1, seq, heads * 2 * DIFF_DQK), lambda b, g, i: (b, 0, g)),
            pl.BlockSpec((1, heads * DIFF_DV, seq), lambda b, g, i: (b, g, 0)),
            vec(DIFF_DQK), vec(DIFF_DQK), vec(DIFF_DQK), vec(DIFF_DQK), vec(DIFF_DV),
        ],
        out_specs=pl.BlockSpec((1, tqs, heads * DIFF_DV), lambda b, g, i: (b, i, g)),
        out_shape=jax.ShapeDtypeStruct((bsz, seq, DIFF_WIDTH), BF16),
        scratch_shapes=[pltpu.VMEM((q_blocks, heads, DIFF_DV + _SUM_ROWS, 2 * tq), F32),
                        pltpu.VMEM((2, q_blocks, heads, tq, 2 * tq), F32)],
        compiler_params=pltpu.CompilerParams(
            dimension_semantics=("parallel", "parallel", "parallel"),
            vmem_limit_bytes=VMEM_LIMIT),
        name="diff_attn",
    )(dq, dk, dvt, lq1, lk1, lq2, lk2, diff_norm)


def _out_mlp_kernel(x_ref, go_ref, do_ref, ada_ref, pn_mix_ref, pre_mlp_ref, pn_mlp_ref,
                    wo_ref, wu_ref, wd_ref, o_ref, u_ref, *, ff_chunk, row_parts):
    tm = x_ref.shape[1]
    parts = [slice(p * tm // row_parts, (p + 1) * tm // row_parts) for p in range(row_parts)]
    gt_a, sh_m, sc_m, gt_m = _ada_rows(ada_ref, 2, 4)
    halves = [slice(p * tm // (2 * row_parts), (p + 1) * tm // (2 * row_parts))
              for p in range(2 * row_parts)]
    ys = [_dot(go_ref[0, r, :], wo_ref[:GLA_WIDTH, :]) + _dot(do_ref[0, r, :], wo_ref[GLA_WIDTH:, :])
          for r in halves]
    x1h = [x_ref[0, r, :] + gt_a * (_rms(y) * pn_mix_ref[...]) for r, y in zip(halves, ys)]
    hh = [(_rms(x1) * pre_mlp_ref[...] * (1.0 + sc_m) + sh_m).astype(BF16) for x1 in x1h]
    x1s = [jnp.concatenate(x1h[2 * p:2 * p + 2], axis=0) for p in range(row_parts)]
    hs = [jnp.concatenate(hh[2 * p:2 * p + 2], axis=0) for p in range(row_parts)]
    d_ff = wu_ref.shape[1]
    for f in range(d_ff // ff_chunk):
        cols = slice(f * ff_chunk, (f + 1) * ff_chunk)
        for r, h in zip(parts, hs):
            u = jnp.maximum(_dot(h, wu_ref[:, cols]), 0.0)
            u_ref[r, cols] = (u * u).astype(BF16)
    y2s = [_dot(u_ref[r, :], wd_ref[...]) for r in parts]
    for r, x1, y2 in zip(parts, x1s, y2s):
        o_ref[0, r, :] = x1 + gt_m * (_rms(y2) * pn_mlp_ref[...])


def _out_mlp(x, go, do, ada, post_mix, pre_mlp, post_mlp, w_out, w_up, w_down, tm, ff_chunk):
    bsz, seq, d = x.shape
    d_ff = w_up.shape[1]
    row = lambda b, i: (b, i, 0)
    const2 = lambda b, i: (0, 0)
    resident = functools.partial(pl.BlockSpec, index_map=const2, pipeline_mode=pl.Buffered(1))
    return pl.pallas_call(
        functools.partial(_out_mlp_kernel, ff_chunk=ff_chunk, row_parts=tm // 256),
        grid=(bsz, seq // tm),
        in_specs=[
            pl.BlockSpec((1, tm, d), row),
            pl.BlockSpec((1, tm, GLA_WIDTH), row),
            pl.BlockSpec((1, tm, DIFF_WIDTH), row),
            pl.BlockSpec((bsz, N_ADA * d), const2),
            pl.BlockSpec((1, d), const2),
            pl.BlockSpec((1, d), const2),
            pl.BlockSpec((1, d), const2),
            resident((d, d)),
            resident((d, d_ff)),
            resident((d_ff, d)),
        ],
        out_specs=pl.BlockSpec((1, tm, d), row),
        out_shape=jax.ShapeDtypeStruct((bsz, seq, d), F32),
        scratch_shapes=[pltpu.VMEM((tm, d_ff), BF16)],
        compiler_params=pltpu.CompilerParams(
            dimension_semantics=("parallel", "parallel"), vmem_limit_bytes=VMEM_LIMIT),
        name="out_mlp",
    )(x, go, do, ada, post_mix, pre_mlp, post_mlp, w_out, w_up, w_down)


def kernel(x, c, positions, ada_w, ada_b, pre_norm_mix, post_norm_mix, w_in, gla_gate_w, gla_gate_b, gla_norm, lambda_q1, lambda_k1, lambda_q2, lambda_k2, diff_norm, w_out, pre_norm_mlp, post_norm_mlp, w_up, w_down):
    depth = ada_w.shape[0]
    bsz, seq, d = x.shape
    vec = lambda t: t.reshape(1, -1)
    for l in range(depth):
        lambda_init = 0.8 - 0.6 * math.exp(-0.3 * l)
        ada, cos, sin = _ada_rope(c, ada_w[l], ada_b[l], positions)
        (gq, gk, gv, gog, glog, dq, dk, dvt), (w_out_b, w_up_b, w_down_b) = _inproj(
            x, ada, vec(pre_norm_mix[l]), cos, sin, w_in[l].T, gla_gate_w[l].astype(BF16),
            vec(gla_gate_b[l]), (w_out[l], w_up[l], w_down[l]), tm=1024)
        go = _gla(gq, gk, gv, glog, gog, vec(gla_norm[l]), ts=1024)
        do = _diff_attn(dq, dk, dvt, vec(lambda_q1[l]), vec(lambda_k1[l]), vec(lambda_q2[l]),
                        vec(lambda_k2[l]), vec(diff_norm[l]), lambda_init, tq=256, heads=2, q_blocks=4)
        x = _out_mlp(x, go, do, ada, vec(post_norm_mix[l]), vec(pre_norm_mlp[l]),
                     vec(post_norm_mlp[l]), w_out_b, w_up_b, w_down_b, tm=512, ff_chunk=1024)
    return x
```

```python
import functools
import math

import jax
import jax.numpy as jnp
from jax import lax
from jax.experimental import pallas as pl
from jax.experimental.pallas import tpu as pltpu

F32 = jnp.float32
BF16 = jnp.bfloat16

GLA_HEADS = 4
GLA_DK = 64
GLA_DV = 128
GLA_QK = GLA_HEADS * GLA_DK
GLA_WIDTH = GLA_HEADS * GLA_DV
GLA_GATE_RANK = 16
GLA_GATE_NORM = 16.0
GLA_CHUNK = 64
DIFF_HEADS = 4
DIFF_DQK = 64
DIFF_DV = 128
DIFF_QK = DIFF_HEADS * 2 * DIFF_DQK
DIFF_WIDTH = DIFF_HEADS * DIFF_DV
ROPE_THETA = 10000.0
EPS = 1e-6
N_ADA = 6

LANES = 128
ROPE_HALF = DIFF_DQK // 2
_SUM_ROWS = 16
LOG2E = math.log2(math.e)
VMEM_LIMIT = 56 * 1024 * 1024

_COL_GQ = 0
_COL_GK = _COL_GQ + GLA_QK
_COL_GV = _COL_GK + GLA_QK
_COL_LR = _COL_GV + GLA_WIDTH
_COL_GOG = _COL_LR + GLA_GATE_RANK
_COL_DQ = _COL_GOG + GLA_WIDTH
_COL_DK = _COL_DQ + DIFF_QK
_COL_DV = _COL_DK + DIFF_QK
_COL_END = _COL_DV + DIFF_WIDTH


def _dot(a, b):
    return jnp.dot(a, b, preferred_element_type=F32)


def _dot_nt(a, b):
    return lax.dot_general(a, b, (((1,), (1,)), ((), ())), preferred_element_type=F32)


def _dot_tn(a, b):
    return lax.dot_general(a, b, (((0,), (0,)), ((), ())), preferred_element_type=F32)


def _rms(t):
    return t * lax.rsqrt(jnp.mean(t * t, axis=-1, keepdims=True) + EPS)


def _silu(t):
    return t * (1.0 / (1.0 + jnp.exp(-t)))


def _ada_kernel(c_ref, w_ref, b_ref, pos_ref, freq_ref, o_ref, cos_ref, sin_ref):
    ca = _silu(c_ref[...]).astype(BF16)
    o_ref[...] = _dot(ca, w_ref[...].astype(BF16)) + b_ref[...]
    ang = pos_ref[...] * freq_ref[...]
    tr = ang.shape[0]
    for table_ref, t in ((cos_ref, jnp.cos(ang)), (sin_ref, jnp.sin(ang))):
        for g in range(LANES // ROPE_HALF):
            tg = t if g == 0 else pltpu.roll(t, LANES - g * ROPE_HALF, 1)
            table_ref[g * tr:(g + 1) * tr, :] = tg[:, :ROPE_HALF]


def _ada_rope(c, ada_w, ada_b, positions, steps=4):
    bsz, d = c.shape
    n = ada_w.shape[1]
    tn = n // steps
    per_row = LANES // ROPE_HALF
    inv_freq = 1.0 / (ROPE_THETA ** (jnp.arange(0, DIFF_DQK, 2, dtype=F32) / DIFF_DQK))
    freq = jnp.tile(inv_freq, per_row).reshape(1, LANES)
    tokens = positions.size
    tr = tokens // (steps * per_row)
    pos = positions.astype(F32).reshape(steps, per_row, tr).transpose(0, 2, 1)
    pos = jnp.repeat(pos.reshape(steps * tr, per_row), ROPE_HALF, axis=1)
    dense = pl.BlockSpec((tr, LANES), lambda j: (j, 0))
    table = (pl.BlockSpec((tr * per_row, ROPE_HALF), lambda j: (j, 0)),
             jax.ShapeDtypeStruct((tokens, ROPE_HALF), F32))
    ada, cos, sin = pl.pallas_call(
        _ada_kernel,
        grid=(steps,),
        in_specs=[
            pl.BlockSpec((bsz, d), lambda j: (0, 0)),
            pl.BlockSpec((d, tn), lambda j: (0, j)),
            pl.BlockSpec((1, tn), lambda j: (0, j)),
            dense,
            pl.BlockSpec((1, LANES), lambda j: (0, 0)),
        ],
        out_specs=[pl.BlockSpec((bsz, tn), lambda j: (0, j)), table[0], table[0]],
        out_shape=[jax.ShapeDtypeStruct((bsz, n), F32), table[1], table[1]],
        compiler_params=pltpu.CompilerParams(
            dimension_semantics=("parallel",), vmem_limit_bytes=VMEM_LIMIT),
        name="ada_ln",
    )(c, ada_w, ada_b.reshape(1, n), pos, freq)
    shape = positions.shape + (ROPE_HALF,)
    return ada, cos.reshape(shape), sin.reshape(shape)


def _ada_rows(ada_ref, first, count):
    b = pl.program_id(0)
    d = ada_ref.shape[1] // N_ADA
    return [ada_ref[pl.ds(b, 1), n * d:(n + 1) * d] for n in range(first, first + count)]


def _inproj_kernel(x_ref, ada_ref, pn_ref, cos_ref, sin_ref, wt_ref, gw_ref, gb_ref, *rest, n_cast):
    cast_src, rest = rest[:n_cast], rest[n_cast:]
    gq_ref, gk_ref, gv_ref, gog_ref, glog_ref, dq_ref, dk_ref, dvt_ref = rest[:8]
    wb_ref = rest[-1]
    for src, dst in zip(cast_src, rest[8:8 + n_cast]):
        dst[...] = src[...].astype(dst.dtype)

    @pl.when((pl.program_id(0) == 0) & (pl.program_id(1) == 0))
    def _():
        wb_ref[...] = wt_ref[...].astype(BF16)

    x = x_ref[0]
    shift, scale = _ada_rows(ada_ref, 0, 2)
    h = _rms(x) * pn_ref[...] * (1.0 + scale) + shift
    hb = h.astype(BF16)

    def proj(lo, hi):
        return _dot_nt(hb, wb_ref[lo:hi, :])

    dvt_ref[0] = proj(_COL_DV, _COL_END).T.astype(BF16)

    c32, s32 = cos_ref[0], sin_ref[0]
    cs = jnp.concatenate([c32, s32, c32, s32], axis=1)
    lane = lax.broadcasted_iota(jnp.int32, cs.shape, 1)
    first_half = (lane % DIFF_DQK) < ROPE_HALF
    cos = jnp.where(first_half, cs, pltpu.roll(cs, ROPE_HALF, 1))
    sin = jnp.where(first_half, -pltpu.roll(cs, LANES - ROPE_HALF, 1), cs)

    def rope_store(out_ref, lo, scale):
        t = proj(lo, lo + DIFF_QK)
        for c in range(DIFF_QK // LANES):
            tc = t[:, c * LANES:(c + 1) * LANES]
            partner = jnp.where(first_half, pltpu.roll(tc, LANES - ROPE_HALF, 1),
                                pltpu.roll(tc, ROPE_HALF, 1))
            out_ref[0, :, c * LANES:(c + 1) * LANES] = ((tc * cos + partner * sin) * scale).astype(BF16)

    rope_store(dq_ref, _COL_DQ, DIFF_DQK ** -0.5 * LOG2E)
    rope_store(dk_ref, _COL_DK, 1.0)

    lr = proj(_COL_LR, _COL_GOG).astype(BF16)
    z = _dot(lr, gw_ref[...]) + gb_ref[...]
    log_sig = jnp.minimum(z, 0.0) - jnp.log1p(jnp.exp(-jnp.abs(z)))
    glog_ref[0] = log_sig * (LOG2E / GLA_GATE_NORM)

    gq_ref[0] = (proj(_COL_GQ, _COL_GK) * (GLA_DK ** -0.5)).astype(BF16)
    gk_ref[0] = proj(_COL_GK, _COL_GV).astype(BF16)
    gv_ref[0] = proj(_COL_GV, _COL_LR).astype(BF16)
    gog_ref[0] = proj(_COL_GOG, _COL_DQ).astype(BF16)


def _inproj(x, ada, pre_norm, cos, sin, w_in_t, gate_w, gate_b, later_weights, tm):
    bsz, seq, d = x.shape
    grid = (bsz, seq // tm)
    row = lambda b, i: (b, i, 0)
    const2 = lambda b, i: (0, 0)

    def out(width, dtype):
        return (pl.BlockSpec((1, tm, width), row), jax.ShapeDtypeStruct((bsz, seq, width), dtype))

    dvt = (pl.BlockSpec((1, DIFF_WIDTH, tm), lambda b, i: (b, 0, i)),
           jax.ShapeDtypeStruct((bsz, DIFF_WIDTH, seq), BF16))
    outs = [out(GLA_QK, BF16), out(GLA_QK, BF16), out(GLA_WIDTH, BF16), out(GLA_WIDTH, BF16),
            out(GLA_QK, F32), out(DIFF_QK, BF16), out(DIFF_QK, BF16), dvt]

    steps = grid[0] * grid[1]
    slab = lambda b, i: (b * grid[1] + i, 0)
    cast_in, cast_out = [], []
    for w in later_weights:
        rows, cols = w.shape
        assert rows % (steps * 16) == 0, (rows, steps)
        cast_in.append(pl.BlockSpec((rows // steps, cols), slab))
        cast_out.append((pl.BlockSpec((rows // steps, cols), slab),
                         jax.ShapeDtypeStruct((rows, cols), BF16)))
    res = pl.pallas_call(
        functools.partial(_inproj_kernel, n_cast=len(later_weights)),
        grid=grid,
        in_specs=[
            pl.BlockSpec((1, tm, d), row),
            pl.BlockSpec((bsz, N_ADA * d), const2),
            pl.BlockSpec((1, d), const2),
            pl.BlockSpec((1, tm, ROPE_HALF), row),
            pl.BlockSpec((1, tm, ROPE_HALF), row),
            pl.BlockSpec((_COL_END, d), const2, pipeline_mode=pl.Buffered(1)),
            pl.BlockSpec((GLA_GATE_RANK, GLA_QK), const2),
            pl.BlockSpec((1, GLA_QK), const2),
        ] + cast_in,
        out_specs=[o[0] for o in outs + cast_out],
        out_shape=[o[1] for o in outs + cast_out],
        scratch_shapes=[pltpu.VMEM((_COL_END, d), BF16)],
        compiler_params=pltpu.CompilerParams(
            dimension_semantics=("arbitrary", "arbitrary"), vmem_limit_bytes=VMEM_LIMIT),
        name="in_proj",
    )(x, ada, pre_norm, cos, sin, w_in_t, gate_w, gate_b, *later_weights)
    return res[:len(outs)], res[len(outs):]


def _head_stack(t, lane_head):
    return jnp.concatenate(
        [jnp.where(lane_head == h, t, jnp.zeros_like(t)) for h in range(GLA_HEADS)], axis=0)


def _gla_kernel(q_ref, k_ref, v_ref, g_ref, og_ref, gn_ref, o_ref, state_ref, *, chunks):
    C = GLA_CHUNK

    @pl.when(pl.program_id(1) == 0)
    def _():
        state_ref[...] = jnp.zeros_like(state_ref)

    row = lax.broadcasted_iota(jnp.int32, (C, C), 0)
    col = lax.broadcasted_iota(jnp.int32, (C, C), 1)
    cum_mat = (row >= col).astype(BF16)
    lane_head = lax.broadcasted_iota(jnp.int32, (C, GLA_QK), 1) // GLA_DK
    key = lax.broadcasted_iota(jnp.int32, (C, GLA_HEADS * C), 1) % C
    causal = lax.broadcasted_iota(jnp.int32, (C, GLA_HEADS * C), 0) >= key
    gn = gn_ref[...]
    zeros_v = jnp.zeros((C, GLA_DV), BF16)

    chunk_rows = [slice(c * C, (c + 1) * C) for c in range(chunks)]

    cum = []
    for rows in chunk_rows:
        g = g_ref[0, rows, :]
        g_hi = g.astype(BF16)
        g_lo = (g - g_hi.astype(F32)).astype(BF16)
        cum.append(_dot(cum_mat, g_hi) + _dot(cum_mat, g_lo))

    q_ins, scores, upds, decays = [], [], [], []
    for rows, b in zip(chunk_rows, cum):
        b_last = b[C - 1:C, :]
        b_mid = b[C // 2 - 1:C // 2, :]
        q = q_ref[0, rows, :].astype(F32)
        k = k_ref[0, rows, :].astype(F32)
        q_ins.append((q * jnp.exp2(b)).astype(BF16))
        q_mid = (q * jnp.exp2(b - b_mid)).astype(BF16)
        k_mid = (k * jnp.exp2(b_mid - b)).astype(BF16)
        k_out = (k * jnp.exp2(b_last - b)).astype(BF16)
        s = _dot_nt(q_mid, _head_stack(k_mid, lane_head))
        scores.append(jnp.where(causal, s, 0.0).astype(BF16))
        v_stack = jnp.concatenate(
            [v_ref[0, rows, h * GLA_DV:(h + 1) * GLA_DV] for h in range(GLA_HEADS)], axis=0)
        upds.append(_dot_tn(v_stack, _head_stack(k_out, lane_head)))
        decays.append(jnp.exp2(b_last))

    state = state_ref[...]
    zeros_s = jnp.zeros((GLA_DK, GLA_DV), BF16)
    bd_states = []
    for upd, decay in zip(upds, decays):
        s_nat = state.T.astype(BF16)
        bd_states.append(jnp.concatenate(
            [jnp.concatenate([s_nat[h * GLA_DK:(h + 1) * GLA_DK] if hh == h else zeros_s
                              for hh in range(GLA_HEADS)], axis=1)
             for h in range(GLA_HEADS)], axis=0))
        state = state * decay + upd
    state_ref[...] = state

    for rows, q_in, s, bd_state in zip(chunk_rows, q_ins, scores, bd_states):
        bd_v = jnp.concatenate(
            [jnp.concatenate([v_ref[0, rows, h * GLA_DV:(h + 1) * GLA_DV] if hh == h else zeros_v
                              for hh in range(GLA_HEADS)], axis=1)
             for h in range(GLA_HEADS)], axis=0)
        o_all = _dot(q_in, bd_state) + _dot(s, bd_v)
        for h in range(GLA_HEADS):
            hv = slice(h * GLA_DV, (h + 1) * GLA_DV)
            og = og_ref[0, rows, hv].astype(F32)
            o_ref[0, rows, hv] = (_rms(o_all[:, hv]) * gn * _silu(og)).astype(o_ref.dtype)


def _gla(gq, gk, gv, glog, gog, gla_norm, ts):
    bsz, seq, _ = gq.shape
    row = lambda b, i: (b, i, 0)
    return pl.pallas_call(
        functools.partial(_gla_kernel, chunks=ts // GLA_CHUNK),
        grid=(bsz, seq // ts),
        in_specs=[
            pl.BlockSpec((1, ts, GLA_QK), row),
            pl.BlockSpec((1, ts, GLA_QK), row),
            pl.BlockSpec((1, ts, GLA_WIDTH), row),
            pl.BlockSpec((1, ts, GLA_QK), row),
            pl.BlockSpec((1, ts, GLA_WIDTH), row),
            pl.BlockSpec((1, GLA_DV), lambda b, i: (0, 0)),
        ],
        out_specs=pl.BlockSpec((1, ts, GLA_WIDTH), row),
        out_shape=jax.ShapeDtypeStruct((bsz, seq, GLA_WIDTH), BF16),
        scratch_shapes=[pltpu.VMEM((GLA_DV, GLA_QK), F32)],
        compiler_params=pltpu.CompilerParams(
            dimension_semantics=("parallel", "arbitrary"), vmem_limit_bytes=VMEM_LIMIT),
        name="gla",
    )(gq, gk, gv, glog, gog, gla_norm)


def _diff_kernel(q_ref, k_ref, vt_ref, lq1_ref, lk1_ref, lq2_ref, lk2_ref, dn_ref, o_ref,
                 acc_ref, s_ref, *, tq, heads, q_blocks, lambda_init):
    step = pl.program_id(2)
    lane = lax.broadcasted_iota(jnp.int32, (tq, 2 * DIFF_DQK), 1)
    ones_rows = jnp.ones((_SUM_ROWS, tq), BF16)

    def head_cols(h, width):
        return slice(h * width, (h + 1) * width)

    def q_rows(a):
        return slice(a * tq, (a + 1) * tq)

    qs = []
    for a in range(q_blocks):
        qs.append([])
        for h in range(heads):
            q = q_ref[0, q_rows(a), head_cols(h, 2 * DIFF_DQK)]
            zero = jnp.zeros_like(q)
            qs[a].append(jnp.concatenate([jnp.where(lane < DIFF_DQK, q, zero),
                                          jnp.where(lane >= DIFF_DQK, q, zero)], axis=0))

    def score(slot, a, j, hs=None):
        for h in (range(heads) if hs is None else hs):
            kb = k_ref[0, j * tq:(j + 1) * tq, head_cols(h, 2 * DIFF_DQK)]
            s_ref[slot, a, h] = _dot_nt(kb, qs[a][h])

    def consume(slot, a, j, ms, masked, hs=None):
        out = list(ms)
        for h in (range(heads) if hs is None else hs):
            vtb = vt_ref[0, head_cols(h, DIFF_DV), j * tq:(j + 1) * tq]
            m_new, p = [], []
            for c in range(2 * tq // LANES):
                cols = slice(c * LANES, (c + 1) * LANES)
                st = s_ref[slot, a, h, :, cols]
                if masked:
                    key = lax.broadcasted_iota(jnp.int32, st.shape, 0)
                    qry = lax.broadcasted_iota(jnp.int32, st.shape, 1) + (c * LANES) % tq
                    st = jnp.where(key <= qry, st, -jnp.inf)
                mc = jnp.max(st, axis=0, keepdims=True)
                if ms[h] is not None:
                    mc = jnp.maximum(ms[h][:, cols], mc)
                p.append(jnp.exp2(st - mc).astype(BF16))
                m_new.append(mc)
            m_new = jnp.concatenate(m_new, axis=1)
            v_aug = jnp.concatenate([vtb, ones_rows], axis=0)
            pv = _dot(v_aug, jnp.concatenate(p, axis=1))
            if ms[h] is None:
                acc_ref[a, h] = pv
            else:
                acc_ref[a, h] = jnp.exp2(ms[h] - m_new) * acc_ref[a, h] + pv
            out[h] = m_new
        return out

    def finish(a):
        lam = (jnp.exp(jnp.sum(lq1_ref[...] * lk1_ref[...], axis=-1, keepdims=True))
               - jnp.exp(jnp.sum(lq2_ref[...] * lk2_ref[...], axis=-1, keepdims=True))
               + lambda_init)
        for h in range(heads):
            acc = acc_ref[a, h]
            o_all = acc[:DIFF_DV] * (1.0 / acc[DIFF_DV:DIFF_DV + 1])
            ot = o_all[:, :tq] - lam * o_all[:, tq:]
            ot = ot * lax.rsqrt(jnp.mean(ot * ot, axis=0, keepdims=True) + EPS)
            o_ref[0, q_rows(a), head_cols(h, DIFF_DV)] = (
                ot.T * dn_ref[...] * (1.0 - lambda_init)).astype(o_ref.dtype)

    def run(s):
        last = [s * q_blocks + a for a in range(q_blocks)]
        ms = [[None] * heads for _ in range(q_blocks)]
        for a in range(q_blocks):
            score(0, a, 0)
        for j in range(max(last) + 1):
            for a in range(q_blocks):
                if j > last[a]:
                    continue
                for h in range(heads):
                    if j < last[a]:
                        score((j + 1) % 2, a, j + 1, (h,))
                    ms[a] = consume(j % 2, a, j, ms[a], j == last[a], (h,))
                if j == last[a]:
                    finish(a)

    for s in range(k_ref.shape[1] // (tq * q_blocks)):
        pl.when(step == s)(functools.partial(run, s))


def _diff_attn(dq, dk, dvt, lq1, lk1, lq2, lk2, diff_norm, lambda_init, tq, heads, q_blocks):
    bsz, seq, _ = dq.shape
    vec = lambda n: pl.BlockSpec((1, n), lambda b, g, i: (0, 0))
    tqs = tq * q_blocks
    return pl.pallas_call(
        functools.partial(_diff_kernel, tq=tq, heads=heads, q_blocks=q_blocks,
                          lambda_init=lambda_init),
        grid=(bsz, DIFF_HEADS // heads, seq // tqs),
        in_specs=[
            pl.BlockSpec((1, tqs, heads * 2 * DIFF_DQK), lambda b, g, i: (b, i, g)),
            pl.BlockSpec((1, seq, heads * 2 * DIFF_DQK), lambda b, g, i: (b, 0, g)),
            pl.BlockSpec((1, heads * DIFF_DV, seq), lambda b, g, i: (b, g, 0)),
            vec(DIFF_DQK), vec(DIFF_DQK), vec(DIFF_DQK), vec(DIFF_DQK), vec(DIFF_DV),
        ],
        out_specs=pl.BlockSpec((1, tqs, heads * DIFF_DV), lambda b, g, i: (b, i, g)),
        out_shape=jax.ShapeDtypeStruct((bsz, seq, DIFF_WIDTH), BF16),
        scratch_shapes=[pltpu.VMEM((q_blocks, heads, DIFF_DV + _SUM_ROWS, 2 * tq), F32),
                        pltpu.VMEM((2, q_blocks, heads, tq, 2 * tq), F32)],
        compiler_params=pltpu.CompilerParams(
            dimension_semantics=("parallel", "parallel", "parallel"),
            vmem_limit_bytes=VMEM_LIMIT),
        name="diff_attn",
    )(dq, dk, dvt, lq1, lk1, lq2, lk2, diff_norm)


def _out_mlp_kernel(x_ref, go_ref, do_ref, ada_ref, pn_mix_ref, pre_mlp_ref, pn_mlp_ref,
                    wo_ref, wu_ref, wd_ref, o_ref, u_ref, *, ff_chunk, row_parts):
    tm = x_ref.shape[1]
    parts = [slice(p * tm // row_parts, (p + 1) * tm // row_parts) for p in range(row_parts)]
    gt_a, sh_m, sc_m, gt_m = _ada_rows(ada_ref, 2, 4)
    halves = [slice(p * tm // (2 * row_parts), (p + 1) * tm // (2 * row_parts))
              for p in range(2 * row_parts)]
    ys = [_dot(go_ref[0, r, :], wo_ref[:GLA_WIDTH, :]) + _dot(do_ref[0, r, :], wo_ref[GLA_WIDTH:, :])
          for r in halves]
    x1h = [x_ref[0, r, :] + gt_a * (_rms(y) * pn_mix_ref[...]) for r, y in zip(halves, ys)]
    hh = [(_rms(x1) * pre_mlp_ref[...] * (1.0 + sc_m) + sh_m).astype(BF16) for x1 in x1h]
    x1s = [jnp.concatenate(x1h[2 * p:2 * p + 2], axis=0) for p in range(row_parts)]
    hs = [jnp.concatenate(hh[2 * p:2 * p + 2], axis=0) for p in range(row_parts)]
    d_ff = wu_ref.shape[1]
    for f in range(d_ff // ff_chunk):
        cols = slice(f * ff_chunk, (f + 1) * ff_chunk)
        for r, h in zip(parts, hs):
            u = jnp.maximum(_dot(h, wu_ref[:, cols]), 0.0)
            u_ref[r, cols] = (u * u).astype(BF16)
    y2s = [_dot(u_ref[r, :], wd_ref[...]) for r in parts]
    for r, x1, y2 in zip(parts, x1s, y2s):
        o_ref[0, r, :] = x1 + gt_m * (_rms(y2) * pn_mlp_ref[...])


def _out_mlp(x, go, do, ada, post_mix, pre_mlp, post_mlp, w_out, w_up, w_down, tm, ff_chunk):
    bsz, seq, d = x.shape
    d_ff = w_up.shape[1]
    row = lambda b, i: (b, i, 0)
    const2 = lambda b, i: (0, 0)
    resident = functools.partial(pl.BlockSpec, index_map=const2, pipeline_mode=pl.Buffered(1))
    return pl.pallas_call(
        functools.partial(_out_mlp_kernel, ff_chunk=ff_chunk, row_parts=tm // 256),
        grid=(bsz, seq // tm),
        in_specs=[
            pl.BlockSpec((1, tm, d), row),
            pl.BlockSpec((1, tm, GLA_WIDTH), row),
            pl.BlockSpec((1, tm, DIFF_WIDTH), row),
            pl.BlockSpec((bsz, N_ADA * d), const2),
            pl.BlockSpec((1, d), const2),
            pl.BlockSpec((1, d), const2),
            pl.BlockSpec((1, d), const2),
            resident((d, d)),
            resident((d, d_ff)),
            resident((d_ff, d)),
        ],
        out_specs=pl.BlockSpec((1, tm, d), row),
        out_shape=jax.ShapeDtypeStruct((bsz, seq, d), F32),
        scratch_shapes=[pltpu.VMEM((tm, d_ff), BF16)],
        compiler_params=pltpu.CompilerParams(
            dimension_semantics=("parallel", "parallel"), vmem_limit_bytes=VMEM_LIMIT),
        name="out_mlp",
    )(x, go, do, ada, post_mix, pre_mlp, post_mlp, w_out, w_up, w_down)


def kernel(x, c, positions, ada_w, ada_b, pre_norm_mix, post_norm_mix, w_in, gla_gate_w, gla_gate_b, gla_norm, lambda_q1, lambda_k1, lambda_q2, lambda_k2, diff_norm, w_out, pre_norm_mlp, post_norm_mlp, w_up, w_down):
    depth = ada_w.shape[0]
    bsz, seq, d = x.shape
    vec = lambda t: t.reshape(1, -1)
    for l in range(depth):
        lambda_init = 0.8 - 0.6 * math.exp(-0.3 * l)
        ada, cos, sin = _ada_rope(c, ada_w[l], ada_b[l], positions)
        (gq, gk, gv, gog, glog, dq, dk, dvt), (w_out_b, w_up_b, w_down_b) = _inproj(
            x, ada, vec(pre_norm_mix[l]), cos, sin, w_in[l].T, gla_gate_w[l].astype(BF16),
            vec(gla_gate_b[l]), (w_out[l], w_up[l], w_down[l]), tm=1024)
        go = _gla(gq, gk, gv, glog, gog, vec(gla_norm[l]), ts=1024)
        do = _diff_attn(dq, dk, dvt, vec(lambda_q1[l]), vec(lambda_k1[l]), vec(lambda_q2[l]),
                        vec(lambda_k2[l]), vec(diff_norm[l]), lambda_init, tq=256, heads=2, q_blocks=4)
        x = _out_mlp(x, go, do, ada, vec(post_norm_mix[l]), vec(pre_norm_mlp[l]),
                     vec(post_norm_mlp[l]), w_out_b, w_up_b, w_down_b, tm=512, ff_chunk=1024)
    return x
```

```python
import functools
import math

import jax
import jax.numpy as jnp
from jax import lax
from jax.experimental import pallas as pl
from jax.experimental.pallas import tpu as pltpu

F32 = jnp.float32
BF16 = jnp.bfloat16

GLA_HEADS = 4
GLA_DK = 64
GLA_DV = 128
GLA_QK = GLA_HEADS * GLA_DK
GLA_WIDTH = GLA_HEADS * GLA_DV
GLA_GATE_RANK = 16
GLA_GATE_NORM = 16.0
GLA_CHUNK = 64
DIFF_HEADS = 4
DIFF_DQK = 64
DIFF_DV = 128
DIFF_QK = DIFF_HEADS * 2 * DIFF_DQK
DIFF_WIDTH = DIFF_HEADS * DIFF_DV
ROPE_THETA = 10000.0
EPS = 1e-6
N_ADA = 6

LANES = 128
ROPE_HALF = DIFF_DQK // 2
_SUM_ROWS = 16
LOG2E = math.log2(math.e)
VMEM_LIMIT = 56 * 1024 * 1024

_COL_GQ = 0
_COL_GK = _COL_GQ + GLA_QK
_COL_GV = _COL_GK + GLA_QK
_COL_LR = _COL_GV + GLA_WIDTH
_COL_GOG = _COL_LR + GLA_GATE_RANK
_COL_DQ = _COL_GOG + GLA_WIDTH
_COL_DK = _COL_DQ + DIFF_QK
_COL_DV = _COL_DK + DIFF_QK
_COL_END = _COL_DV + DIFF_WIDTH


def _dot(a, b):
    return jnp.dot(a, b, preferred_element_type=F32)


def _dot_nt(a, b):
    return lax.dot_general(a, b, (((1,), (1,)), ((), ())), preferred_element_type=F32)


def _dot_tn(a, b):
    return lax.dot_general(a, b, (((0,), (0,)), ((), ())), preferred_element_type=F32)


def _rms(t):
    return t * lax.rsqrt(jnp.mean(t * t, axis=-1, keepdims=True) + EPS)


def _silu(t):
    return t * (1.0 / (1.0 + jnp.exp(-t)))


def _ada_kernel(c_ref, w_ref, b_ref, pos_ref, freq_ref, o_ref, cos_ref, sin_ref):
    ca = _silu(c_ref[...]).astype(BF16)
    o_ref[...] = _dot(ca, w_ref[...].astype(BF16)) + b_ref[...]
    ang = pos_ref[...] * freq_ref[...]
    tr = ang.shape[0]
    for table_ref, t in ((cos_ref, jnp.cos(ang)), (sin_ref, jnp.sin(ang))):
        for g in range(LANES // ROPE_HALF):
            tg = t if g == 0 else pltpu.roll(t, LANES - g * ROPE_HALF, 1)
            table_ref[g * tr:(g + 1) * tr, :] = tg[:, :ROPE_HALF]


def _ada_rope(c, ada_w, ada_b, positions, steps=4):
    bsz, d = c.shape
    n = ada_w.shape[1]
    tn = n // steps
    per_row = LANES // ROPE_HALF
    inv_freq = 1.0 / (ROPE_THETA ** (jnp.arange(0, DIFF_DQK, 2, dtype=F32) / DIFF_DQK))
    freq = jnp.tile(inv_freq, per_row).reshape(1, LANES)
    tokens = positions.size
    tr = tokens // (steps * per_row)
    pos = positions.astype(F32).reshape(steps, per_row, tr).transpose(0, 2, 1)
    pos = jnp.repeat(pos.reshape(steps * tr, per_row), ROPE_HALF, axis=1)
    dense = pl.BlockSpec((tr, LANES), lambda j: (j, 0))
    table = (pl.BlockSpec((tr * per_row, ROPE_HALF), lambda j: (j, 0)),
             jax.ShapeDtypeStruct((tokens, ROPE_HALF), F32))
    ada, cos, sin = pl.pallas_call(
        _ada_kernel,
        grid=(steps,),
        in_specs=[
            pl.BlockSpec((bsz, d), lambda j: (0, 0)),
            pl.BlockSpec((d, tn), lambda j: (0, j)),
            pl.BlockSpec((1, tn), lambda j: (0, j)),
            dense,
            pl.BlockSpec((1, LANES), lambda j: (0, 0)),
        ],
        out_specs=[pl.BlockSpec((bsz, tn), lambda j: (0, j)), table[0], table[0]],
        out_shape=[jax.ShapeDtypeStruct((bsz, n), F32), table[1], table[1]],
        compiler_params=pltpu.CompilerParams(
            dimension_semantics=("parallel",), vmem_limit_bytes=VMEM_LIMIT),
        name="ada_ln",
    )(c, ada_w, ada_b.reshape(1, n), pos, freq)
    shape = positions.shape + (ROPE_HALF,)
    return ada, cos.reshape(shape), sin.reshape(shape)


def _ada_rows(ada_ref, first, count):
    b = pl.program_id(0)
    d = ada_ref.shape[1] // N_ADA
    return [ada_ref[pl.ds(b, 1), n * d:(n + 1) * d] for n in range(first, first + count)]


def _inproj_kernel(x_ref, ada_ref, pn_ref, cos_ref, sin_ref, wt_ref, gw_ref, gb_ref, *rest, n_cast):
    cast_src, rest = rest[:n_cast], rest[n_cast:]
    gq_ref, gk_ref, gv_ref, gog_ref, glog_ref, dq_ref, dk_ref, dvt_ref = rest[:8]
    wb_ref = rest[-1]
    for src, dst in zip(cast_src, rest[8:8 + n_cast]):
        dst[...] = src[...].astype(dst.dtype)

    @pl.when((pl.program_id(0) == 0) & (pl.program_id(1) == 0))
    def _():
        wb_ref[...] = wt_ref[...].astype(BF16)

    x = x_ref[0]
    shift, scale = _ada_rows(ada_ref, 0, 2)
    h = _rms(x) * pn_ref[...] * (1.0 + scale) + shift
    hb = h.astype(BF16)

    def proj(lo, hi):
        return _dot_nt(hb, wb_ref[lo:hi, :])

    dvt_ref[0] = proj(_COL_DV, _COL_END).T.astype(BF16)

    c32, s32 = cos_ref[0], sin_ref[0]
    cs = jnp.concatenate([c32, s32, c32, s32], axis=1)
    lane = lax.broadcasted_iota(jnp.int32, cs.shape, 1)
    first_half = (lane % DIFF_DQK) < ROPE_HALF
    cos = jnp.where(first_half, cs, pltpu.roll(cs, ROPE_HALF, 1))
    sin = jnp.where(first_half, -pltpu.roll(cs, LANES - ROPE_HALF, 1), cs)

    def rope_store(out_ref, lo, scale):
        t = proj(lo, lo + DIFF_QK)
        for c in range(DIFF_QK // LANES):
            tc = t[:, c * LANES:(c + 1) * LANES]
            partner = jnp.where(first_half, pltpu.roll(tc, LANES - ROPE_HALF, 1),
                                pltpu.roll(tc, ROPE_HALF, 1))
            out_ref[0, :, c * LANES:(c + 1) * LANES] = ((tc * cos + partner * sin) * scale).astype(BF16)

    rope_store(dq_ref, _COL_DQ, DIFF_DQK ** -0.5 * LOG2E)
    rope_store(dk_ref, _COL_DK, 1.0)

    lr = proj(_COL_LR, _COL_GOG).astype(BF16)
    z = _dot(lr, gw_ref[...]) + gb_ref[...]
    log_sig = jnp.minimum(z, 0.0) - jnp.log1p(jnp.exp(-jnp.abs(z)))
    glog_ref[0] = log_sig * (LOG2E / GLA_GATE_NORM)

    gq_ref[0] = (proj(_COL_GQ, _COL_GK) * (GLA_DK ** -0.5)).astype(BF16)
    gk_ref[0] = proj(_COL_GK, _COL_GV).astype(BF16)
    gv_ref[0] = proj(_COL_GV, _COL_LR).astype(BF16)
    gog_ref[0] = proj(_COL_GOG, _COL_DQ).astype(BF16)


def _inproj(x, ada, pre_norm, cos, sin, w_in_t, gate_w, gate_b, later_weights, tm):
    bsz, seq, d = x.shape
    grid = (bsz, seq // tm)
    row = lambda b, i: (b, i, 0)
    const2 = lambda b, i: (0, 0)

    def out(width, dtype):
        return (pl.BlockSpec((1, tm, width), row), jax.ShapeDtypeStruct((bsz, seq, width), dtype))

    dvt = (pl.BlockSpec((1, DIFF_WIDTH, tm), lambda b, i: (b, 0, i)),
           jax.ShapeDtypeStruct((bsz, DIFF_WIDTH, seq), BF16))
    outs = [out(GLA_QK, BF16), out(GLA_QK, BF16), out(GLA_WIDTH, BF16), out(GLA_WIDTH, BF16),
            out(GLA_QK, F32), out(DIFF_QK, BF16), out(DIFF_QK, BF16), dvt]

    steps = grid[0] * grid[1]
    slab = lambda b, i: (b * grid[1] + i, 0)
    cast_in, cast_out = [], []
    for w in later_weights:
        rows, cols = w.shape
        assert rows % (steps * 16) == 0, (rows, steps)
        cast_in.append(pl.BlockSpec((rows // steps, cols), slab))
        cast_out.append((pl.BlockSpec((rows // steps, cols), slab),
                         jax.ShapeDtypeStruct((rows, cols), BF16)))
    res = pl.pallas_call(
        functools.partial(_inproj_kernel, n_cast=len(later_weights)),
        grid=grid,
        in_specs=[
            pl.BlockSpec((1, tm, d), row),
            pl.BlockSpec((bsz, N_ADA * d), const2),
            pl.BlockSpec((1, d), const2),
            pl.BlockSpec((1, tm, ROPE_HALF), row),
            pl.BlockSpec((1, tm, ROPE_HALF), row),
            pl.BlockSpec((_COL_END, d), const2, pipeline_mode=pl.Buffered(1)),
            pl.BlockSpec((GLA_GATE_RANK, GLA_QK), const2),
            pl.BlockSpec((1, GLA_QK), const2),
        ] + cast_in,
        out_specs=[o[0] for o in outs + cast_out],
        out_shape=[o[1] for o in outs + cast_out],
        scratch_shapes=[pltpu.VMEM((_COL_END, d), BF16)],
        compiler_params=pltpu.CompilerParams(
            dimension_semantics=("arbitrary", "arbitrary"), vmem_limit_bytes=VMEM_LIMIT),
        name="in_proj",
    )(x, ada, pre_norm, cos, sin, w_in_t, gate_w, gate_b, *later_weights)
    return res[:len(outs)], res[len(outs):]


def _head_stack(t, lane_head):
    return jnp.concatenate(
        [jnp.where(lane_head == h, t, jnp.zeros_like(t)) for h in range(GLA_HEADS)], axis=0)


def _gla_kernel(q_ref, k_ref, v_ref, g_ref, og_ref, gn_ref, o_ref, state_ref, *, chunks):
    C = GLA_CHUNK

    @pl.when(pl.program_id(1) == 0)
    def _():
        state_ref[...] = jnp.zeros_like(state_ref)

    row = lax.broadcasted_iota(jnp.int32, (C, C), 0)
    col = lax.broadcasted_iota(jnp.int32, (C, C), 1)
    cum_mat = (row >= col).astype(BF16)
    lane_head = lax.broadcasted_iota(jnp.int32, (C, GLA_QK), 1) // GLA_DK
    key = lax.broadcasted_iota(jnp.int32, (C, GLA_HEADS * C), 1) % C
    causal = lax.broadcasted_iota(jnp.int32, (C, GLA_HEADS * C), 0) >= key
    gn = gn_ref[...]
    zeros_v = jnp.zeros((C, GLA_DV), BF16)

    chunk_rows = [slice(c * C, (c + 1) * C) for c in range(chunks)]

    cum = []
    for rows in chunk_rows:
        g = g_ref[0, rows, :]
        g_hi = g.astype(BF16)
        g_lo = (g - g_hi.astype(F32)).astype(BF16)
        cum.append(_dot(cum_mat, g_hi) + _dot(cum_mat, g_lo))

    q_ins, scores, upds, decays = [], [], [], []
    for rows, b in zip(chunk_rows, cum):
        b_last = b[C - 1:C, :]
        b_mid = b[C // 2 - 1:C // 2, :]
        q = q_ref[0, rows, :].astype(F32)
        k = k_ref[0, rows, :].astype(F32)
        q_ins.append((q * jnp.exp2(b)).astype(BF16))
        q_mid = (q * jnp.exp2(b - b_mid)).astype(BF16)
        k_mid = (k * jnp.exp2(b_mid - b)).astype(BF16)
        k_out = (k * jnp.exp2(b_last - b)).astype(BF16)
        s = _dot_nt(q_mid, _head_stack(k_mid, lane_head))
        scores.append(jnp.where(causal, s, 0.0).astype(BF16))
        v_stack = jnp.concatenate(
            [v_ref[0, rows, h * GLA_DV:(h + 1) * GLA_DV] for h in range(GLA_HEADS)], axis=0)
        upds.append(_dot_tn(v_stack, _head_stack(k_out, lane_head)))
        decays.append(jnp.exp2(b_last))

    state = state_ref[...]
    zeros_s = jnp.zeros((GLA_DK, GLA_DV), BF16)
    bd_states = []
    for upd, decay in zip(upds, decays):
        s_nat = state.T.astype(BF16)
        bd_states.append(jnp.concatenate(
            [jnp.concatenate([s_nat[h * GLA_DK:(h + 1) * GLA_DK] if hh == h else zeros_s
                              for hh in range(GLA_HEADS)], axis=1)
             for h in range(GLA_HEADS)], axis=0))
        state = state * decay + upd
    state_ref[...] = state

    for rows, q_in, s, bd_state in zip(chunk_rows, q_ins, scores, bd_states):
        bd_v = jnp.concatenate(
            [jnp.concatenate([v_ref[0, rows, h * GLA_DV:(h + 1) * GLA_DV] if hh == h else zeros_v
                              for hh in range(GLA_HEADS)], axis=1)
             for h in range(GLA_HEADS)], axis=0)
        o_all = _dot(q_in, bd_state) + _dot(s, bd_v)
        for h in range(GLA_HEADS):
            hv = slice(h * GLA_DV, (h + 1) * GLA_DV)
            og = og_ref[0, rows, hv].astype(F32)
            o_ref[0, rows, hv] = (_rms(o_all[:, hv]) * gn * _silu(og)).astype(o_ref.dtype)


def _gla(gq, gk, gv, glog, gog, gla_norm, ts):
    bsz, seq, _ = gq.shape
    row = lambda b, i: (b, i, 0)
    return pl.pallas_call(
        functools.partial(_gla_kernel, chunks=ts // GLA_CHUNK),
        grid=(bsz, seq // ts),
        in_specs=[
            pl.BlockSpec((1, ts, GLA_QK), row),
            pl.BlockSpec((1, ts, GLA_QK), row),
            pl.BlockSpec((1, ts, GLA_WIDTH), row),
            pl.BlockSpec((1, ts, GLA_QK), row),
            pl.BlockSpec((1, ts, GLA_WIDTH), row),
            pl.BlockSpec((1, GLA_DV), lambda b, i: (0, 0)),
        ],
        out_specs=pl.BlockSpec((1, ts, GLA_WIDTH), row),
        out_shape=jax.ShapeDtypeStruct((bsz, seq, GLA_WIDTH), BF16),
        scratch_shapes=[pltpu.VMEM((GLA_DV, GLA_QK), F32)],
        compiler_params=pltpu.CompilerParams(
            dimension_semantics=("parallel", "arbitrary"), vmem_limit_bytes=VMEM_LIMIT),
        name="gla",
    )(gq, gk, gv, glog, gog, gla_norm)


def _diff_kernel(q_ref, k_ref, vt_ref, lq1_ref, lk1_ref, lq2_ref, lk2_ref, dn_ref, o_ref,
                 acc_ref, s_ref, *, tq, heads, q_blocks, lambda_init):
    step = pl.program_id(2)
    lane = lax.broadcasted_iota(jnp.int32, (tq, 2 * DIFF_DQK), 1)
    ones_rows = jnp.ones((_SUM_ROWS, tq), BF16)

    def head_cols(h, width):
        return slice(h * width, (h + 1) * width)

    def q_rows(a):
        return slice(a * tq, (a + 1) * tq)

    qs = []
    for a in range(q_blocks):
        qs.append([])
        for h in range(heads):
            q = q_ref[0, q_rows(a), head_cols(h, 2 * DIFF_DQK)]
            zero = jnp.zeros_like(q)
            qs[a].append(jnp.concatenate([jnp.where(lane < DIFF_DQK, q, zero),
                                          jnp.where(lane >= DIFF_DQK, q, zero)], axis=0))

    def score(slot, a, j, hs=None):
        for h in (range(heads) if hs is None else hs):
            kb = k_ref[0, j * tq:(j + 1) * tq, head_cols(h, 2 * DIFF_DQK)]
            s_ref[slot, a, h] = _dot_nt(kb, qs[a][h])

    def consume(slot, a, j, ms, masked, hs=None):
        out = list(ms)
        for h in (range(heads) if hs is None else hs):
            vtb = vt_ref[0, head_cols(h, DIFF_DV), j * tq:(j + 1) * tq]
            m_new, p = [], []
            for c in range(2 * tq // LANES):
                cols = slice(c * LANES, (c + 1) * LANES)
                st = s_ref[slot, a, h, :, cols]
                if masked:
                    key = lax.broadcasted_iota(jnp.int32, st.shape, 0)
                    qry = lax.broadcasted_iota(jnp.int32, st.shape, 1) + (c * LANES) % tq
                    st = jnp.where(key <= qry, st, -jnp.inf)
                mc = jnp.max(st, axis=0, keepdims=True)
                if ms[h] is not None:
                    mc = jnp.maximum(ms[h][:, cols], mc)
                p.append(jnp.exp2(st - mc).astype(BF16))
                m_new.append(mc)
            m_new = jnp.concatenate(m_new, axis=1)
            v_aug = jnp.concatenate([vtb, ones_rows], axis=0)
            pv = _dot(v_aug, jnp.concatenate(p, axis=1))
            if ms[h] is None:
                acc_ref[a, h] = pv
            else:
                acc_ref[a, h] = jnp.exp2(ms[h] - m_new) * acc_ref[a, h] + pv
            out[h] = m_new
        return out

    def finish(a):
        lam = (jnp.exp(jnp.sum(lq1_ref[...] * lk1_ref[...], axis=-1, keepdims=True))
               - jnp.exp(jnp.sum(lq2_ref[...] * lk2_ref[...], axis=-1, keepdims=True))
               + lambda_init)
        for h in range(heads):
            acc = acc_ref[a, h]
            o_all = acc[:DIFF_DV] * (1.0 / acc[DIFF_DV:DIFF_DV + 1])
            ot = o_all[:, :tq] - lam * o_all[:, tq:]
            ot = ot * lax.rsqrt(jnp.mean(ot * ot, axis=0, keepdims=True) + EPS)
            o_ref[0, q_rows(a), head_cols(h, DIFF_DV)] = (
                ot.T * dn_ref[...] * (1.0 - lambda_init)).astype(o_ref.dtype)

    def run(s):
        last = [s * q_blocks + a for a in range(q_blocks)]
        ms = [[None] * heads for _ in range(q_blocks)]
        for a in range(q_blocks):
            score(0, a, 0)
        for j in range(max(last) + 1):
            for a in range(q_blocks):
                if j > last[a]:
                    continue
                for h in range(heads):
                    if j < last[a]:
                        score((j + 1) % 2, a, j + 1, (h,))
                    ms[a] = consume(j % 2, a, j, ms[a], j == last[a], (h,))
                if j == last[a]:
                    finish(a)

    for s in range(k_ref.shape[1] // (tq * q_blocks)):
        pl.when(step == s)(functools.partial(run, s))


def _diff_attn(dq, dk, dvt, lq1, lk1, lq2, lk2, diff_norm, lambda_init, tq, heads, q_blocks):
    bsz, seq, _ = dq.shape
    vec = lambda n: pl.BlockSpec((1, n), lambda b, g, i: (0, 0))
    tqs = tq * q_blocks
    return pl.pallas_call(
        functools.partial(_diff_kernel, tq=tq, heads=heads, q_blocks=q_blocks,
                          lambda_init=lambda_init),
        grid=(bsz, DIFF_HEADS // heads, seq // tqs),
        in_specs=[
            pl.BlockSpec((1, tqs, heads * 2 * DIFF_DQK), lambda b, g, i: (b, i, g)),
            pl.BlockSpec((1, seq, heads * 2 * DIFF_DQK), lambda b, g, i: (b, 0, g)),
            pl.BlockSpec((1, heads * DIFF_DV, seq), lambda b, g, i: (b, g, 0)),
            vec(DIFF_DQK), vec(DIFF_DQK), vec(DIFF_DQK), vec(DIFF_DQK), vec(DIFF_DV),
        ],
        out_specs=pl.BlockSpec((1, tqs, heads * DIFF_DV), lambda b, g, i: (b, i, g)),
        out_shape=jax.ShapeDtypeStruct((bsz, seq, DIFF_WIDTH), BF16),
        scratch_shapes=[pltpu.VMEM((q_blocks, heads, DIFF_DV + _SUM_ROWS, 2 * tq), F32),
                        pltpu.VMEM((2, q_blocks, heads, tq, 2 * tq), F32)],
        compiler_params=pltpu.CompilerParams(
            dimension_semantics=("parallel", "parallel", "parallel"),
            vmem_limit_bytes=VMEM_LIMIT),
        name="diff_attn",
    )(dq, dk, dvt, lq1, lk1, lq2, lk2, diff_norm)


def _out_mlp_kernel(x_ref, go_ref, do_ref, ada_ref, pn_mix_ref, pre_mlp_ref, pn_mlp_ref,
                    wo_ref, wu_ref, wd_ref, o_ref, u_ref, *, ff_chunk, row_parts):
    tm = x_ref.shape[1]
    parts = [slice(p * tm // row_parts, (p + 1) * tm // row_parts) for p in range(row_parts)]
    gt_a, sh_m, sc_m, gt_m = _ada_rows(ada_ref, 2, 4)
    halves = [slice(p * tm // (2 * row_parts), (p + 1) * tm // (2 * row_parts))
              for p in range(2 * row_parts)]
    ys = [_dot(go_ref[0, r, :], wo_ref[:GLA_WIDTH, :]) + _dot(do_ref[0, r, :], wo_ref[GLA_WIDTH:, :])
          for r in halves]
    x1h = [x_ref[0, r, :] + gt_a * (_rms(y) * pn_mix_ref[...]) for r, y in zip(halves, ys)]
    hh = [(_rms(x1) * pre_mlp_ref[...] * (1.0 + sc_m) + sh_m).astype(BF16) for x1 in x1h]
    x1s = [jnp.concatenate(x1h[2 * p:2 * p + 2], axis=0) for p in range(row_parts)]
    hs = [jnp.concatenate(hh[2 * p:2 * p + 2], axis=0) for p in range(row_parts)]
    d_ff = wu_ref.shape[1]
    for f in range(d_ff // ff_chunk):
        cols = slice(f * ff_chunk, (f + 1) * ff_chunk)
        for r, h in zip(parts, hs):
            u = jnp.maximum(_dot(h, wu_ref[:, cols]), 0.0)
            u_ref[r, cols] = (u * u).astype(BF16)
    y2s = [_dot(u_ref[r, :], wd_ref[...]) for r in parts]
    for r, x1, y2 in zip(parts, x1s, y2s):
        o_ref[0, r, :] = x1 + gt_m * (_rms(y2) * pn_mlp_ref[...])


def _out_mlp(x, go, do, ada, post_mix, pre_mlp, post_mlp, w_out, w_up, w_down, tm, ff_chunk):
    bsz, seq, d = x.shape
    d_ff = w_up.shape[1]
    row = lambda b, i: (b, i, 0)
    const2 = lambda b, i: (0, 0)
    resident = functools.partial(pl.BlockSpec, index_map=const2, pipeline_mode=pl.Buffered(1))
    return pl.pallas_call(
        functools.partial(_out_mlp_kernel, ff_chunk=ff_chunk, row_parts=tm // 256),
        grid=(bsz, seq // tm),
        in_specs=[
            pl.BlockSpec((1, tm, d), row),
            pl.BlockSpec((1, tm, GLA_WIDTH), row),
            pl.BlockSpec((1, tm, DIFF_WIDTH), row),
            pl.BlockSpec((bsz, N_ADA * d), const2),
            pl.BlockSpec((1, d), const2),
            pl.BlockSpec((1, d), const2),
            pl.BlockSpec((1, d), const2),
            resident((d, d)),
            resident((d, d_ff)),
            resident((d_ff, d)),
        ],
        out_specs=pl.BlockSpec((1, tm, d), row),
        out_shape=jax.ShapeDtypeStruct((bsz, seq, d), F32),
        scratch_shapes=[pltpu.VMEM((tm, d_ff), BF16)],
        compiler_params=pltpu.CompilerParams(
            dimension_semantics=("parallel", "parallel"), vmem_limit_bytes=VMEM_LIMIT),
        name="out_mlp",
    )(x, go, do, ada, post_mix, pre_mlp, post_mlp, w_out, w_up, w_down)


def kernel(x, c, positions, ada_w, ada_b, pre_norm_mix, post_norm_mix, w_in, gla_gate_w, gla_gate_b, gla_norm, lambda_q1, lambda_k1, lambda_q2, lambda_k2, diff_norm, w_out, pre_norm_mlp, post_norm_mlp, w_up, w_down):
    depth = ada_w.shape[0]
    bsz, seq, d = x.shape
    vec = lambda t: t.reshape(1, -1)
    for l in range(depth):
        lambda_init = 0.8 - 0.6 * math.exp(-0.3 * l)
        ada, cos, sin = _ada_rope(c, ada_w[l], ada_b[l], positions)
        (gq, gk, gv, gog, glog, dq, dk, dvt), (w_out_b, w_up_b, w_down_b) = _inproj(
            x, ada, vec(pre_norm_mix[l]), cos, sin, w_in[l].T, gla_gate_w[l].astype(BF16),
            vec(gla_gate_b[l]), (w_out[l], w_up[l], w_down[l]), tm=1024)
        go = _gla(gq, gk, gv, glog, gog, vec(gla_norm[l]), ts=2048)
        do = _diff_attn(dq, dk, dvt, vec(lambda_q1[l]), vec(lambda_k1[l]), vec(lambda_q2[l]),
                        vec(lambda_k2[l]), vec(diff_norm[l]), lambda_init, tq=256, heads=2, q_blocks=4)
        x = _out_mlp(x, go, do, ada, vec(post_norm_mix[l]), vec(pre_norm_mlp[l]),
                     vec(post_norm_mlp[l]), w_out_b, w_up_b, w_down_b, tm=512, ff_chunk=1024)
    return x
```

```python
import functools
import math

import jax
import jax.numpy as jnp
from jax import lax
from jax.experimental import pallas as pl
from jax.experimental.pallas import tpu as pltpu

F32 = jnp.float32
BF16 = jnp.bfloat16

GLA_HEADS = 4
GLA_DK = 64
GLA_DV = 128
GLA_QK = GLA_HEADS * GLA_DK
GLA_WIDTH = GLA_HEADS * GLA_DV
GLA_GATE_RANK = 16
GLA_GATE_NORM = 16.0
GLA_CHUNK = 64
DIFF_HEADS = 4
DIFF_DQK = 64
DIFF_DV = 128
DIFF_QK = DIFF_HEADS * 2 * DIFF_DQK
DIFF_WIDTH = DIFF_HEADS * DIFF_DV
ROPE_THETA = 10000.0
EPS = 1e-6
N_ADA = 6

LANES = 128
ROPE_HALF = DIFF_DQK // 2
_SUM_ROWS = 16
LOG2E = math.log2(math.e)
VMEM_LIMIT = 56 * 1024 * 1024

_COL_GQ = 0
_COL_GK = _COL_GQ + GLA_QK
_COL_GV = _COL_GK + GLA_QK
_COL_LR = _COL_GV + GLA_WIDTH
_COL_GOG = _COL_LR + GLA_GATE_RANK
_COL_DQ = _COL_GOG + GLA_WIDTH
_COL_DK = _COL_DQ + DIFF_QK
_COL_DV = _COL_DK + DIFF_QK
_COL_END = _COL_DV + DIFF_WIDTH


def _dot(a, b):
    return jnp.dot(a, b, preferred_element_type=F32)


def _dot_nt(a, b):
    return lax.dot_general(a, b, (((1,), (1,)), ((), ())), preferred_element_type=F32)


def _dot_tn(a, b):
    return lax.dot_general(a, b, (((0,), (0,)), ((), ())), preferred_element_type=F32)


def _rms(t):
    return t * lax.rsqrt(jnp.mean(t * t, axis=-1, keepdims=True) + EPS)


def _silu(t):
    return t * (1.0 / (1.0 + jnp.exp(-t)))


def _ada_kernel(c_ref, w_ref, b_ref, pos_ref, freq_ref, o_ref, cos_ref, sin_ref):
    ca = _silu(c_ref[...]).astype(BF16)
    o_ref[...] = _dot(ca, w_ref[...].astype(BF16)) + b_ref[...]
    ang = pos_ref[...] * freq_ref[...]
    tr = ang.shape[0]
    for table_ref, t in ((cos_ref, jnp.cos(ang)), (sin_ref, jnp.sin(ang))):
        for g in range(LANES // ROPE_HALF):
            tg = t if g == 0 else pltpu.roll(t, LANES - g * ROPE_HALF, 1)
            table_ref[g * tr:(g + 1) * tr, :] = tg[:, :ROPE_HALF]


def _ada_rope(c, ada_w, ada_b, positions, steps=4):
    bsz, d = c.shape
    n = ada_w.shape[1]
    tn = n // steps
    per_row = LANES // ROPE_HALF
    inv_freq = 1.0 / (ROPE_THETA ** (jnp.arange(0, DIFF_DQK, 2, dtype=F32) / DIFF_DQK))
    freq = jnp.tile(inv_freq, per_row).reshape(1, LANES)
    tokens = positions.size
    tr = tokens // (steps * per_row)
    pos = positions.astype(F32).reshape(steps, per_row, tr).transpose(0, 2, 1)
    pos = jnp.repeat(pos.reshape(steps * tr, per_row), ROPE_HALF, axis=1)
    dense = pl.BlockSpec((tr, LANES), lambda j: (j, 0))
    table = (pl.BlockSpec((tr * per_row, ROPE_HALF), lambda j: (j, 0)),
             jax.ShapeDtypeStruct((tokens, ROPE_HALF), F32))
    ada, cos, sin = pl.pallas_call(
        _ada_kernel,
        grid=(steps,),
        in_specs=[
            pl.BlockSpec((bsz, d), lambda j: (0, 0)),
            pl.BlockSpec((d, tn), lambda j: (0, j)),
            pl.BlockSpec((1, tn), lambda j: (0, j)),
            dense,
            pl.BlockSpec((1, LANES), lambda j: (0, 0)),
        ],
        out_specs=[pl.BlockSpec((bsz, tn), lambda j: (0, j)), table[0], table[0]],
        out_shape=[jax.ShapeDtypeStruct((bsz, n), F32), table[1], table[1]],
        compiler_params=pltpu.CompilerParams(
            dimension_semantics=("parallel",), vmem_limit_bytes=VMEM_LIMIT),
        name="ada_ln",
    )(c, ada_w, ada_b.reshape(1, n), pos, freq)
    shape = positions.shape + (ROPE_HALF,)
    return ada, cos.reshape(shape), sin.reshape(shape)


def _ada_rows(ada_ref, first, count):
    b = pl.program_id(0)
    d = ada_ref.shape[1] // N_ADA
    return [ada_ref[pl.ds(b, 1), n * d:(n + 1) * d] for n in range(first, first + count)]


def _inproj_kernel(x_ref, ada_ref, pn_ref, cos_ref, sin_ref, wt_ref, gw_ref, gb_ref, *rest, n_cast):
    cast_src, rest = rest[:n_cast], rest[n_cast:]
    gq_ref, gk_ref, gv_ref, gog_ref, glog_ref, dq_ref, dk_ref, dvt_ref = rest[:8]
    wb_ref = rest[-1]
    for src, dst in zip(cast_src, rest[8:8 + n_cast]):
        dst[...] = src[...].astype(dst.dtype)

    @pl.when((pl.program_id(0) == 0) & (pl.program_id(1) == 0))
    def _():
        wb_ref[...] = wt_ref[...].astype(BF16)

    x = x_ref[0]
    shift, scale = _ada_rows(ada_ref, 0, 2)
    h = _rms(x) * pn_ref[...] * (1.0 + scale) + shift
    hb = h.astype(BF16)

    def proj(lo, hi):
        return _dot_nt(hb, wb_ref[lo:hi, :])

    dvt_ref[0] = proj(_COL_DV, _COL_END).T.astype(BF16)

    c32, s32 = cos_ref[0], sin_ref[0]
    cs = jnp.concatenate([c32, s32, c32, s32], axis=1)
    lane = lax.broadcasted_iota(jnp.int32, cs.shape, 1)
    first_half = (lane % DIFF_DQK) < ROPE_HALF
    cos = jnp.where(first_half, cs, pltpu.roll(cs, ROPE_HALF, 1))
    sin = jnp.where(first_half, -pltpu.roll(cs, LANES - ROPE_HALF, 1), cs)

    def rope_store(out_ref, lo, scale):
        t = proj(lo, lo + DIFF_QK)
        for c in range(DIFF_QK // LANES):
            tc = t[:, c * LANES:(c + 1) * LANES]
            partner = jnp.where(first_half, pltpu.roll(tc, LANES - ROPE_HALF, 1),
                                pltpu.roll(tc, ROPE_HALF, 1))
            out_ref[0, :, c * LANES:(c + 1) * LANES] = ((tc * cos + partner * sin) * scale).astype(BF16)

    rope_store(dq_ref, _COL_DQ, DIFF_DQK ** -0.5 * LOG2E)
    rope_store(dk_ref, _COL_DK, 1.0)

    lr = proj(_COL_LR, _COL_GOG).astype(BF16)
    z = _dot(lr, gw_ref[...]) + gb_ref[...]
    log_sig = jnp.minimum(z, 0.0) - jnp.log1p(jnp.exp(-jnp.abs(z)))
    glog_ref[0] = log_sig * (LOG2E / GLA_GATE_NORM)

    gq_ref[0] = (proj(_COL_GQ, _COL_GK) * (GLA_DK ** -0.5)).astype(BF16)
    gk_ref[0] = proj(_COL_GK, _COL_GV).astype(BF16)
    gv_ref[0] = proj(_COL_GV, _COL_LR).astype(BF16)
    gog_ref[0] = proj(_COL_GOG, _COL_DQ).astype(BF16)


def _inproj(x, ada, pre_norm, cos, sin, w_in_t, gate_w, gate_b, later_weights, tm):
    bsz, seq, d = x.shape
    grid = (bsz, seq // tm)
    row = lambda b, i: (b, i, 0)
    const2 = lambda b, i: (0, 0)

    def out(width, dtype):
        return (pl.BlockSpec((1, tm, width), row), jax.ShapeDtypeStruct((bsz, seq, width), dtype))

    dvt = (pl.BlockSpec((1, DIFF_WIDTH, tm), lambda b, i: (b, 0, i)),
           jax.ShapeDtypeStruct((bsz, DIFF_WIDTH, seq), BF16))
    outs = [out(GLA_QK, BF16), out(GLA_QK, BF16), out(GLA_WIDTH, BF16), out(GLA_WIDTH, BF16),
            out(GLA_QK, F32), out(DIFF_QK, BF16), out(DIFF_QK, BF16), dvt]

    steps = grid[0] * grid[1]
    slab = lambda b, i: (b * grid[1] + i, 0)
    cast_in, cast_out = [], []
    for w in later_weights:
        rows, cols = w.shape
        assert rows % (steps * 16) == 0, (rows, steps)
        cast_in.append(pl.BlockSpec((rows // steps, cols), slab))
        cast_out.append((pl.BlockSpec((rows // steps, cols), slab),
                         jax.ShapeDtypeStruct((rows, cols), BF16)))
    res = pl.pallas_call(
        functools.partial(_inproj_kernel, n_cast=len(later_weights)),
        grid=grid,
        in_specs=[
            pl.BlockSpec((1, tm, d), row),
            pl.BlockSpec((bsz, N_ADA * d), const2),
            pl.BlockSpec((1, d), const2),
            pl.BlockSpec((1, tm, ROPE_HALF), row),
            pl.BlockSpec((1, tm, ROPE_HALF), row),
            pl.BlockSpec((_COL_END, d), const2, pipeline_mode=pl.Buffered(1)),
            pl.BlockSpec((GLA_GATE_RANK, GLA_QK), const2),
            pl.BlockSpec((1, GLA_QK), const2),
        ] + cast_in,
        out_specs=[o[0] for o in outs + cast_out],
        out_shape=[o[1] for o in outs + cast_out],
        scratch_shapes=[pltpu.VMEM((_COL_END, d), BF16)],
        compiler_params=pltpu.CompilerParams(
            dimension_semantics=("arbitrary", "arbitrary"), vmem_limit_bytes=VMEM_LIMIT),
        name="in_proj",
    )(x, ada, pre_norm, cos, sin, w_in_t, gate_w, gate_b, *later_weights)
    return res[:len(outs)], res[len(outs):]


def _head_stack(t, lane_head):
    return jnp.concatenate(
        [jnp.where(lane_head == h, t, jnp.zeros_like(t)) for h in range(GLA_HEADS)], axis=0)


def _gla_kernel(q_ref, k_ref, v_ref, g_ref, og_ref, gn_ref, o_ref, state_ref, *, chunks):
    C = GLA_CHUNK

    @pl.when(pl.program_id(1) == 0)
    def _():
        state_ref[...] = jnp.zeros_like(state_ref)

    row = lax.broadcasted_iota(jnp.int32, (C, C), 0)
    col = lax.broadcasted_iota(jnp.int32, (C, C), 1)
    cum_mat = (row >= col).astype(BF16)
    lane_head = lax.broadcasted_iota(jnp.int32, (C, GLA_QK), 1) // GLA_DK
    key = lax.broadcasted_iota(jnp.int32, (C, GLA_HEADS * C), 1) % C
    causal = lax.broadcasted_iota(jnp.int32, (C, GLA_HEADS * C), 0) >= key
    gn = gn_ref[...]
    zeros_v = jnp.zeros((C, GLA_DV), BF16)

    chunk_rows = [slice(c * C, (c + 1) * C) for c in range(chunks)]

    cum = []
    for rows in chunk_rows:
        g = g_ref[0, rows, :]
        g_hi = g.astype(BF16)
        g_lo = (g - g_hi.astype(F32)).astype(BF16)
        cum.append(_dot(cum_mat, g_hi) + _dot(cum_mat, g_lo))

    q_ins, scores, upds, decays = [], [], [], []
    for rows, b in zip(chunk_rows, cum):
        b_last = b[C - 1:C, :]
        b_mid = b[C // 2 - 1:C // 2, :]
        q = q_ref[0, rows, :].astype(F32)
        k = k_ref[0, rows, :].astype(F32)
        q_ins.append((q * jnp.exp2(b)).astype(BF16))
        q_mid = (q * jnp.exp2(b - b_mid)).astype(BF16)
        k_mid = (k * jnp.exp2(b_mid - b)).astype(BF16)
        k_out = (k * jnp.exp2(b_last - b)).astype(BF16)
        s = _dot_nt(q_mid, _head_stack(k_mid, lane_head))
        scores.append(jnp.where(causal, s, 0.0).astype(BF16))
        v_stack = jnp.concatenate(
            [v_ref[0, rows, h * GLA_DV:(h + 1) * GLA_DV] for h in range(GLA_HEADS)], axis=0)
        upds.append(_dot_tn(v_stack, _head_stack(k_out, lane_head)))
        decays.append(jnp.exp2(b_last))

    state = state_ref[...]
    zeros_s = jnp.zeros((GLA_DK, GLA_DV), BF16)
    bd_states = []
    for upd, decay in zip(upds, decays):
        s_nat = state.T.astype(BF16)
        bd_states.append(jnp.concatenate(
            [jnp.concatenate([s_nat[h * GLA_DK:(h + 1) * GLA_DK] if hh == h else zeros_s
                              for hh in range(GLA_HEADS)], axis=1)
             for h in range(GLA_HEADS)], axis=0))
        state = state * decay + upd
    state_ref[...] = state

    for rows, q_in, s, bd_state in zip(chunk_rows, q_ins, scores, bd_states):
        bd_v = jnp.concatenate(
            [jnp.concatenate([v_ref[0, rows, h * GLA_DV:(h + 1) * GLA_DV] if hh == h else zeros_v
                              for hh in range(GLA_HEADS)], axis=1)
             for h in range(GLA_HEADS)], axis=0)
        o_all = _dot(q_in, bd_state) + _dot(s, bd_v)
        for h in range(GLA_HEADS):
            hv = slice(h * GLA_DV, (h + 1) * GLA_DV)
            og = og_ref[0, rows, hv].astype(F32)
            o_ref[0, rows, hv] = (_rms(o_all[:, hv]) * gn * _silu(og)).astype(o_ref.dtype)


def _gla(gq, gk, gv, glog, gog, gla_norm, ts):
    bsz, seq, _ = gq.shape
    row = lambda b, i: (b, i, 0)
    return pl.pallas_call(
        functools.partial(_gla_kernel, chunks=ts // GLA_CHUNK),
        grid=(bsz, seq // ts),
        in_specs=[
            pl.BlockSpec((1, ts, GLA_QK), row),
            pl.BlockSpec((1, ts, GLA_QK), row),
            pl.BlockSpec((1, ts, GLA_WIDTH), row),
            pl.BlockSpec((1, ts, GLA_QK), row),
            pl.BlockSpec((1, ts, GLA_WIDTH), row),
            pl.BlockSpec((1, GLA_DV), lambda b, i: (0, 0)),
        ],
        out_specs=pl.BlockSpec((1, ts, GLA_WIDTH), row),
        out_shape=jax.ShapeDtypeStruct((bsz, seq, GLA_WIDTH), BF16),
        scratch_shapes=[pltpu.VMEM((GLA_DV, GLA_QK), F32)],
        compiler_params=pltpu.CompilerParams(
            dimension_semantics=("parallel", "arbitrary"), vmem_limit_bytes=VMEM_LIMIT),
        name="gla",
    )(gq, gk, gv, glog, gog, gla_norm)


def _diff_kernel(q_ref, k_ref, vt_ref, lq1_ref, lk1_ref, lq2_ref, lk2_ref, dn_ref, o_ref,
                 acc_ref, s_ref, *, tq, heads, q_blocks, lambda_init):
    step = pl.program_id(2)
    lane = lax.broadcasted_iota(jnp.int32, (tq, 2 * DIFF_DQK), 1)
    ones_rows = jnp.ones((_SUM_ROWS, tq), BF16)

    def head_cols(h, width):
        return slice(h * width, (h + 1) * width)

    def q_rows(a):
        return slice(a * tq, (a + 1) * tq)

    qs = []
    for a in range(q_blocks):
        qs.append([])
        for h in range(heads):
            q = q_ref[0, q_rows(a), head_cols(h, 2 * DIFF_DQK)]
            zero = jnp.zeros_like(q)
            qs[a].append(jnp.concatenate([jnp.where(lane < DIFF_DQK, q, zero),
                                          jnp.where(lane >= DIFF_DQK, q, zero)], axis=0))

    def score(slot, a, j, hs=None):
        for h in (range(heads) if hs is None else hs):
            kb = k_ref[0, j * tq:(j + 1) * tq, head_cols(h, 2 * DIFF_DQK)]
            s_ref[slot, a, h] = _dot_nt(kb, qs[a][h])

    def consume(slot, a, j, ms, masked, hs=None):
        out = list(ms)
        for h in (range(heads) if hs is None else hs):
            vtb = vt_ref[0, head_cols(h, DIFF_DV), j * tq:(j + 1) * tq]
            m_new, p = [], []
            for c in range(2 * tq // LANES):
                cols = slice(c * LANES, (c + 1) * LANES)
                st = s_ref[slot, a, h, :, cols]
                if masked:
                    key = lax.broadcasted_iota(jnp.int32, st.shape, 0)
                    qry = lax.broadcasted_iota(jnp.int32, st.shape, 1) + (c * LANES) % tq
                    st = jnp.where(key <= qry, st, -jnp.inf)
                mc = jnp.max(st, axis=0, keepdims=True)
                if ms[h] is not None:
                    mc = jnp.maximum(ms[h][:, cols], mc)
                p.append(jnp.exp2(st - mc).astype(BF16))
                m_new.append(mc)
            m_new = jnp.concatenate(m_new, axis=1)
            v_aug = jnp.concatenate([vtb, ones_rows], axis=0)
            pv = _dot(v_aug, jnp.concatenate(p, axis=1))
            if ms[h] is None:
                acc_ref[a, h] = pv
            else:
                acc_ref[a, h] = jnp.exp2(ms[h] - m_new) * acc_ref[a, h] + pv
            out[h] = m_new
        return out

    def finish(a):
        lam = (jnp.exp(jnp.sum(lq1_ref[...] * lk1_ref[...], axis=-1, keepdims=True))
               - jnp.exp(jnp.sum(lq2_ref[...] * lk2_ref[...], axis=-1, keepdims=True))
               + lambda_init)
        for h in range(heads):
            acc = acc_ref[a, h]
            o_all = acc[:DIFF_DV] * (1.0 / acc[DIFF_DV:DIFF_DV + 1])
            ot = o_all[:, :tq] - lam * o_all[:, tq:]
            ot = ot * lax.rsqrt(jnp.mean(ot * ot, axis=0, keepdims=True) + EPS)
            o_ref[0, q_rows(a), head_cols(h, DIFF_DV)] = (
                ot.T * dn_ref[...] * (1.0 - lambda_init)).astype(o_ref.dtype)

    def run(s):
        last = [s * q_blocks + a for a in range(q_blocks)]
        ms = [[None] * heads for _ in range(q_blocks)]
        for a in range(q_blocks):
            score(0, a, 0)
        for j in range(max(last) + 1):
            for a in range(q_blocks):
                if j > last[a]:
                    continue
                for h in range(heads):
                    if j < last[a]:
                        score((j + 1) % 2, a, j + 1, (h,))
                    ms[a] = consume(j % 2, a, j, ms[a], j == last[a], (h,))
                if j == last[a]:
                    finish(a)

    for s in range(k_ref.shape[1] // (tq * q_blocks)):
        pl.when(step == s)(functools.partial(run, s))


def _diff_attn(dq, dk, dvt, lq1, lk1, lq2, lk2, diff_norm, lambda_init, tq, heads, q_blocks):
    bsz, seq, _ = dq.shape
    vec = lambda n: pl.BlockSpec((1, n), lambda b, g, i: (0, 0))
    tqs = tq * q_blocks
    return pl.pallas_call(
        functools.partial(_diff_kernel, tq=tq, heads=heads, q_blocks=q_blocks,
                          lambda_init=lambda_init),
        grid=(bsz, DIFF_HEADS // heads, seq // tqs),
        in_specs=[
            pl.BlockSpec((1, tqs, heads * 2 * DIFF_DQK), lambda b, g, i: (b, i, g)),
            pl.BlockSpec((1, seq, heads * 2 * DIFF_DQK), lambda b, g, i: (b, 0, g)),
            pl.BlockSpec((1, heads * DIFF_DV, seq), lambda b, g, i: (b, g, 0)),
            vec(DIFF_DQK), vec(DIFF_DQK), vec(DIFF_DQK), vec(DIFF_DQK), vec(DIFF_DV),
        ],
        out_specs=pl.BlockSpec((1, tqs, heads * DIFF_DV), lambda b, g, i: (b, i, g)),
        out_shape=jax.ShapeDtypeStruct((bsz, seq, DIFF_WIDTH), BF16),
        scratch_shapes=[pltpu.VMEM((q_blocks, heads, DIFF_DV + _SUM_ROWS, 2 * tq), F32),
                        pltpu.VMEM((2, q_blocks, heads, tq, 2 * tq), F32)],
        compiler_params=pltpu.CompilerParams(
            dimension_semantics=("parallel", "parallel", "parallel"),
            vmem_limit_bytes=VMEM_LIMIT),
        name="diff_attn",
    )(dq, dk, dvt, lq1, lk1, lq2, lk2, diff_norm)


def _out_mlp_kernel(x_ref, go_ref, do_ref, ada_ref, pn_mix_ref, pre_mlp_ref, pn_mlp_ref,
                    wo_hbm, wu_hbm, wd_hbm, o_ref, u_ref, wo_ref, wu_ref, wd_ref, sem,
                    *, ff_chunk, row_parts):
    first = (pl.program_id(0) == 0) & (pl.program_id(1) == 0)
    copies = [pltpu.make_async_copy(src, dst, sem.at[n])
              for n, (src, dst) in enumerate(((wo_hbm, wo_ref), (wu_hbm, wu_ref), (wd_hbm, wd_ref)))]

    def tile(wait):
        _out_mlp_tile(x_ref, go_ref, do_ref, ada_ref, pn_mix_ref, pre_mlp_ref, pn_mlp_ref,
                      wo_ref, wu_ref, wd_ref, o_ref, u_ref, ff_chunk, row_parts, wait)

    @pl.when(first)
    def _():
        for cp in copies:
            cp.start()
        tile(lambda n: copies[n].wait())

    @pl.when(jnp.logical_not(first))
    def _():
        tile(lambda n: None)


def _out_mlp_tile(x_ref, go_ref, do_ref, ada_ref, pn_mix_ref, pre_mlp_ref, pn_mlp_ref,
                  wo_ref, wu_ref, wd_ref, o_ref, u_ref, ff_chunk, row_parts, wait):
    tm = x_ref.shape[1]
    parts = [slice(p * tm // row_parts, (p + 1) * tm // row_parts) for p in range(row_parts)]
    gt_a, sh_m, sc_m, gt_m = _ada_rows(ada_ref, 2, 4)
    halves = [slice(p * tm // (2 * row_parts), (p + 1) * tm // (2 * row_parts))
              for p in range(2 * row_parts)]
    wait(0)
    ys = [_dot(go_ref[0, r, :], wo_ref[:GLA_WIDTH, :]) + _dot(do_ref[0, r, :], wo_ref[GLA_WIDTH:, :])
          for r in halves]
    x1h = [x_ref[0, r, :] + gt_a * (_rms(y) * pn_mix_ref[...]) for r, y in zip(halves, ys)]
    hh = [(_rms(x1) * pre_mlp_ref[...] * (1.0 + sc_m) + sh_m).astype(BF16) for x1 in x1h]
    x1s = [jnp.concatenate(x1h[2 * p:2 * p + 2], axis=0) for p in range(row_parts)]
    hs = [jnp.concatenate(hh[2 * p:2 * p + 2], axis=0) for p in range(row_parts)]
    d_ff = wu_ref.shape[1]
    wait(1)
    for f in range(d_ff // ff_chunk):
        cols = slice(f * ff_chunk, (f + 1) * ff_chunk)
        for r, h in zip(parts, hs):
            u = jnp.maximum(_dot(h, wu_ref[:, cols]), 0.0)
            u_ref[r, cols] = (u * u).astype(BF16)
    wait(2)
    y2s = [_dot(u_ref[r, :], wd_ref[...]) for r in parts]
    for r, x1, y2 in zip(parts, x1s, y2s):
        o_ref[0, r, :] = x1 + gt_m * (_rms(y2) * pn_mlp_ref[...])


def _out_mlp(x, go, do, ada, post_mix, pre_mlp, post_mlp, w_out, w_up, w_down, tm, ff_chunk):
    bsz, seq, d = x.shape
    d_ff = w_up.shape[1]
    row = lambda b, i: (b, i, 0)
    const2 = lambda b, i: (0, 0)
    in_hbm = pl.BlockSpec(memory_space=pl.ANY)
    return pl.pallas_call(
        functools.partial(_out_mlp_kernel, ff_chunk=ff_chunk, row_parts=tm // 256),
        grid=(bsz, seq // tm),
        in_specs=[
            pl.BlockSpec((1, tm, d), row),
            pl.BlockSpec((1, tm, GLA_WIDTH), row),
            pl.BlockSpec((1, tm, DIFF_WIDTH), row),
            pl.BlockSpec((bsz, N_ADA * d), const2),
            pl.BlockSpec((1, d), const2),
            pl.BlockSpec((1, d), const2),
            pl.BlockSpec((1, d), const2),
            in_hbm, in_hbm, in_hbm,
        ],
        out_specs=pl.BlockSpec((1, tm, d), row),
        out_shape=jax.ShapeDtypeStruct((bsz, seq, d), F32),
        scratch_shapes=[pltpu.VMEM((tm, d_ff), BF16), pltpu.VMEM((d, d), BF16),
                        pltpu.VMEM((d, d_ff), BF16), pltpu.VMEM((d_ff, d), BF16),
                        pltpu.SemaphoreType.DMA((3,))],
        compiler_params=pltpu.CompilerParams(
            dimension_semantics=("arbitrary", "arbitrary"), vmem_limit_bytes=VMEM_LIMIT),
        name="out_mlp",
    )(x, go, do, ada, post_mix, pre_mlp, post_mlp, w_out, w_up, w_down)


def kernel(x, c, positions, ada_w, ada_b, pre_norm_mix, post_norm_mix, w_in, gla_gate_w, gla_gate_b, gla_norm, lambda_q1, lambda_k1, lambda_q2, lambda_k2, diff_norm, w_out, pre_norm_mlp, post_norm_mlp, w_up, w_down):
    depth = ada_w.shape[0]
    bsz, seq, d = x.shape
    vec = lambda t: t.reshape(1, -1)
    for l in range(depth):
        lambda_init = 0.8 - 0.6 * math.exp(-0.3 * l)
        ada, cos, sin = _ada_rope(c, ada_w[l], ada_b[l], positions)
        (gq, gk, gv, gog, glog, dq, dk, dvt), (w_out_b, w_up_b, w_down_b) = _inproj(
            x, ada, vec(pre_norm_mix[l]), cos, sin, w_in[l].T, gla_gate_w[l].astype(BF16),
            vec(gla_gate_b[l]), (w_out[l], w_up[l], w_down[l]), tm=1024)
        go = _gla(gq, gk, gv, glog, gog, vec(gla_norm[l]), ts=1024)
        do = _diff_attn(dq, dk, dvt, vec(lambda_q1[l]), vec(lambda_k1[l]), vec(lambda_q2[l]),
                        vec(lambda_k2[l]), vec(diff_norm[l]), lambda_init, tq=256, heads=2, q_blocks=4)
        x = _out_mlp(x, go, do, ada, vec(post_norm_mix[l]), vec(pre_norm_mlp[l]),
                     vec(post_norm_mlp[l]), w_out_b, w_up_b, w_down_b, tm=512, ff_chunk=1024)
    return x
```
